```python
import jax, jax.numpy as jnp
from jax import lax
import numpy as np

D_MODEL = 4096
BATCH = 2
SEQ = 4096
DEPTH = 4
DEC_BATCH = 8
DEC_SEQ = 32
PAST_LEN = 1024

CHUNK = 64
Q_BLOCK = 128
N_MIXERS = 3
NORM_EPS = 1e-6

FOX_HEADS = 32
FOX_HEAD_DIM = D_MODEL // FOX_HEADS
FOX_W = FOX_HEADS * FOX_HEAD_DIM
FOX_COLS = 4 * FOX_W + FOX_HEADS

ML_HEADS = 8
ML_QK_DIM = D_MODEL // (2 * ML_HEADS)
ML_V_DIM = D_MODEL // ML_HEADS
ML_QK_W = ML_HEADS * ML_QK_DIM
ML_V_W = ML_HEADS * ML_V_DIM
ML_COLS = 2 * ML_QK_W + 3 * ML_V_W + 2 * ML_HEADS

GDN_K_HEADS = 16
GDN_V_HEADS = 32
GDN_HEAD_DIM = D_MODEL // GDN_V_HEADS
GDN_K_W = GDN_K_HEADS * GDN_HEAD_DIM
GDN_V_W = GDN_V_HEADS * GDN_HEAD_DIM
GDN_CONV_W = 4
GDN_CONV_CH = 2 * GDN_K_W + GDN_V_W
GDN_COLS = GDN_CONV_CH + GDN_V_W + 2 * GDN_V_HEADS

N_FOX = (DEPTH + N_MIXERS - 1) // N_MIXERS
N_MLSTM = (DEPTH + N_MIXERS - 2) // N_MIXERS
N_GDN = (DEPTH + N_MIXERS - 3) // N_MIXERS

kernel_name = 'hybrid_fox_mlstm_gdn_stream_step'

F32 = jnp.float32


def rms_norm(x, g):
    xf = x.astype(F32)
    y = xf * lax.rsqrt(jnp.mean(xf * xf, axis=-1, keepdims=True) + NORM_EPS)
    return (y * g.astype(F32)).astype(x.dtype)


def l2_norm(x):
    xf = x.astype(F32)
    return xf * lax.rsqrt(jnp.sum(xf * xf, axis=-1, keepdims=True) + NORM_EPS)


def to_chunks(a, n_chunks):
    return jnp.swapaxes(a.reshape((a.shape[0], n_chunks, CHUNK) + a.shape[2:]), 0, 1)


def from_chunks(a):
    a = jnp.swapaxes(a, 0, 1)
    return a.reshape((a.shape[0], a.shape[1] * a.shape[2]) + a.shape[3:])


def fox_project(h, w_in, b_f, q_gain, k_gain):
    B, T, _ = h.shape
    proj = jnp.einsum('btd,dc->btc', h, w_in)
    q, k, v, z, f = jnp.split(proj, [FOX_W, 2 * FOX_W, 3 * FOX_W, 4 * FOX_W], axis=-1)
    q = rms_norm(q.reshape(B, T, FOX_HEADS, FOX_HEAD_DIM), q_gain)
    k = rms_norm(k.reshape(B, T, FOX_HEADS, FOX_HEAD_DIM), k_gain)
    v = v.reshape(B, T, FOX_HEADS, FOX_HEAD_DIM)
    log_f = jax.nn.log_sigmoid((f + b_f).astype(F32))
    return q, k, v, z, log_f


def fox_attend(q, k, v, cum_q, cum_k, q_pos, k_pos):
    s = jnp.einsum('bqhd,bkhd->bhqk', q, k, preferred_element_type=F32) * (FOX_HEAD_DIM ** -0.5)
    s = s + jnp.swapaxes(cum_q, 1, 2)[..., :, None] - jnp.swapaxes(cum_k, 1, 2)[..., None, :]
    s = jnp.where(k_pos[None, :] <= q_pos[:, None], s, -jnp.inf)
    p = jax.nn.softmax(s, axis=-1)
    return jnp.einsum('bhqk,bkhd->bqhd', p.astype(v.dtype), v)


def fox_output(o, z, w_out):
    B, T = o.shape[:2]
    return jnp.einsum('btc,cd->btd', o.reshape(B, T, FOX_W) * jax.nn.silu(z), w_out)


def fox_prompt(h, w_in, b_f, q_gain, k_gain, w_out):
    B, T, _ = h.shape
    q, k, v, z, log_f = fox_project(h, w_in, b_f, q_gain, k_gain)
    cum = jnp.cumsum(log_f, axis=1)
    nb = T // Q_BLOCK
    pos = jnp.arange(T)
    q_blk = jnp.swapaxes(q.reshape(B, nb, Q_BLOCK, FOX_HEADS, FOX_HEAD_DIM), 0, 1)
    c_blk = jnp.swapaxes(cum.reshape(B, nb, Q_BLOCK, FOX_HEADS), 0, 1)
    p_blk = pos.reshape(nb, Q_BLOCK)
    o = lax.map(lambda a: fox_attend(a[0], k, v, a[1], cum, a[2], pos), (q_blk, c_blk, p_blk))
    o = jnp.swapaxes(o, 0, 1).reshape(B, T, FOX_HEADS, FOX_HEAD_DIM)
    return fox_output(o, z, w_out), k, v, log_f


def fox_sample(h, cache_k, cache_v, cache_lf, w_in, b_f, q_gain, k_gain, w_out):
    T = h.shape[1]
    P = cache_k.shape[1]
    q, k, v, z, log_f = fox_project(h, w_in, b_f, q_gain, k_gain)
    k_all = jnp.concatenate([cache_k.astype(k.dtype), k], axis=1)
    v_all = jnp.concatenate([cache_v.astype(v.dtype), v], axis=1)
    cum = jnp.cumsum(jnp.concatenate([cache_lf.astype(F32), log_f], axis=1), axis=1)
    o = fox_attend(q, k_all, v_all, cum[:, P:], cum, P + jnp.arange(T), jnp.arange(P + T))
    return fox_output(o, z, w_out), k, v, log_f


def mlstm_project(h, w_in, b_i, b_f):
    B, T, _ = h.shape
    proj = jnp.einsum('btd,dc->btc', h, w_in)
    i1 = ML_QK_W
    i2 = 2 * ML_QK_W
    i3 = i2 + ML_V_W
    i4 = i3 + ML_V_W
    i5 = i4 + ML_V_W
    i6 = i5 + ML_HEADS
    q, k, v, o, z, ig, fg = jnp.split(proj, [i1, i2, i3, i4, i5, i6], axis=-1)
    q = q.reshape(B, T, ML_HEADS, ML_QK_DIM).astype(F32)
    k = k.reshape(B, T, ML_HEADS, ML_QK_DIM).astype(F32) * (ML_QK_DIM ** -0.5)
    v = v.reshape(B, T, ML_HEADS, ML_V_DIM).astype(F32)
    log_i = (ig + b_i).astype(F32)
    log_f = jax.nn.log_sigmoid((fg + b_f).astype(F32))
    return q, k, v, log_i, log_f, o, z


def mlstm_chunk(carry, xs):
    c, n, m = carry
    q, k, v, log_i, log_f = xs
    L = q.shape[1]
    b = jnp.swapaxes(jnp.cumsum(log_f, axis=1), 1, 2)
    li = jnp.swapaxes(log_i, 1, 2)
    causal = jnp.tril(jnp.ones((L, L), dtype=bool))
    d = jnp.where(causal, b[..., :, None] - b[..., None, :] + li[..., None, :], -jnp.inf)
    inter = b + m[..., None]
    m_t = jnp.maximum(inter, jnp.max(d, axis=-1))
    w = jnp.exp(d - m_t[..., None]) * jnp.einsum('blhd,bshd->bhls', q, k)
    inter_w = jnp.exp(inter - m_t)
    num = jnp.einsum('bhls,bshv->bhlv', w, v) + inter_w[..., None] * jnp.einsum('bhvd,blhd->bhlv', c, q)
    den = jnp.sum(w, axis=-1) + inter_w * jnp.einsum('bhd,blhd->bhl', n, q)
    hid = num / jnp.maximum(jnp.abs(den), jnp.exp(-m_t))[..., None]
    b_last = b[..., -1]
    g = b_last[..., None] - b + li
    m_new = jnp.maximum(b_last + m, jnp.max(g, axis=-1))
    wk = jnp.exp(g - m_new[..., None])
    carry_w = jnp.exp(b_last + m - m_new)
    c_new = carry_w[..., None, None] * c + jnp.einsum('bhs,bshv,bshd->bhvd', wk, v, k)
    n_new = carry_w[..., None] * n + jnp.einsum('bhs,bshd->bhd', wk, k)
    return (c_new, n_new, m_new), jnp.swapaxes(hid, 1, 2)


def mlstm_output(hid, o, z, gain, w_out):
    B, T = hid.shape[:2]
    hn = rms_norm(hid, gain.reshape(ML_HEADS, ML_V_DIM)).reshape(B, T, ML_V_W)
    y = hn * jax.nn.sigmoid(o.astype(F32)) * jax.nn.silu(z.astype(F32))
    return jnp.einsum('btc,cd->btd', y.astype(w_out.dtype), w_out)


def mlstm_prompt(h, w_in, b_i, b_f, gain, w_out):
    B, T, _ = h.shape
    q, k, v, li, lf, o, z = mlstm_project(h, w_in, b_i, b_f)
    nc = T // CHUNK
    init = (jnp.zeros((B, ML_HEADS, ML_V_DIM, ML_QK_DIM), F32),
            jnp.zeros((B, ML_HEADS, ML_QK_DIM), F32),
            jnp.zeros((B, ML_HEADS), F32))
    (c, n, m), hs = lax.scan(mlstm_chunk, init, tuple(to_chunks(a, nc) for a in (q, k, v, li, lf)))
    return mlstm_output(from_chunks(hs), o, z, gain, w_out), c, n, m


def mlstm_sample(h, c0, n0, m0, w_in, b_i, b_f, gain, w_out):
    q, k, v, li, lf, o, z = mlstm_project(h, w_in, b_i, b_f)
    (c, n, m), hid = mlstm_chunk((c0.astype(F32), n0.astype(F32), m0.astype(F32)), (q, k, v, li, lf))
    return mlstm_output(hid, o, z, gain, w_out), c, n, m


def causal_dwconv(x_pad, w):
    return lax.conv_general_dilated(x_pad, w[:, None, :].astype(x_pad.dtype), window_strides=(1,),
                                    padding='VALID', dimension_numbers=('NWC', 'WIO', 'NWC'),
                                    feature_group_count=GDN_CONV_CH)


def gdn_project(h, w_in):
    proj = jnp.einsum('btd,dc->btc', h, w_in)
    return jnp.split(proj, [GDN_CONV_CH, GDN_CONV_CH + GDN_V_W, GDN_CONV_CH + GDN_V_W + GDN_V_HEADS], axis=-1)


def gdn_core_inputs(conv_out, a, b, a_log, dt_bias):
    B, T, _ = conv_out.shape
    act = jax.nn.silu(conv_out.astype(F32))
    q, k, v = jnp.split(act, [GDN_K_W, 2 * GDN_K_W], axis=-1)
    rep = GDN_V_HEADS // GDN_K_HEADS
    q = jnp.repeat(l2_norm(q.reshape(B, T, GDN_K_HEADS, GDN_HEAD_DIM)), rep, axis=2) * (GDN_HEAD_DIM ** -0.5)
    k = jnp.repeat(l2_norm(k.reshape(B, T, GDN_K_HEADS, GDN_HEAD_DIM)), rep, axis=2)
    v = v.reshape(B, T, GDN_V_HEADS, GDN_HEAD_DIM)
    g = -jnp.exp(a_log.astype(F32)) * jax.nn.softplus(a.astype(F32) + dt_bias.astype(F32))
    beta = jax.nn.sigmoid(b.astype(F32))
    return q, k, v, g, beta


def gdn_chunk(s, xs):
    q, k, v, g, beta = xs
    L = q.shape[1]
    G = jnp.swapaxes(jnp.cumsum(g, axis=1), 1, 2)
    bt = jnp.swapaxes(beta, 1, 2)
    diff = G[..., :, None] - G[..., None, :]
    incl = jnp.tril(jnp.ones((L, L), dtype=bool))
    strict = jnp.tril(jnp.ones((L, L), dtype=bool), -1)
    dec_incl = jnp.where(incl, jnp.exp(jnp.where(incl, diff, 0.0)), 0.0)
    dec_strict = jnp.where(strict, dec_incl, 0.0)
    a_mat = bt[..., :, None] * dec_strict * jnp.einsum('blhd,bshd->bhls', k, k)
    eG = jnp.exp(G)
    v_t = jnp.swapaxes(v, 1, 2)
    rhs = bt[..., None] * (v_t - eG[..., None] * jnp.einsum('bhvd,blhd->bhlv', s, k))
    u = lax.linalg.triangular_solve(a_mat + jnp.eye(L, dtype=a_mat.dtype), rhs,
                                    left_side=True, lower=True, unit_diagonal=True)
    o = eG[..., None] * jnp.einsum('bhvd,blhd->bhlv', s, q) + \
        jnp.einsum('bhls,bhsv->bhlv', dec_incl * jnp.einsum('blhd,bshd->bhls', q, k), u)
    G_last = G[..., -1]
    s_new = jnp.exp(G_last)[..., None, None] * s + \
        jnp.einsum('bhs,bhsv,bshd->bhvd', jnp.exp(G_last[..., None] - G), u, k)
    return s_new, jnp.swapaxes(o, 1, 2)


def gdn_output(o, z, gain, w_out):
    B, T = o.shape[:2]
    zh = z.reshape(B, T, GDN_V_HEADS, GDN_HEAD_DIM).astype(F32)
    y = (rms_norm(o, gain) * jax.nn.silu(zh)).reshape(B, T, GDN_V_W)
    return jnp.einsum('btc,cd->btd', y.astype(w_out.dtype), w_out)


def gdn_prompt(h, w_in, conv_w, a_log, dt_bias, gain, w_out):
    B, T, _ = h.shape
    qkv, z, a, b = gdn_project(h, w_in)
    x_pad = jnp.pad(qkv, ((0, 0), (GDN_CONV_W - 1, 0), (0, 0)))
    conv_state = x_pad[:, -(GDN_CONV_W - 1):]
    q, k, v, g, beta = gdn_core_inputs(causal_dwconv(x_pad, conv_w), a, b, a_log, dt_bias)
    nc = T // CHUNK
    s0 = jnp.zeros((B, GDN_V_HEADS, GDN_HEAD_DIM, GDN_HEAD_DIM), F32)
    s, os_ = lax.scan(gdn_chunk, s0, tuple(to_chunks(t, nc) for t in (q, k, v, g, beta)))
    return gdn_output(from_chunks(os_), z, gain, w_out), s, conv_state


def gdn_sample(h, s0, conv0, w_in, conv_w, a_log, dt_bias, gain, w_out):
    qkv, z, a, b = gdn_project(h, w_in)
    x_pad = jnp.concatenate([conv0.astype(qkv.dtype), qkv], axis=1)
    conv_state = x_pad[:, -(GDN_CONV_W - 1):]
    q, k, v, g, beta = gdn_core_inputs(causal_dwconv(x_pad, conv_w), a, b, a_log, dt_bias)
    s, o = gdn_chunk(s0.astype(F32), (q, k, v, g, beta))
    return gdn_output(o, z, gain, w_out), s, conv_state


def setup_inputs(seed: int = 0) -> dict:
    key = jax.random.key(seed)
    ks = jax.random.split(key, 32)

    def nrm(k, shape, scale):
        return scale * jax.random.normal(k, shape, F32)

    def uni(k, shape, lo, hi):
        return jax.random.uniform(k, shape, F32, minval=lo, maxval=hi)

    dt = jnp.exp(uni(ks[25], (N_GDN, GDN_V_HEADS), float(np.log(1e-3)), float(np.log(1e-1))))
    return {
        'x_prompt': nrm(ks[0], (BATCH, SEQ, D_MODEL), 1.0),
        'x_sample': nrm(ks[1], (DEC_BATCH, DEC_SEQ, D_MODEL), 1.0),
        'cache_fox_k': nrm(ks[2], (N_FOX, DEC_BATCH, PAST_LEN, FOX_HEADS, FOX_HEAD_DIM), 1.0),
        'cache_fox_v': nrm(ks[3], (N_FOX, DEC_BATCH, PAST_LEN, FOX_HEADS, FOX_HEAD_DIM), 1.0),
        'cache_fox_logf': jax.nn.log_sigmoid(2.5 + nrm(ks[4], (N_FOX, DEC_BATCH, PAST_LEN, FOX_HEADS), 1.0)),
        'state_mlstm_c': nrm(ks[5], (N_MLSTM, DEC_BATCH, ML_HEADS, ML_V_DIM, ML_QK_DIM), 0.1),
        'state_mlstm_n': nrm(ks[6], (N_MLSTM, DEC_BATCH, ML_HEADS, ML_QK_DIM), 0.1),
        'state_mlstm_m': nrm(ks[7], (N_MLSTM, DEC_BATCH, ML_HEADS), 1.0),
        'state_gdn_s': nrm(ks[8], (N_GDN, DEC_BATCH, GDN_V_HEADS, GDN_HEAD_DIM, GDN_HEAD_DIM), 0.1),
        'state_gdn_conv': nrm(ks[9], (N_GDN, DEC_BATCH, GDN_CONV_W - 1, GDN_CONV_CH), 1.0),
        'pre_norm': 1.0 + nrm(ks[10], (DEPTH, D_MODEL), 0.05),
        'post_norm': 1.0 + nrm(ks[11], (DEPTH, D_MODEL), 0.05),
        'fox_w_in': nrm(ks[12], (N_FOX, D_MODEL, FOX_COLS), D_MODEL ** -0.5),
        'fox_b_f': uni(ks[13], (N_FOX, FOX_HEADS), 1.0, 4.0),
        'fox_q_norm': 1.0 + nrm(ks[14], (N_FOX, FOX_HEAD_DIM), 0.05),
        'fox_k_norm': 1.0 + nrm(ks[15], (N_FOX, FOX_HEAD_DIM), 0.05),
        'fox_w_out': nrm(ks[16], (N_FOX, FOX_W, D_MODEL), FOX_W ** -0.5),
        'mlstm_w_in': nrm(ks[17], (N_MLSTM, D_MODEL, ML_COLS), D_MODEL ** -0.5),
        'mlstm_b_i': nrm(ks[18], (N_MLSTM, ML_HEADS), 0.1),
        'mlstm_b_f': uni(ks[19], (N_MLSTM, ML_HEADS), 3.0, 6.0),
        'mlstm_h_norm': 1.0 + nrm(ks[20], (N_MLSTM, ML_V_W), 0.05),
        'mlstm_w_out': nrm(ks[21], (N_MLSTM, ML_V_W, D_MODEL), ML_V_W ** -0.5),
        'gdn_w_in': nrm(ks[22], (N_GDN, D_MODEL, GDN_COLS), D_MODEL ** -0.5),
        'gdn_conv_w': nrm(ks[23], (N_GDN, GDN_CONV_W, GDN_CONV_CH), GDN_CONV_W ** -0.5),
        'gdn_a_log': jnp.log(uni(ks[24], (N_GDN, GDN_V_HEADS), 1.0, 16.0)),
        'gdn_dt_bias': dt + jnp.log(-jnp.expm1(-dt)),
        'gdn_o_norm': 1.0 + nrm(ks[26], (N_GDN, GDN_HEAD_DIM), 0.05),
        'gdn_w_out': nrm(ks[27], (N_GDN, GDN_V_W, D_MODEL), GDN_V_W ** -0.5),
    }


def reference(x_prompt, x_sample, cache_fox_k, cache_fox_v, cache_fox_logf,
              state_mlstm_c, state_mlstm_n, state_mlstm_m, state_gdn_s, state_gdn_conv,
              pre_norm, post_norm,
              fox_w_in, fox_b_f, fox_q_norm, fox_k_norm, fox_w_out,
              mlstm_w_in, mlstm_b_i, mlstm_b_f, mlstm_h_norm, mlstm_w_out,
              gdn_w_in, gdn_conv_w, gdn_a_log, gdn_dt_bias, gdn_o_norm, gdn_w_out):
    xp, xs = x_prompt, x_sample
    fk_p, fv_p, fl_p, fk_s, fv_s, fl_s = [], [], [], [], [], []
    mc_p, mn_p, mm_p, mc_s, mn_s, mm_s = [], [], [], [], [], []
    gs_p, gc_p, gs_s, gc_s = [], [], [], []
    for layer in range(DEPTH):
        kind = layer % N_MIXERS
        j = layer // N_MIXERS
        hp = rms_norm(xp, pre_norm[layer])
        hs = rms_norm(xs, pre_norm[layer])
        if kind == 0:
            mp, a1, a2, a3 = fox_prompt(hp, fox_w_in[j], fox_b_f[j], fox_q_norm[j], fox_k_norm[j], fox_w_out[j])
            ms, b1, b2, b3 = fox_sample(hs, cache_fox_k[j], cache_fox_v[j], cache_fox_logf[j],
                                        fox_w_in[j], fox_b_f[j], fox_q_norm[j], fox_k_norm[j], fox_w_out[j])
            fk_p.append(a1); fv_p.append(a2); fl_p.append(a3)
            fk_s.append(b1); fv_s.append(b2); fl_s.append(b3)
        elif kind == 1:
            mp, a1, a2, a3 = mlstm_prompt(hp, mlstm_w_in[j], mlstm_b_i[j], mlstm_b_f[j], mlstm_h_norm[j], mlstm_w_out[j])
            ms, b1, b2, b3 = mlstm_sample(hs, state_mlstm_c[j], state_mlstm_n[j], state_mlstm_m[j],
                                          mlstm_w_in[j], mlstm_b_i[j], mlstm_b_f[j], mlstm_h_norm[j], mlstm_w_out[j])
            mc_p.append(a1); mn_p.append(a2); mm_p.append(a3)
            mc_s.append(b1); mn_s.append(b2); mm_s.append(b3)
        else:
            mp, a1, a2 = gdn_prompt(hp, gdn_w_in[j], gdn_conv_w[j], gdn_a_log[j], gdn_dt_bias[j], gdn_o_norm[j], gdn_w_out[j])
            ms, b1, b2 = gdn_sample(hs, state_gdn_s[j], state_gdn_conv[j], gdn_w_in[j], gdn_conv_w[j],
                                    gdn_a_log[j], gdn_dt_bias[j], gdn_o_norm[j], gdn_w_out[j])
            gs_p.append(a1); gc_p.append(a2)
            gs_s.append(b1); gc_s.append(b2)
        xp = xp + rms_norm(mp, post_norm[layer])
        xs = xs + rms_norm(ms, post_norm[layer])
    y_prompt, y_sample = xp, xs
    fox_k_prompt, fox_v_prompt, fox_logf_prompt = jnp.stack(fk_p), jnp.stack(fv_p), jnp.stack(fl_p)
    fox_k_sample, fox_v_sample, fox_logf_sample = jnp.stack(fk_s), jnp.stack(fv_s), jnp.stack(fl_s)
    mlstm_c_prompt, mlstm_n_prompt, mlstm_m_prompt = jnp.stack(mc_p), jnp.stack(mn_p), jnp.stack(mm_p)
    mlstm_c_sample, mlstm_n_sample, mlstm_m_sample = jnp.stack(mc_s), jnp.stack(mn_s), jnp.stack(mm_s)
    gdn_s_prompt, gdn_conv_prompt = jnp.stack(gs_p), jnp.stack(gc_p)
    gdn_s_sample, gdn_conv_sample = jnp.stack(gs_s), jnp.stack(gc_s)
    return (y_prompt, y_sample,
            fox_k_prompt, fox_v_prompt, fox_logf_prompt,
            fox_k_sample, fox_v_sample, fox_logf_sample,
            mlstm_c_prompt, mlstm_n_prompt, mlstm_m_prompt,
            mlstm_c_sample, mlstm_n_sample, mlstm_m_sample,
            gdn_s_prompt, gdn_conv_prompt, gdn_s_sample, gdn_conv_sample)
```

```python
import functools

import numpy as np
import jax
import jax.numpy as jnp
from jax import lax
from jax.experimental import pallas as pl
from jax.experimental.pallas import tpu as pltpu

F32 = jnp.float32
BF16 = jnp.bfloat16
NORM_EPS = 1e-6
CHUNK = 64
LANES = 128
NEG = -1e30
VMEM_LIMIT = 56 * 1024 * 1024
FOX_TQ = 512
HIGHEST = lax.Precision.HIGHEST


def _params(sem, vmem=VMEM_LIMIT):
    return pltpu.CompilerParams(dimension_semantics=sem, vmem_limit_bytes=vmem)


def _pick_tile(n, target, mult):
    best = None
    for t in range(mult, min(n, target) + 1, mult):
        if n % t == 0:
            best = t
    assert best is not None, (n, target, mult)
    return best


def _sigmoid(x):
    return 1.0 / (1.0 + jnp.exp(-x))


def _log_sigmoid(x):
    return jnp.minimum(x, 0.0) - jnp.log(1.0 + jnp.exp(-jnp.abs(x)))


def _softplus(x):
    return jnp.maximum(x, 0.0) + jnp.log(1.0 + jnp.exp(-jnp.abs(x)))


def _dot(a, b):
    return jnp.dot(a, b, preferred_element_type=F32)


def _dot_nt(a, b):
    return lax.dot_general(a, b, (((1,), (1,)), ((), ())), preferred_element_type=F32)


def _dot_tn(a, b):
    return lax.dot_general(a, b, (((0,), (0,)), ((), ())), preferred_element_type=F32)


def _rmsnorm_cast_kernel(x_ref, g_ref, o_ref):
    x = x_ref[...]
    ms = jnp.mean(x * x, axis=-1, keepdims=True)
    o_ref[...] = (x * lax.rsqrt(ms + NORM_EPS) * g_ref[...]).astype(o_ref.dtype)


def rmsnorm_cast(x, g):
    m, d = x.shape
    tm = _pick_tile(m, 256, 16)
    return pl.pallas_call(
        _rmsnorm_cast_kernel,
        grid=(m // tm,),
        in_specs=[pl.BlockSpec((tm, d), lambda i: (i, 0)), pl.BlockSpec((1, d), lambda i: (0, 0))],
        out_specs=pl.BlockSpec((tm, d), lambda i: (i, 0)),
        out_shape=jax.ShapeDtypeStruct((m, d), BF16),
        compiler_params=_params(("parallel",)),
        name="rmsnorm_cast",
    )(x, g.reshape(1, d))


def _postnorm_residual_kernel(y_ref, x_ref, g_ref, o_ref):
    y = y_ref[...]
    ms = jnp.mean(y * y, axis=-1, keepdims=True)
    o_ref[...] = x_ref[...] + y * lax.rsqrt(ms + NORM_EPS) * g_ref[...]


def postnorm_residual(y, x, g):
    m, d = x.shape
    tm = _pick_tile(m, 256, 8)
    row = pl.BlockSpec((tm, d), lambda i: (i, 0))
    return pl.pallas_call(
        _postnorm_residual_kernel,
        grid=(m // tm,),
        in_specs=[row, row, pl.BlockSpec((1, d), lambda i: (0, 0))],
        out_specs=row,
        out_shape=jax.ShapeDtypeStruct((m, d), F32),
        compiler_params=_params(("parallel",)),
        name="postnorm_residual",
    )(y, x, g.reshape(1, d))


def _matmul_kernel(a_ref, w_ref, o_ref):
    o_ref[...] = _dot(a_ref[...], w_ref[...].astype(BF16)).astype(o_ref.dtype)


def matmul(a, w, n_cols, tn_target=512, tm_target=1408):
    m, k = a.shape
    tm = _pick_tile(m, tm_target, 16)
    tn = n_cols if n_cols < LANES else _pick_tile(n_cols, tn_target, LANES)
    return pl.pallas_call(
        _matmul_kernel,
        grid=(m // tm, n_cols // tn),
        in_specs=[pl.BlockSpec((tm, k), lambda i, j: (i, 0), pipeline_mode=pl.Buffered(1)),
                  pl.BlockSpec((k, tn), lambda i, j: (0, j))],
        out_specs=pl.BlockSpec((tm, tn), lambda i, j: (i, j)),
        out_shape=jax.ShapeDtypeStruct((m, n_cols), F32),
        compiler_params=_params(("parallel", "arbitrary")),
        name="matmul",
    )(a, w)


def _logf_cumsum_kernel(f_ref, b_ref, init_ref, lf_ref, cum_ref, carry, *, apply_gate):
    @pl.when(pl.program_id(1) == 0)
    def _():
        carry[...] = init_ref[0]

    x = f_ref[...]
    if apply_gate:
        x = _log_sigmoid(x + b_ref[...])
    tb = x.shape[0]
    r = lax.broadcasted_iota(jnp.int32, (tb, tb), 0)
    c = lax.broadcasted_iota(jnp.int32, (tb, tb), 1)
    tri = (c <= r).astype(F32)
    cum = jnp.dot(tri, x, precision=HIGHEST, preferred_element_type=F32) + carry[...]
    lf_ref[...] = x
    cum_ref[...] = cum
    carry[...] = cum[tb - 1:tb, :]


def logf_cumsum(f, row_off, nseq, t, bias, init, apply_gate):
    h = f.shape[1]
    tb = _pick_tile(t, 512, 8)
    nt = t // tb
    off = row_off // tb
    assert row_off % tb == 0
    blk = pl.BlockSpec((tb, h), lambda s, i: (s * nt + i, 0))
    return pl.pallas_call(
        functools.partial(_logf_cumsum_kernel, apply_gate=apply_gate),
        grid=(nseq, nt),
        in_specs=[pl.BlockSpec((tb, h), lambda s, i: (off + s * nt + i, 0)),
                  pl.BlockSpec((1, h), lambda s, i: (0, 0)),
                  pl.BlockSpec((1, 1, h), lambda s, i: (s, 0, 0))],
        out_specs=[blk, blk],
        out_shape=[jax.ShapeDtypeStruct((nseq * t, h), F32)] * 2,
        scratch_shapes=[pltpu.VMEM((1, h), F32)],
        compiler_params=_params(("parallel", "arbitrary")),
        name="logf_cumsum",
    )(f, bias.reshape(1, h), init)


def _fox_select_matrices(h, k_side):
    sel = np.zeros((4, h, h * LANES), np.float32)
    part_lane0, ones_lane0 = (3, 0) if k_side else (0, 3)
    for hh in range(h):
        for p in range(3):
            sel[p, hh, hh * LANES + part_lane0 + p] = 1.0
            sel[3, hh, hh * LANES + ones_lane0 + p] = 1.0
    return jnp.asarray(sel, BF16)


def _fox_build_kernel(x_ref, cum_ref, g_ref, sel_ref, xp_ref, *xn_refs, normalize, scale, negate):
    c = cum_ref[...]
    if negate:
        c = -c
    c_hi = c.astype(BF16)
    r1 = c - c_hi.astype(F32)
    c_mid = r1.astype(BF16)
    c_lo = (r1 - c_mid.astype(F32)).astype(BF16)
    extra = (_dot(c_hi, sel_ref[0]) + _dot(c_mid, sel_ref[1]) + _dot(c_lo, sel_ref[2])
             + _dot(jnp.ones_like(c_hi), sel_ref[3]))
    g = g_ref[...]
    for h in range(c.shape[1]):
        xs = x_ref[:, h * LANES:(h + 1) * LANES]
        if normalize:
            xs = xs * lax.rsqrt(jnp.mean(xs * xs, axis=-1, keepdims=True) + NORM_EPS) * g
            xn_refs[0][:, h * LANES:(h + 1) * LANES] = xs
        xp_ref[:, 2 * h * LANES:(2 * h + 1) * LANES] = (xs * scale).astype(BF16)
        xp_ref[:, (2 * h + 1) * LANES:(2 * h + 2) * LANES] = extra[:, h * LANES:(h + 1) * LANES].astype(BF16)


def fox_build(x, col_block, row_off, rows, cum, gain, k_side, normalize, scale):
    h = cum.shape[1]
    w = h * LANES
    tm = _pick_tile(rows, 128, 16)
    assert row_off % tm == 0
    off = row_off // tm
    out_specs = [pl.BlockSpec((tm, 2 * w), lambda i: (i, 0))]
    out_shape = [jax.ShapeDtypeStruct((rows, 2 * w), BF16)]
    if normalize:
        out_specs.append(pl.BlockSpec((tm, w), lambda i: (i, 0)))
        out_shape.append(jax.ShapeDtypeStruct((rows, w), F32))
    return pl.pallas_call(
        functools.partial(_fox_build_kernel, normalize=normalize, scale=scale, negate=k_side),
        grid=(rows // tm,),
        in_specs=[pl.BlockSpec((tm, w), lambda i: (off + i, col_block)),
                  pl.BlockSpec((tm, h), lambda i: (i, 0)),
                  pl.BlockSpec((1, LANES), lambda i: (0, 0)),
                  pl.BlockSpec((4, h, w), lambda i: (0, 0, 0))],
        out_specs=out_specs,
        out_shape=out_shape,
        compiler_params=_params(("parallel",)),
        name="fox_build",
    )(x, cum, gain.reshape(1, LANES), _fox_select_matrices(h, k_side))


def _fox_attn_kernel(q_ref, k_ref, v_ref, z_ref, o_ref, *, tq, tk, offset):
    i = pl.program_id(1)
    q = q_ref[...]
    row0 = offset + i * tq
    n_full = (row0 + 1) // tk
    n_all = (row0 + tq + tk - 1) // tk

    def step(j, carry, masked):
        m, l, acc = carry
        start = pl.multiple_of(j * tk, tk)
        s = _dot_nt(q, k_ref[pl.ds(start, tk), :])
        if masked:
            r = row0 + lax.broadcasted_iota(jnp.int32, (tq, tk), 0)
            c = start + lax.broadcasted_iota(jnp.int32, (tq, tk), 1)
            s = jnp.where(c <= r, s, NEG)
        m_new = jnp.maximum(m, jnp.max(s, axis=-1, keepdims=True))
        alpha = jnp.exp(m - m_new)
        p = jnp.exp(s - m_new)
        l = alpha * l + jnp.sum(p, axis=-1, keepdims=True)
        v = v_ref[pl.ds(start, tk), :].astype(BF16)
        acc = alpha * acc + _dot(p.astype(BF16), v)
        return m_new, l, acc

    carry = (jnp.full((tq, 1), NEG, F32), jnp.zeros((tq, 1), F32), jnp.zeros((tq, LANES), F32))
    carry = lax.fori_loop(0, n_full, functools.partial(step, masked=False), carry)
    _, l, acc = lax.fori_loop(n_full, n_all, functools.partial(step, masked=True), carry)
    z = z_ref[...]
    o_ref[...] = (acc / l * (z * _sigmoid(z))).astype(o_ref.dtype)


def fox_attention(qp, kp, v, v_col_off, z, z_row_off, z_col_off, nseq, tq_total, tk_total, nheads, tq, tk, offset):
    nq = tq_total // tq
    assert z_row_off % tq == 0 and tk_total % tk == 0
    zoff = z_row_off // tq
    return pl.pallas_call(
        functools.partial(_fox_attn_kernel, tq=tq, tk=tk, offset=offset),
        grid=(nseq * nheads, nq),
        in_specs=[pl.BlockSpec((tq, 2 * LANES), lambda g, i: (g // nheads * nq + i, g % nheads)),
                  pl.BlockSpec((tk_total, 2 * LANES), lambda g, i: (g // nheads, g % nheads)),
                  pl.BlockSpec((tk_total, LANES), lambda g, i: (g // nheads, v_col_off + g % nheads)),
                  pl.BlockSpec((tq, LANES), lambda g, i: (zoff + g // nheads * nq + i, z_col_off + g % nheads))],
        out_specs=pl.BlockSpec((tq, LANES), lambda g, i: (g // nheads * nq + i, g % nheads)),
        out_shape=jax.ShapeDtypeStruct((nseq * tq_total, nheads * LANES), BF16),
        compiler_params=_params(("parallel", "arbitrary")),
        name="fox_attention",
    )(qp, kp, v, z)


def fox_layer(h, mp_rows, b, t, bs, ts, cache_k, cache_v, cache_lf, w_in, b_f, q_gain, k_gain):
    nheads = b_f.shape[0]
    w = nheads * LANES
    p = cache_k.shape[1]
    scale = float(LANES) ** -0.5
    proj = matmul(h, w_in, 4 * w)
    fgate = matmul(h, w_in[:, 4 * w:], nheads)
    zeros = jnp.zeros((b, 1, nheads), F32)
    lf_p, cum_p = logf_cumsum(fgate, 0, b, t, b_f, zeros, True)
    qp_p, = fox_build(proj, 0, 0, mp_rows, cum_p, q_gain, False, True, scale)[:1]
    kp_p, kn_p = fox_build(proj, 1, 0, mp_rows, cum_p, k_gain, True, True, 1.0)
    tq = _pick_tile(t, FOX_TQ, 16)
    y_p = fox_attention(qp_p, kp_p, proj, 2 * nheads, proj, 0, 3 * nheads, b, t, t, nheads, tq, tq, 0)
    v_p = proj[:mp_rows, 2 * w:3 * w]
    ms_rows = bs * ts
    lf_c, cum_c = logf_cumsum(cache_lf.reshape(bs * p, nheads), 0, bs, p, b_f,
                              jnp.zeros((bs, 1, nheads), F32), False)
    init_s = cum_c.reshape(bs, p, nheads)[:, p - 1:, :]
    lf_s, cum_s = logf_cumsum(fgate, mp_rows, bs, ts, b_f, init_s, True)
    qp_s, = fox_build(proj, 0, mp_rows, ms_rows, cum_s, q_gain, False, True, scale)[:1]
    kp_s, kn_s = fox_build(proj, 1, mp_rows, ms_rows, cum_s, k_gain, True, True, 1.0)
    kp_c, = fox_build(cache_k.reshape(bs * p, w), 0, 0, bs * p, cum_c, k_gain, True, False, 1.0)
    v_s = proj[mp_rows:, 2 * w:3 * w]
    pad = (-ts) % LANES
    tk_s = p + ts + pad
    kp_all = jnp.concatenate([kp_c.reshape(bs, p, 2 * w), kp_s.reshape(bs, ts, 2 * w),
                              jnp.zeros((bs, pad, 2 * w), BF16)], axis=1).reshape(bs * tk_s, 2 * w)
    v_all = jnp.concatenate([cache_v.reshape(bs, p, w), v_s.reshape(bs, ts, w),
                             jnp.zeros((bs, pad, w), F32)], axis=1).reshape(bs * tk_s, w)
    y_s = fox_attention(qp_s, kp_all, v_all, 0, proj, mp_rows, 3 * nheads, bs, ts, tk_s, nheads, ts, tk_s, p)
    y = jnp.concatenate([y_p, y_s], axis=0)
    hd = LANES
    outs = (kn_p.reshape(b, t, nheads, hd), v_p.reshape(b, t, nheads, hd), lf_p.reshape(b, t, nheads),
            kn_s.reshape(bs, ts, nheads, hd), v_s.reshape(bs, ts, nheads, hd), lf_s.reshape(bs, ts, nheads))
    return y, outs


def _mlstm_kernel(q_ref, k_ref, v_ref, o_ref, z_ref, gate_ref, bi_ref, bf_ref, gain_ref,
                  c0_ref, n0_ref, m0_ref, y_ref, c_ref, n_ref, m_ref, *, nheads, dk, dv):
    @pl.when(pl.program_id(1) == 0)
    def _():
        c_ref[...] = c0_ref[...]
        n_ref[...] = n0_ref[...]
        m_ref[...] = m0_ref[...]

    ln = q_ref.shape[0]
    gates = gate_ref[...]
    log_i = gates[:, :nheads] + bi_ref[...]
    log_f = _log_sigmoid(gates[:, nheads:] + bf_ref[...])
    r = lax.broadcasted_iota(jnp.int32, (ln, ln), 0)
    c = lax.broadcasted_iota(jnp.int32, (ln, ln), 1)
    causal = c <= r
    eye = c == r
    b_all = jnp.dot(causal.astype(F32), log_f, precision=HIGHEST, preferred_element_type=F32)
    kscale = float(dk) ** -0.5
    for h in range(nheads):
        b_col = b_all[:, h:h + 1]
        li_col = log_i[:, h:h + 1]
        b_row = jnp.sum(jnp.where(eye, b_col, 0.0), axis=0, keepdims=True)
        li_row = jnp.sum(jnp.where(eye, li_col, 0.0), axis=0, keepdims=True)
        m_prev = m_ref[0, h:h + 1, 0:1]
        d = jnp.where(causal, b_col - b_row + li_row, -jnp.inf)
        inter = b_col + m_prev
        m_t = jnp.maximum(inter, jnp.max(d, axis=-1, keepdims=True))
        q = q_ref[:, h * dk:(h + 1) * dk]
        k = k_ref[:, h * dk:(h + 1) * dk] * kscale
        v = v_ref[:, h * dv:(h + 1) * dv]
        qb = q.astype(BF16)
        kb = k.astype(BF16)
        wmat = jnp.exp(d - m_t) * _dot_nt(qb, kb)
        inter_w = jnp.exp(inter - m_t)
        cst = c_ref[0, h]
        nst = n_ref[0, h:h + 1, :]
        num = _dot(wmat.astype(BF16), v.astype(BF16)) + inter_w * _dot_nt(qb, cst.astype(BF16))
        den = jnp.sum(wmat, axis=-1, keepdims=True) + inter_w * jnp.sum(q * nst, axis=-1, keepdims=True)
        hid = num / jnp.maximum(jnp.abs(den), jnp.exp(-m_t))
        b_last = b_col[ln - 1:ln, :]
        g_col = b_last - b_col + li_col
        m_new = jnp.maximum(b_last + m_prev, jnp.max(g_col, axis=0, keepdims=True))
        wk = jnp.exp(g_col - m_new)
        carry_w = jnp.exp(b_last + m_prev - m_new)
        c_ref[0, h] = carry_w * cst + _dot_tn((wk * v).astype(BF16), kb)
        n_ref[0, h:h + 1, :] = carry_w * nst + jnp.sum(wk * k, axis=0, keepdims=True)
        m_ref[0, h:h + 1, :] = jnp.broadcast_to(m_new, (1, LANES))
        hn = hid * lax.rsqrt(jnp.mean(hid * hid, axis=-1, keepdims=True) + NORM_EPS)
        hn = hn * gain_ref[:, h * dv:(h + 1) * dv]
        og = o_ref[:, h * dv:(h + 1) * dv]
        zg = z_ref[:, h * dv:(h + 1) * dv]
        y_ref[:, h * dv:(h + 1) * dv] = (hn * _sigmoid(og) * (zg * _sigmoid(zg))).astype(y_ref.dtype)


def mlstm_scan(proj, gates, row_off, nseq, t, ln, c0, n0, m0, b_i, b_f, gain):
    nheads, dv, dk = c0.shape[1:]
    nc = t // ln
    assert row_off % ln == 0
    off = row_off // ln
    qk_w, v_w = nheads * dk, nheads * dv
    assert v_w == 2 * qk_w

    def rows(width, cb):
        return pl.BlockSpec((ln, width), lambda s, i: (off + s * nc + i, cb))

    m0b = jnp.broadcast_to(m0[:, :, None], (nseq, nheads, LANES))
    y, c, n, m = pl.pallas_call(
        functools.partial(_mlstm_kernel, nheads=nheads, dk=dk, dv=dv),
        grid=(nseq, nc),
        in_specs=[rows(qk_w, 0), rows(qk_w, 1), rows(v_w, 1), rows(v_w, 2), rows(v_w, 3),
                  rows(2 * nheads, 0),
                  pl.BlockSpec((1, nheads), lambda s, i: (0, 0)),
                  pl.BlockSpec((1, nheads), lambda s, i: (0, 0)),
                  pl.BlockSpec((1, v_w), lambda s, i: (0, 0)),
                  pl.BlockSpec((1, nheads, dv, dk), lambda s, i: (s, 0, 0, 0)),
                  pl.BlockSpec((1, nheads, dk), lambda s, i: (s, 0, 0)),
                  pl.BlockSpec((1, nheads, LANES), lambda s, i: (s, 0, 0))],
        out_specs=[pl.BlockSpec((ln, v_w), lambda s, i: (s * nc + i, 0)),
                   pl.BlockSpec((1, nheads, dv, dk), lambda s, i: (s, 0, 0, 0)),
                   pl.BlockSpec((1, nheads, dk), lambda s, i: (s, 0, 0)),
                   pl.BlockSpec((1, nheads, LANES), lambda s, i: (s, 0, 0))],
        out_shape=[jax.ShapeDtypeStruct((nseq * t, v_w), BF16),
                   jax.ShapeDtypeStruct((nseq, nheads, dv, dk), F32),
                   jax.ShapeDtypeStruct((nseq, nheads, dk), F32),
                   jax.ShapeDtypeStruct((nseq, nheads, LANES), F32)],
        compiler_params=_params(("parallel", "arbitrary")),
        name="mlstm_scan",
    )(proj, proj, proj, proj, proj, gates, b_i.reshape(1, nheads), b_f.reshape(1, nheads),
      gain.reshape(1, v_w), c0, n0, m0b)
    return y, c, n, m[:, :, 0]


def mlstm_layer(h, mp_rows, b, t, bs, ts, c0, n0, m0, w_in, b_i, b_f, gain):
    nheads, dv, dk = c0.shape[1:]
    main = 2 * nheads * dk + 3 * nheads * dv
    proj = matmul(h, w_in, main)
    gates = matmul(h, w_in[:, main:], 2 * nheads)
    zc = jnp.zeros((b, nheads, dv, dk), F32)
    y_p, c_p, n_p, m_p = mlstm_scan(proj, gates, 0, b, t, min(CHUNK, t), zc, zc[:, :, 0, :], zc[:, :, 0, 0],
                                    b_i, b_f, gain)
    y_s, c_s, n_s, m_s = mlstm_scan(proj, gates, mp_rows, bs, ts, ts, c0, n0, m0, b_i, b_f, gain)
    return jnp.concatenate([y_p, y_s], axis=0), (c_p, n_p, m_p, c_s, n_s, m_s)


def _gdn_conv_kernel(x_ref, halo_ref, init_ref, w_ref, sc_ref, o_ref, *, l2norm):
    tm = x_ref.shape[0]
    prev = jnp.where(pl.program_id(1) == 0, init_ref[0], halo_ref[...])
    xc = jnp.concatenate([prev, x_ref[...]], axis=0)
    w = w_ref[...]
    acc = w[3:4] * xc[8:8 + tm] + w[2:3] * xc[7:7 + tm] + w[1:2] * xc[6:6 + tm] + w[0:1] * xc[5:5 + tm]
    act = acc * _sigmoid(acc)
    if l2norm:
        for g in range(act.shape[1] // LANES):
            a = act[:, g * LANES:(g + 1) * LANES]
            a = a * lax.rsqrt(jnp.sum(a * a, axis=-1, keepdims=True) + NORM_EPS)
            o_ref[:, g * LANES:(g + 1) * LANES] = (a * sc_ref[:, g * LANES:(g + 1) * LANES]).astype(o_ref.dtype)
    else:
        o_ref[...] = act.astype(o_ref.dtype)


def gdn_conv(proj, row_off, nseq, t, ch_off, ch, init, conv_w, scale, l2norm, out_dtype):
    tm = _pick_tile(t, 512, 8)
    tc = _pick_tile(ch, 1024, LANES)
    nt = t // tm
    assert row_off % tm == 0 and ch_off % tc == 0 and tm % 8 == 0
    off, coff, sub = row_off // tm, ch_off // tc, tm // 8
    return pl.pallas_call(
        functools.partial(_gdn_conv_kernel, l2norm=l2norm),
        grid=(nseq, nt, ch // tc),
        in_specs=[pl.BlockSpec((tm, tc), lambda s, i, j: (off + s * nt + i, coff + j)),
                  pl.BlockSpec((8, tc), lambda s, i, j: (jnp.maximum((off + s * nt + i) * sub - 1, 0), coff + j)),
                  pl.BlockSpec((1, 8, tc), lambda s, i, j: (s, 0, coff + j)),
                  pl.BlockSpec((conv_w.shape[0], tc), lambda s, i, j: (0, coff + j)),
                  pl.BlockSpec((1, tc), lambda s, i, j: (0, j))],
        out_specs=pl.BlockSpec((tm, tc), lambda s, i, j: (s * nt + i, j)),
        out_shape=jax.ShapeDtypeStruct((nseq * t, ch), out_dtype),
        compiler_params=_params(("parallel", "arbitrary", "arbitrary")),
        name="gdn_conv",
    )(proj, proj, init, conv_w, scale)


def _unit_lower_inverse(a, r, c):
    ln = a.shape[0]
    x = jnp.where(r == c, 1.0, 0.0) - jnp.where(((r >> 1) == (c >> 1)) & (c < r), a, 0.0)
    sh = 1
    while (1 << sh) < ln:
        off = jnp.where(((r >> (sh + 1)) == (c >> (sh + 1))) & ((r >> sh) != (c >> sh)) & (c < r), a, 0.0)
        xb = x.astype(BF16)
        x = x - _dot(_dot(xb, off.astype(BF16)).astype(BF16), xb)
        sh += 1
    return x


def _gdn_kernel(qk_ref, v_ref, z_ref, gate_ref, alog_ref, dt_ref, gain_ref, s0_ref, y_ref, s_ref,
                *, kheads, vheads):
    @pl.when(pl.program_id(1) == 0)
    def _():
        s_ref[...] = s0_ref[...]

    ln = v_ref.shape[0]
    hd = LANES
    rep = vheads // kheads
    gates = gate_ref[...]
    g_all = -jnp.exp(alog_ref[...]) * _softplus(gates[:, :vheads] + dt_ref[...])
    beta_all = _sigmoid(gates[:, vheads:])
    r = lax.broadcasted_iota(jnp.int32, (ln, ln), 0)
    c = lax.broadcasted_iota(jnp.int32, (ln, ln), 1)
    incl = c <= r
    strict = c < r
    eye = c == r
    gsum_all = jnp.dot(incl.astype(F32), g_all, precision=HIGHEST, preferred_element_type=F32)
    gain = gain_ref[...]
    for kh in range(kheads):
        q = qk_ref[:, kh * hd:(kh + 1) * hd]
        k = qk_ref[:, (kheads + kh) * hd:(kheads + kh + 1) * hd]
        kk = _dot_nt(k, k)
        qk = _dot_nt(q, k)
        for h in range(kh * rep, (kh + 1) * rep):
            g_col = gsum_all[:, h:h + 1]
            bt = beta_all[:, h:h + 1]
            g_row = jnp.sum(jnp.where(eye, g_col, 0.0), axis=0, keepdims=True)
            dec_incl = jnp.where(incl, jnp.exp(jnp.where(incl, g_col - g_row, 0.0)), 0.0)
            a_mat = jnp.where(strict, bt * dec_incl * kk, 0.0)
            eg = jnp.exp(g_col)
            st = s_ref[0, h]
            sb = st.astype(BF16)
            v = v_ref[:, h * hd:(h + 1) * hd]
            rhs = bt * (v - eg * _dot_nt(k, sb))
            u = _dot(_unit_lower_inverse(a_mat, r, c).astype(BF16), rhs.astype(BF16))
            ub = u.astype(BF16)
            o = eg * _dot_nt(q, sb) + _dot((dec_incl * qk).astype(BF16), ub)
            g_last = g_col[ln - 1:ln, :]
            s_ref[0, h] = jnp.exp(g_last) * st + _dot_tn((jnp.exp(g_last - g_col) * u).astype(BF16), k)
            on = o * lax.rsqrt(jnp.mean(o * o, axis=-1, keepdims=True) + NORM_EPS) * gain
            zg = z_ref[:, h * hd:(h + 1) * hd]
            y_ref[:, h * hd:(h + 1) * hd] = (on * (zg * _sigmoid(zg))).astype(y_ref.dtype)


def gdn_scan(qk, v, proj, z_col_block, gates, row_off, nseq, t, ln, s0, a_log, dt_bias, gain):
    vheads = s0.shape[1]
    v_w = vheads * LANES
    kheads = qk.shape[1] // (2 * LANES)
    nc = t // ln
    assert row_off % ln == 0
    off = row_off // ln
    local = lambda width: pl.BlockSpec((ln, width), lambda s, i: (s * nc + i, 0))
    state = pl.BlockSpec((1, vheads, LANES, LANES), lambda s, i: (s, 0, 0, 0))
    vec = lambda width: pl.BlockSpec((1, width), lambda s, i: (0, 0))
    return pl.pallas_call(
        functools.partial(_gdn_kernel, kheads=kheads, vheads=vheads),
        grid=(nseq, nc),
        in_specs=[local(qk.shape[1]), local(v_w),
                  pl.BlockSpec((ln, v_w), lambda s, i: (off + s * nc + i, z_col_block)),
                  pl.BlockSpec((ln, 2 * vheads), lambda s, i: (off + s * nc + i, 0)),
                  vec(vheads), vec(vheads), vec(LANES), state],
        out_specs=[local(v_w), state],
        out_shape=[jax.ShapeDtypeStruct((nseq * t, v_w), BF16),
                   jax.ShapeDtypeStruct(s0.shape, F32)],
        compiler_params=_params(("parallel", "arbitrary")),
        name="gdn_scan",
    )(qk, v, proj, gates, a_log.reshape(1, vheads), dt_bias.reshape(1, vheads), gain.reshape(1, LANES), s0)


def gdn_layer(h, mp_rows, b, t, bs, ts, s0, conv0, w_in, conv_w, a_log, dt_bias, gain):
    vheads = s0.shape[1]
    v_w = vheads * LANES
    ch = conv_w.shape[1]
    qk_w = ch - v_w
    kheads = qk_w // (2 * LANES)
    taps = conv_w.shape[0]
    proj = matmul(h, w_in, ch + v_w)
    gates = matmul(h, w_in[:, ch + v_w:], 2 * vheads)
    scale = jnp.concatenate([jnp.full((1, qk_w // 2), float(LANES) ** -0.5, F32), jnp.ones((1, qk_w // 2), F32)], axis=1)
    ones = jnp.ones((1, v_w), F32)
    outs = []
    ys = []
    for row_off, nseq, tt, ln, init_rows, st0 in (
            (0, b, t, min(CHUNK, t), jnp.zeros((b, taps - 1, ch), F32), jnp.zeros((b,) + s0.shape[1:], F32)),
            (mp_rows, bs, ts, ts, conv0, s0)):
        init = jnp.pad(init_rows, ((0, 0), (8 - (taps - 1), 0), (0, 0)))
        qk = gdn_conv(proj, row_off, nseq, tt, 0, qk_w, init, conv_w, scale, True, BF16)
        v = gdn_conv(proj, row_off, nseq, tt, qk_w, v_w, init, conv_w, ones, False, F32)
        y, s_new = gdn_scan(qk, v, proj, ch // v_w, gates, row_off, nseq, tt, ln, st0, a_log, dt_bias, gain)
        conv_state = proj[row_off:row_off + nseq * tt, :ch].reshape(nseq, tt, ch)[:, tt - (taps - 1):, :]
        ys.append(y)
        outs += [s_new, conv_state]
    return jnp.concatenate(ys, axis=0), tuple(outs)


def kernel(x_prompt, x_sample, cache_fox_k, cache_fox_v, cache_fox_logf, state_mlstm_c, state_mlstm_n, state_mlstm_m, state_gdn_s, state_gdn_conv, pre_norm, post_norm, fox_w_in, fox_b_f, fox_q_norm, fox_k_norm, fox_w_out, mlstm_w_in, mlstm_b_i, mlstm_b_f, mlstm_h_norm, mlstm_w_out, gdn_w_in, gdn_conv_w, gdn_a_log, gdn_dt_bias, gdn_o_norm, gdn_w_out):
    b, t, d = x_prompt.shape
    bs, ts, _ = x_sample.shape
    mp_rows = b * t
    x = jnp.concatenate([x_prompt.reshape(mp_rows, d), x_sample.reshape(bs * ts, d)], axis=0)
    fox, ml, gd = [], [], []
    for layer in range(pre_norm.shape[0]):
        kind, j = layer % 3, layer // 3
        h = rmsnorm_cast(x, pre_norm[layer])
        if kind == 0:
            y, outs = fox_layer(h, mp_rows, b, t, bs, ts, cache_fox_k[j], cache_fox_v[j], cache_fox_logf[j],
                                fox_w_in[j], fox_b_f[j], fox_q_norm[j], fox_k_norm[j])
            fox.append(outs)
            w_out = fox_w_out[j]
        elif kind == 1:
            y, outs = mlstm_layer(h, mp_rows, b, t, bs, ts, state_mlstm_c[j], state_mlstm_n[j], state_mlstm_m[j],
                                  mlstm_w_in[j], mlstm_b_i[j], mlstm_b_f[j], mlstm_h_norm[j])
            ml.append(outs)
            w_out = mlstm_w_out[j]
        else:
            y, outs = gdn_layer(h, mp_rows, b, t, bs, ts, state_gdn_s[j], state_gdn_conv[j], gdn_w_in[j],
                                gdn_conv_w[j], gdn_a_log[j], gdn_dt_bias[j], gdn_o_norm[j])
            gd.append(outs)
            w_out = gdn_w_out[j]
        x = postnorm_residual(matmul(y, w_out, d), x, post_norm[layer])
    stack = lambda group, idx: jnp.stack([o[idx] for o in group])
    return ((x[:mp_rows].reshape(b, t, d), x[mp_rows:].reshape(bs, ts, d))
            + tuple(stack(fox, i) for i in range(6))
            + tuple(stack(ml, i) for i in range(6))
            + tuple(stack(gd, i) for i in range(4)))
```

```python
import functools
import math

import numpy as np
import jax
import jax.numpy as jnp
from jax import lax
from jax.experimental import pallas as pl
from jax.experimental.pallas import tpu as pltpu

F32 = jnp.float32
BF16 = jnp.bfloat16
NORM_EPS = 1e-6
CHUNK = 64
LANES = 128
MXU_DIM = 256
NEG = -1e30
VMEM_LIMIT = 56 * 1024 * 1024
FOX_TQ = 512
FOX_SUB = 2
FOX_HEADS_PER_STEP = 8
LOG2E = math.log2(math.e)
HIGHEST = lax.Precision.HIGHEST
ANY = pl.BlockSpec(memory_space=pl.ANY)


def _params(sem, vmem=VMEM_LIMIT):
    return pltpu.CompilerParams(dimension_semantics=sem, vmem_limit_bytes=vmem)


def _pick_tile(n, target, mult):
    best = None
    for t in range(mult, min(n, target) + 1, mult):
        if n % t == 0:
            best = t
    assert best is not None, (n, target, mult)
    return best


def _sigmoid(x):
    return 1.0 / (1.0 + jnp.exp(-x))


def _log_sigmoid(x):
    return jnp.minimum(x, 0.0) - jnp.log(1.0 + jnp.exp(-jnp.abs(x)))


def _softplus(x):
    return jnp.maximum(x, 0.0) + jnp.log(1.0 + jnp.exp(-jnp.abs(x)))


def _dot(a, b):
    return jnp.dot(a, b, preferred_element_type=F32)


def _dot_nt(a, b):
    return lax.dot_general(a, b, (((1,), (1,)), ((), ())), preferred_element_type=F32)


def _dot_tn(a, b):
    return lax.dot_general(a, b, (((0,), (0,)), ((), ())), preferred_element_type=F32)


def _bmm(a, b):
    return lax.dot_general(a, b, (((2,), (1,)), ((0,), (0,))), preferred_element_type=F32)


def _bmm_nt(a, b):
    return lax.dot_general(a, b, (((2,), (2,)), ((0,), (0,))), preferred_element_type=F32)


def _rmsnorm_cast_kernel(x_ref, g_ref, o_ref):
    x = x_ref[...]
    ms = jnp.mean(x * x, axis=-1, keepdims=True)
    o_ref[...] = (x * lax.rsqrt(ms + NORM_EPS) * g_ref[...]).astype(o_ref.dtype)


def rmsnorm_cast(x, g):
    m, d = x.shape
    tm = _pick_tile(m, 256, 16)
    return pl.pallas_call(
        _rmsnorm_cast_kernel,
        grid=(m // tm,),
        in_specs=[pl.BlockSpec((tm, d), lambda i: (i, 0)), pl.BlockSpec((1, d), lambda i: (0, 0))],
        out_specs=pl.BlockSpec((tm, d), lambda i: (i, 0)),
        out_shape=jax.ShapeDtypeStruct((m, d), BF16),
        compiler_params=_params(("parallel",)),
        name="rmsnorm_cast",
    )(x, g.reshape(1, d))


def _postnorm_residual_kernel(y_ref, x_ref, g_ref, o_ref):
    y = y_ref[...]
    ms = jnp.mean(y * y, axis=-1, keepdims=True)
    o_ref[...] = x_ref[...] + y * lax.rsqrt(ms + NORM_EPS) * g_ref[...]


def postnorm_residual(y, x, g):
    m, d = x.shape
    tm = _pick_tile(m, 256, 8)
    row = pl.BlockSpec((tm, d), lambda i: (i, 0))
    return pl.pallas_call(
        _postnorm_residual_kernel,
        grid=(m // tm,),
        in_specs=[row, row, pl.BlockSpec((1, d), lambda i: (0, 0))],
        out_specs=row,
        out_shape=jax.ShapeDtypeStruct((m, d), F32),
        compiler_params=_params(("parallel",)),
        name="postnorm_residual",
    )(y, x, g.reshape(1, d))


def _matmul_kernel(a_ref, w_ref, o_ref):
    o_ref[...] = _dot(a_ref[...], w_ref[...].astype(BF16)).astype(o_ref.dtype)


def matmul(a, w, n_cols, layer=None, tn_target=512, tm_target=1408):
    m, k = a.shape
    tm = _pick_tile(m, tm_target, 16)
    tn = n_cols if n_cols < LANES else _pick_tile(n_cols, tn_target, LANES)
    if layer is None:
        w_spec = pl.BlockSpec((k, tn), lambda i, j: (0, j))
    else:
        w_spec = pl.BlockSpec((None, k, tn), lambda i, j: (layer, 0, j))
    return pl.pallas_call(
        _matmul_kernel,
        grid=(m // tm, n_cols // tn),
        in_specs=[pl.BlockSpec((tm, k), lambda i, j: (i, 0), pipeline_mode=pl.Buffered(1)), w_spec],
        out_specs=pl.BlockSpec((tm, tn), lambda i, j: (i, j)),
        out_shape=jax.ShapeDtypeStruct((m, n_cols), F32),
        compiler_params=_params(("parallel", "arbitrary")),
        name="matmul",
    )(a, w)


def _logf_cumsum_kernel(f_ref, b_ref, init_ref, lf_ref, cum_ref, carry, *, apply_gate):
    @pl.when(pl.program_id(1) == 0)
    def _():
        carry[...] = init_ref[0]

    x = f_ref[...]
    if apply_gate:
        x = _log_sigmoid(x + b_ref[...])
    tb = x.shape[0]
    r = lax.broadcasted_iota(jnp.int32, (tb, tb), 0)
    c = lax.broadcasted_iota(jnp.int32, (tb, tb), 1)
    tri = (c <= r).astype(F32)
    cum = jnp.dot(tri, x, precision=HIGHEST, preferred_element_type=F32) + carry[...]
    lf_ref[...] = x
    cum_ref[...] = cum
    carry[...] = cum[tb - 1:tb, :]


def logf_cumsum(f, row_off, nseq, t, bias, init, apply_gate):
    h = f.shape[1]
    tb = _pick_tile(t, 512, 8)
    nt = t // tb
    off = row_off // tb
    assert row_off % tb == 0
    blk = pl.BlockSpec((tb, h), lambda s, i: (s * nt + i, 0))
    return pl.pallas_call(
        functools.partial(_logf_cumsum_kernel, apply_gate=apply_gate),
        grid=(nseq, nt),
        in_specs=[pl.BlockSpec((tb, h), lambda s, i: (off + s * nt + i, 0)),
                  pl.BlockSpec((1, h), lambda s, i: (0, 0)),
                  pl.BlockSpec((1, 1, h), lambda s, i: (s, 0, 0))],
        out_specs=[blk, blk],
        out_shape=[jax.ShapeDtypeStruct((nseq * t, h), F32)] * 2,
        scratch_shapes=[pltpu.VMEM((1, h), F32)],
        compiler_params=_params(("parallel", "arbitrary")),
        name="logf_cumsum",
    )(f, bias.reshape(1, h), init)


def _fox_select_matrices(h, k_side):
    sel = np.zeros((4, h, h * LANES), np.float32)
    part_lane0, ones_lane0 = (3, 0) if k_side else (0, 3)
    for hh in range(h):
        for p in range(3):
            sel[p, hh, hh * LANES + part_lane0 + p] = 1.0
            sel[3, hh, hh * LANES + ones_lane0 + p] = 1.0
    return jnp.asarray(sel, BF16)


def _fox_build_kernel(x_ref, cum_ref, g_ref, sel_ref, *refs, normalize, emit_norm, aliased, scale, negate):
    refs = refs[1:] if aliased else refs
    xp_ref = refs[0]
    c = cum_ref[...] * LOG2E
    if negate:
        c = -c
    c_hi = c.astype(BF16)
    r1 = c - c_hi.astype(F32)
    c_mid = r1.astype(BF16)
    c_lo = (r1 - c_mid.astype(F32)).astype(BF16)
    extra = (_dot(c_hi, sel_ref[0]) + _dot(c_mid, sel_ref[1]) + _dot(c_lo, sel_ref[2])
             + _dot(jnp.ones_like(c_hi), sel_ref[3]))
    g = g_ref[...]
    for h in range(c.shape[1]):
        xs = x_ref[:, h * LANES:(h + 1) * LANES]
        if normalize:
            xs = xs * lax.rsqrt(jnp.mean(xs * xs, axis=-1, keepdims=True) + NORM_EPS) * g
        if emit_norm:
            refs[1][:, h * LANES:(h + 1) * LANES] = xs
        xp_ref[:, 2 * h * LANES:(2 * h + 1) * LANES] = (xs * scale).astype(BF16)
        xp_ref[:, (2 * h + 1) * LANES:(2 * h + 2) * LANES] = extra[:, h * LANES:(h + 1) * LANES].astype(BF16)


def fox_build(x, col_block, row_off, rows, cum, gain, k_side, normalize, scale, emit_norm=False,
              norm_buf=None, slot=0):
    h = cum.shape[1]
    w = h * LANES
    tm = _pick_tile(rows, 128, 16)
    assert row_off % tm == 0
    off = row_off // tm
    if x.ndim == 3:
        x_spec = pl.BlockSpec((None, tm, w), lambda i: (slot, off + i, col_block))
    else:
        x_spec = pl.BlockSpec((tm, w), lambda i: (off + i, col_block))
    in_specs = [x_spec,
                pl.BlockSpec((tm, h), lambda i: (i, 0)),
                pl.BlockSpec((1, LANES), lambda i: (0, 0)),
                pl.BlockSpec((4, h, w), lambda i: (0, 0, 0))]
    args = [x, cum, gain.reshape(1, LANES), _fox_select_matrices(h, k_side)]
    out_specs = [pl.BlockSpec((tm, 2 * w), lambda i: (i, 0))]
    out_shape = [jax.ShapeDtypeStruct((rows, 2 * w), BF16)]
    aliases = {}
    if emit_norm and norm_buf is not None:
        in_specs.append(ANY)
        args.append(norm_buf)
        out_specs.append(pl.BlockSpec((None, tm, w), lambda i: (slot, i, 0)))
        out_shape.append(jax.ShapeDtypeStruct(norm_buf.shape, F32))
        aliases = {4: 1}
    elif emit_norm:
        out_specs.append(pl.BlockSpec((tm, w), lambda i: (i, 0)))
        out_shape.append(jax.ShapeDtypeStruct((rows, w), F32))
    return pl.pallas_call(
        functools.partial(_fox_build_kernel, normalize=normalize, emit_norm=emit_norm, aliased=bool(aliases),
                          scale=scale, negate=k_side),
        grid=(rows // tm,),
        in_specs=in_specs,
        out_specs=out_specs,
        out_shape=out_shape,
        input_output_aliases=aliases,
        compiler_params=_params(("parallel",)),
        name="fox_build",
    )(*args)


def _fox_attn_kernel(q_ref, k_ref, v_ref, z_ref, ybuf_ref, vbuf_ref, o_ref, vout_ref, *, tq, tk, nsub):
    i = pl.program_id(1)

    @pl.when(i == 0)
    def _():
        vout_ref[...] = v_ref[...]

    sq = tq // nsub
    row0 = i * tq
    n_full = (row0 + 1) // tk
    n_all = (row0 + tq + tk - 1) // tk

    def step(j, carry, masked):
        start = pl.multiple_of(j * tk, tk)
        k = k_ref[pl.ds(start, tk), :]
        v = v_ref[pl.ds(start, tk), :].astype(BF16)
        out = []
        for u in range(nsub):
            m, l, acc = carry[u]
            s = _dot_nt(q_ref[u * sq:(u + 1) * sq, :], k)
            if masked:
                r = row0 + u * sq + lax.broadcasted_iota(jnp.int32, (sq, tk), 0)
                c = start + lax.broadcasted_iota(jnp.int32, (sq, tk), 1)
                s = jnp.where(c <= r, s, NEG)
            m_new = jnp.maximum(m, jnp.max(s, axis=-1, keepdims=True))
            alpha = jnp.exp2(m - m_new)
            p = jnp.exp2(s - m_new)
            l = alpha * l + jnp.sum(p, axis=-1, keepdims=True)
            acc = alpha * acc + _dot(p.astype(BF16), v)
            out.append((m_new, l, acc))
        return tuple(out)

    carry = tuple((jnp.full((sq, 1), NEG, F32), jnp.zeros((sq, 1), F32), jnp.zeros((sq, LANES), F32))
                  for _ in range(nsub))
    carry = lax.fori_loop(0, n_full, functools.partial(step, masked=False), carry)
    carry = lax.fori_loop(n_full, n_all, functools.partial(step, masked=True), carry)
    for u in range(nsub):
        _, l, acc = carry[u]
        z = z_ref[u * sq:(u + 1) * sq, :]
        o_ref[u * sq:(u + 1) * sq, :] = (acc / l * (z * _sigmoid(z))).astype(o_ref.dtype)


def fox_attention_prompt(qp, kp, proj, nseq, t, nheads, ybuf, vbuf, slot):
    tq = _pick_tile(t, FOX_TQ, 16 * FOX_SUB)
    nq = t // tq
    blk = lambda g, i: (g // nheads * nq + i, g % nheads)
    return pl.pallas_call(
        functools.partial(_fox_attn_kernel, tq=tq, tk=tq, nsub=FOX_SUB),
        grid=(nseq * nheads, nq),
        in_specs=[pl.BlockSpec((tq, 2 * LANES), blk),
                  pl.BlockSpec((t, 2 * LANES), lambda g, i: (g // nheads, g % nheads)),
                  pl.BlockSpec((t, LANES), lambda g, i: (g // nheads, 2 * nheads + g % nheads)),
                  pl.BlockSpec((tq, LANES), lambda g, i: (g // nheads * nq + i, 3 * nheads + g % nheads)),
                  ANY, ANY],
        out_specs=[pl.BlockSpec((tq, LANES), blk),
                   pl.BlockSpec((None, t, LANES), lambda g, i: (slot, g // nheads, g % nheads))],
        out_shape=[jax.ShapeDtypeStruct(ybuf.shape, ybuf.dtype), jax.ShapeDtypeStruct(vbuf.shape, vbuf.dtype)],
        input_output_aliases={4: 0, 5: 1},
        compiler_params=_params(("parallel", "arbitrary")),
        name="fox_attention",
    )(qp, kp, proj, proj, ybuf, vbuf)


def _fox_attn_sample_kernel(q_ref, kc_ref, kn_ref, vc_ref, vn_ref, z_ref, ybuf_ref, o_ref, *, nh):
    ts = q_ref.shape[0]
    r = lax.broadcasted_iota(jnp.int32, (ts, ts), 0)
    c = lax.broadcasted_iota(jnp.int32, (ts, ts), 1)
    for u in range(nh):
        q = q_ref[:, 2 * u * LANES:2 * (u + 1) * LANES]
        s_c = _dot_nt(q, kc_ref[:, 2 * u * LANES:2 * (u + 1) * LANES])
        s_n = jnp.where(c <= r, _dot_nt(q, kn_ref[:, 2 * u * LANES:2 * (u + 1) * LANES]), NEG)
        m = jnp.maximum(jnp.max(s_c, axis=-1, keepdims=True), jnp.max(s_n, axis=-1, keepdims=True))
        p_c = jnp.exp2(s_c - m)
        p_n = jnp.exp2(s_n - m)
        l = jnp.sum(p_c, axis=-1, keepdims=True) + jnp.sum(p_n, axis=-1, keepdims=True)
        acc = (_dot(p_c.astype(BF16), vc_ref[:, u * LANES:(u + 1) * LANES].astype(BF16))
               + _dot(p_n.astype(BF16), vn_ref[:, u * LANES:(u + 1) * LANES].astype(BF16)))
        z = z_ref[:, u * LANES:(u + 1) * LANES]
        o_ref[:, u * LANES:(u + 1) * LANES] = (acc / l * (z * _sigmoid(z))).astype(o_ref.dtype)


def fox_attention_sample(qp, kp_c, kp_n, cache_v, slot, proj, row_off, nseq, ts, p, nheads, ybuf):
    nh = _pick_tile(nheads, FOX_HEADS_PER_STEP, 1)
    ng = nheads // nh
    assert row_off % ts == 0
    off = row_off // ts
    return pl.pallas_call(
        functools.partial(_fox_attn_sample_kernel, nh=nh),
        grid=(nseq, ng),
        in_specs=[pl.BlockSpec((ts, 2 * nh * LANES), lambda s, g: (s, g)),
                  pl.BlockSpec((p, 2 * nh * LANES), lambda s, g: (s, g)),
                  pl.BlockSpec((ts, 2 * nh * LANES), lambda s, g: (s, g)),
                  pl.BlockSpec((None, p, nh * LANES), lambda s, g: (slot, s, g)),
                  pl.BlockSpec((ts, nh * LANES), lambda s, g: (off + s, 2 * ng + g)),
                  pl.BlockSpec((ts, nh * LANES), lambda s, g: (off + s, 3 * ng + g)),
                  ANY],
        out_specs=pl.BlockSpec((ts, nh * LANES), lambda s, g: (off + s, g)),
        out_shape=jax.ShapeDtypeStruct(ybuf.shape, ybuf.dtype),
        input_output_aliases={6: 0},
        compiler_params=_params(("parallel", "arbitrary")),
        name="fox_attention_sample",
    )(qp, kp_c, kp_n, cache_v, proj, proj, ybuf)


def fox_layer(h, mp_rows, b, t, bs, ts, cache_k, cache_v, cache_lf, w_in, slot, b_f, q_gain, k_gain, kbuf, vbuf):
    nheads = b_f.shape[0]
    w = nheads * LANES
    p = cache_k.shape[2]
    qscale = float(LANES) ** -0.5 * LOG2E
    proj = matmul(h, w_in, 4 * w, layer=slot)
    fgate = matmul(h, w_in[slot, :, 4 * w:], nheads)
    ybuf = jnp.zeros((h.shape[0], w), BF16)
    lf_p, cum_p = logf_cumsum(fgate, 0, b, t, b_f, jnp.zeros((b, 1, nheads), F32), True)
    qp_p, = fox_build(proj, 0, 0, mp_rows, cum_p, q_gain, False, True, qscale)
    kp_p, kbuf = fox_build(proj, 1, 0, mp_rows, cum_p, k_gain, True, True, 1.0, True, kbuf, slot)
    ybuf, vbuf = fox_attention_prompt(qp_p, kp_p, proj, b, t, nheads, ybuf, vbuf, slot)
    ms_rows = bs * ts
    _, cum_c = logf_cumsum(cache_lf.reshape(bs * p, nheads), 0, bs, p, b_f, jnp.zeros((bs, 1, nheads), F32), False)
    init_s = cum_c.reshape(bs, p, nheads)[:, p - 1:, :]
    lf_s, cum_s = logf_cumsum(fgate, mp_rows, bs, ts, b_f, init_s, True)
    qp_s, = fox_build(proj, 0, mp_rows, ms_rows, cum_s, q_gain, False, True, qscale)
    kp_s, kn_s = fox_build(proj, 1, mp_rows, ms_rows, cum_s, k_gain, True, True, 1.0, True)
    n_layers = cache_k.shape[0]
    kp_c, = fox_build(cache_k.reshape(n_layers, bs * p, w), 0, 0, bs * p, cum_c, k_gain, True, False, 1.0, slot=slot)
    ybuf = fox_attention_sample(qp_s, kp_c, kp_s, cache_v.reshape(n_layers, bs * p, w), slot, proj, mp_rows, bs, ts, p,
                                nheads, ybuf)
    v_s = proj[mp_rows:, 2 * w:3 * w]
    hd = LANES
    outs = (lf_p.reshape(b, t, nheads), kn_s.reshape(bs, ts, nheads, hd), v_s.reshape(bs, ts, nheads, hd),
            lf_s.reshape(bs, ts, nheads))
    return ybuf, kbuf, vbuf, outs


def _mlstm_kernel(q_ref, k_ref, v_ref, o_ref, z_ref, gate_ref, bi_ref, bf_ref, gain_ref,
                  c0_ref, n0_ref, m0_ref, ybuf_ref, y_ref, c_ref, n_ref, m_ref, *, nheads, dk, dv):
    @pl.when(pl.program_id(1) == 0)
    def _():
        c_ref[...] = c0_ref[...]
        n_ref[...] = n0_ref[...]
        m_ref[...] = m0_ref[...]

    ln = q_ref.shape[0]
    gates = gate_ref[...]
    log_i = gates[:, :nheads] + bi_ref[...]
    log_f = _log_sigmoid(gates[:, nheads:] + bf_ref[...])
    r = lax.broadcasted_iota(jnp.int32, (ln, ln), 0)
    c = lax.broadcasted_iota(jnp.int32, (ln, ln), 1)
    causal = c <= r
    eye = c == r
    b_all = jnp.dot(causal.astype(F32), log_f, precision=HIGHEST, preferred_element_type=F32)
    kscale = float(dk) ** -0.5
    for h in range(nheads):
        b_col = b_all[:, h:h + 1]
        li_col = log_i[:, h:h + 1]
        b_row = jnp.sum(jnp.where(eye, b_col, 0.0), axis=0, keepdims=True)
        li_row = jnp.sum(jnp.where(eye, li_col, 0.0), axis=0, keepdims=True)
        m_prev = m_ref[0, h:h + 1, 0:1]
        d = jnp.where(causal, b_col - b_row + li_row, -jnp.inf)
        inter = b_col + m_prev
        m_t = jnp.maximum(inter, jnp.max(d, axis=-1, keepdims=True))
        q = q_ref[:, h * dk:(h + 1) * dk]
        k = k_ref[:, h * dk:(h + 1) * dk] * kscale
        v = v_ref[:, h * dv:(h + 1) * dv]
        qb = q.astype(BF16)
        kb = k.astype(BF16)
        wmat = jnp.exp(d - m_t) * _dot_nt(qb, kb)
        inter_w = jnp.exp(inter - m_t)
        cst = c_ref[0, h]
        nst = n_ref[0, h:h + 1, :]
        num = _dot(wmat.astype(BF16), v.astype(BF16)) + inter_w * _dot_nt(qb, cst.astype(BF16))
        den = jnp.sum(wmat, axis=-1, keepdims=True) + inter_w * jnp.sum(q * nst, axis=-1, keepdims=True)
        hid = num / jnp.maximum(jnp.abs(den), jnp.exp(-m_t))
        b_last = b_col[ln - 1:ln, :]
        g_col = b_last - b_col + li_col
        m_new = jnp.maximum(b_last + m_prev, jnp.max(g_col, axis=0, keepdims=True))
        wk = jnp.exp(g_col - m_new)
        carry_w = jnp.exp(b_last + m_prev - m_new)
        c_ref[0, h] = carry_w * cst + _dot_tn((wk * v).astype(BF16), kb)
        n_ref[0, h:h + 1, :] = carry_w * nst + jnp.sum(wk * k, axis=0, keepdims=True)
        m_ref[0, h:h + 1, :] = jnp.broadcast_to(m_new, (1, LANES))
        hn = hid * lax.rsqrt(jnp.mean(hid * hid, axis=-1, keepdims=True) + NORM_EPS)
        hn = hn * gain_ref[:, h * dv:(h + 1) * dv]
        og = o_ref[:, h * dv:(h + 1) * dv]
        zg = z_ref[:, h * dv:(h + 1) * dv]
        y_ref[:, h * dv:(h + 1) * dv] = (hn * _sigmoid(og) * (zg * _sigmoid(zg))).astype(y_ref.dtype)


def mlstm_scan(proj, gates, row_off, nseq, t, ln, c0, n0, m0, b_i, b_f, gain, ybuf):
    nheads, dv, dk = c0.shape[1:]
    nc = t // ln
    assert row_off % ln == 0
    off = row_off // ln
    qk_w, v_w = nheads * dk, nheads * dv
    assert v_w == 2 * qk_w

    def rows(width, cb):
        return pl.BlockSpec((ln, width), lambda s, i: (off + s * nc + i, cb))

    m0b = jnp.broadcast_to(m0[:, :, None], (nseq, nheads, LANES))
    y, c, n, m = pl.pallas_call(
        functools.partial(_mlstm_kernel, nheads=nheads, dk=dk, dv=dv),
        grid=(nseq, nc),
        in_specs=[rows(qk_w, 0), rows(qk_w, 1), rows(v_w, 1), rows(v_w, 2), rows(v_w, 3),
                  rows(2 * nheads, 0),
                  pl.BlockSpec((1, nheads), lambda s, i: (0, 0)),
                  pl.BlockSpec((1, nheads), lambda s, i: (0, 0)),
                  pl.BlockSpec((1, v_w), lambda s, i: (0, 0)),
                  pl.BlockSpec((1, nheads, dv, dk), lambda s, i: (s, 0, 0, 0)),
                  pl.BlockSpec((1, nheads, dk), lambda s, i: (s, 0, 0)),
                  pl.BlockSpec((1, nheads, LANES), lambda s, i: (s, 0, 0)),
                  ANY],
        out_specs=[rows(v_w, 0),
                   pl.BlockSpec((1, nheads, dv, dk), lambda s, i: (s, 0, 0, 0)),
                   pl.BlockSpec((1, nheads, dk), lambda s, i: (s, 0, 0)),
                   pl.BlockSpec((1, nheads, LANES), lambda s, i: (s, 0, 0))],
        out_shape=[jax.ShapeDtypeStruct(ybuf.shape, ybuf.dtype),
                   jax.ShapeDtypeStruct((nseq, nheads, dv, dk), F32),
                   jax.ShapeDtypeStruct((nseq, nheads, dk), F32),
                   jax.ShapeDtypeStruct((nseq, nheads, LANES), F32)],
        input_output_aliases={12: 0},
        compiler_params=_params(("parallel", "arbitrary")),
        name="mlstm_scan",
    )(proj, proj, proj, proj, proj, gates, b_i.reshape(1, nheads), b_f.reshape(1, nheads),
      gain.reshape(1, v_w), c0, n0, m0b, ybuf)
    return y, c, n, m[:, :, 0]


def mlstm_layer(h, mp_rows, b, t, bs, ts, c0, n0, m0, w_in, slot, b_i, b_f, gain):
    nheads, dv, dk = c0.shape[1:]
    main = 2 * nheads * dk + 3 * nheads * dv
    proj = matmul(h, w_in, main, layer=slot)
    gates = matmul(h, w_in[slot, :, main:], 2 * nheads)
    ybuf = jnp.zeros((h.shape[0], nheads * dv), BF16)
    zc = jnp.zeros((b, nheads, dv, dk), F32)
    ybuf, c_p, n_p, m_p = mlstm_scan(proj, gates, 0, b, t, min(CHUNK, t), zc, zc[:, :, 0, :], zc[:, :, 0, 0],
                                     b_i, b_f, gain, ybuf)
    ybuf, c_s, n_s, m_s = mlstm_scan(proj, gates, mp_rows, bs, ts, ts, c0, n0, m0, b_i, b_f, gain, ybuf)
    return ybuf, (c_p, n_p, m_p, c_s, n_s, m_s)


def _gdn_conv_kernel(x_ref, halo_ref, init_ref, w_ref, sc_ref, o_ref, *, l2norm):
    tm = x_ref.shape[0]
    prev = jnp.where(pl.program_id(1) == 0, init_ref[0], halo_ref[...])
    xc = jnp.concatenate([prev, x_ref[...]], axis=0)
    w = w_ref[...]
    acc = w[3:4] * xc[8:8 + tm] + w[2:3] * xc[7:7 + tm] + w[1:2] * xc[6:6 + tm] + w[0:1] * xc[5:5 + tm]
    act = acc * _sigmoid(acc)
    if l2norm:
        for g in range(act.shape[1] // LANES):
            a = act[:, g * LANES:(g + 1) * LANES]
            a = a * lax.rsqrt(jnp.sum(a * a, axis=-1, keepdims=True) + NORM_EPS)
            o_ref[:, g * LANES:(g + 1) * LANES] = (a * sc_ref[:, g * LANES:(g + 1) * LANES]).astype(o_ref.dtype)
    else:
        o_ref[...] = act.astype(o_ref.dtype)


def gdn_conv(proj, row_off, nseq, t, ch_off, ch, init, conv_w, slot, scale, l2norm, out_dtype):
    tm = _pick_tile(t, 512, 8)
    tc = _pick_tile(ch, 1024, LANES)
    nt = t // tm
    assert row_off % tm == 0 and ch_off % tc == 0 and tm % 8 == 0
    off, coff, sub = row_off // tm, ch_off // tc, tm // 8
    return pl.pallas_call(
        functools.partial(_gdn_conv_kernel, l2norm=l2norm),
        grid=(nseq, nt, ch // tc),
        in_specs=[pl.BlockSpec((tm, tc), lambda s, i, j: (off + s * nt + i, coff + j)),
                  pl.BlockSpec((8, tc), lambda s, i, j: (jnp.maximum((off + s * nt + i) * sub - 1, 0), coff + j)),
                  pl.BlockSpec((1, 8, tc), lambda s, i, j: (s, 0, coff + j)),
                  pl.BlockSpec((None, conv_w.shape[1], tc), lambda s, i, j: (slot, 0, coff + j)),
                  pl.BlockSpec((1, tc), lambda s, i, j: (0, j))],
        out_specs=pl.BlockSpec((tm, tc), lambda s, i, j: (s * nt + i, j)),
        out_shape=jax.ShapeDtypeStruct((nseq * t, ch), out_dtype),
        compiler_params=_params(("parallel", "arbitrary", "arbitrary")),
        name="gdn_conv",
    )(proj, proj, init, conv_w, scale)


def _unit_lower_inverse(a, r, c, levels):
    x = jnp.where(r == c, 1.0, 0.0) - jnp.where(((r >> 1) == (c >> 1)) & (c < r), a, 0.0)
    for sh in range(1, levels):
        off = jnp.where(((r >> (sh + 1)) == (c >> (sh + 1))) & ((r >> sh) != (c >> sh)) & (c < r), a, 0.0)
        xb = x.astype(BF16)
        x = x - _bmm(_bmm(xb, off.astype(BF16)).astype(BF16), xb)
    return x


def _gdn_kernel(qk_ref, v_ref, z_ref, gate_ref, alog_ref, dt_ref, gain_ref, expand_ref, ws0_ref, ybuf_ref,
                y_ref, ws_ref, *, kheads, vheads):
    @pl.when(pl.program_id(1) == 0)
    def _():
        ws_ref[...] = ws0_ref[...]

    ln = v_ref.shape[0]
    hd = LANES
    pack = MXU_DIM // ln
    ngrp = vheads // pack
    rr = pack * ln
    lsh = ln.bit_length() - 1
    rep = vheads // kheads
    gates = gate_ref[...]
    g_all = -jnp.exp(alog_ref[...]) * _softplus(gates[:, :vheads] + dt_ref[...])
    beta_all = _sigmoid(gates[:, vheads:])
    rl = lax.broadcasted_iota(jnp.int32, (ln, ln), 0)
    cl = lax.broadcasted_iota(jnp.int32, (ln, ln), 1)
    gsum_all = jnp.dot((cl <= rl).astype(F32), g_all, precision=HIGHEST, preferred_element_type=F32)
    glast_row = gsum_all[ln - 1:ln, :]
    decay_row = jnp.exp(jnp.dot(glast_row, expand_ref[...], precision=HIGHEST, preferred_element_type=F32))

    def stack_cols(a):
        return jnp.stack([jnp.concatenate([a[:, g * pack + p:g * pack + p + 1] for p in range(pack)], axis=0)
                          for g in range(ngrp)], axis=0)

    def stack_heads(ref, head_of):
        return jnp.stack([jnp.concatenate([ref[:, head_of(g * pack + p) * hd:(head_of(g * pack + p) + 1) * hd]
                                           for p in range(pack)], axis=0) for g in range(ngrp)], axis=0)

    gcol = stack_cols(gsum_all)
    bt = stack_cols(beta_all)
    glast = stack_cols(jnp.broadcast_to(glast_row, (ln, vheads)))
    r = lax.broadcasted_iota(jnp.int32, (rr, rr), 0)
    c = lax.broadcasted_iota(jnp.int32, (rr, rr), 1)
    blk = (r >> lsh) == (c >> lsh)
    incl = blk & (c <= r)
    strict = blk & (c < r)
    grow = jnp.sum(jnp.where(r == c, gcol, 0.0), axis=1, keepdims=True)
    dincl = jnp.where(incl, jnp.exp(jnp.where(incl, gcol - grow, 0.0)), 0.0)
    qst = stack_heads(qk_ref, lambda h: h // rep)
    kst = stack_heads(qk_ref, lambda h: kheads + h // rep)
    vst = stack_heads(v_ref, lambda h: h)
    kk = _bmm_nt(kst, kst)
    qk = _bmm_nt(qst, kst)
    x = _unit_lower_inverse(jnp.where(strict, bt * dincl * kk, 0.0), r, c, lsh)
    eg = jnp.exp(gcol)
    wsb = ws_ref[0].astype(BF16)
    gw = pack * hd
    rowblk = lax.broadcasted_iota(jnp.int32, (rr, 1), 0) >> lsh

    def own_block(wide):
        out = jnp.where(rowblk == 0, wide[:, :hd], 0.0)
        for p in range(1, pack):
            out = out + jnp.where(rowblk == p, wide[:, p * hd:(p + 1) * hd], 0.0)
        return out

    ksd = jnp.stack([own_block(_dot(kst[g], wsb[:, g * gw:(g + 1) * gw])) for g in range(ngrp)], axis=0)
    qsd = jnp.stack([own_block(_dot(qst[g], wsb[:, g * gw:(g + 1) * gw])) for g in range(ngrp)], axis=0)
    rhs = bt * (vst - eg * ksd)
    u = _bmm(x.astype(BF16), rhs.astype(BF16))
    o = eg * qsd + _bmm((dincl * qk).astype(BF16), u.astype(BF16))
    uw = (jnp.exp(glast - gcol) * u).astype(BF16)
    own = ((lax.broadcasted_iota(jnp.int32, (rr, gw), 0) >> lsh)
           == (lax.broadcasted_iota(jnp.int32, (rr, gw), 1) >> (hd.bit_length() - 1)))
    gain = gain_ref[...]
    for g in range(ngrp):
        uwbd = jnp.where(own, jnp.concatenate([uw[g]] * pack, axis=1), jnp.zeros((), BF16))
        cols = slice(g * gw, (g + 1) * gw)
        ws_ref[0, :, cols] = decay_row[:, cols] * ws_ref[0, :, cols] + _dot_tn(kst[g], uwbd)
        for p in range(pack):
            h = g * pack + p
            oh = o[g, p * ln:(p + 1) * ln, :]
            on = oh * lax.rsqrt(jnp.mean(oh * oh, axis=-1, keepdims=True) + NORM_EPS) * gain
            zg = z_ref[:, h * hd:(h + 1) * hd]
            y_ref[:, h * hd:(h + 1) * hd] = (on * (zg * _sigmoid(zg))).astype(y_ref.dtype)


def gdn_scan(qk, v, proj, z_col_block, gates, row_off, nseq, t, ln, s0, a_log, dt_bias, gain, ybuf):
    vheads, dv, dk = s0.shape[1:]
    v_w = vheads * LANES
    kheads = qk.shape[1] // (2 * LANES)
    nc = t // ln
    assert row_off % ln == 0 and MXU_DIM % ln == 0 and vheads % (MXU_DIM // ln) == 0 and dv == LANES and dk == LANES
    off = row_off // ln
    ws0 = jnp.transpose(s0, (0, 3, 1, 2)).reshape(nseq, dk, v_w)
    expand = jnp.asarray(np.kron(np.eye(vheads, dtype=np.float32), np.ones((1, LANES), np.float32)))
    local = lambda width: pl.BlockSpec((ln, width), lambda s, i: (s * nc + i, 0))
    state = pl.BlockSpec((1, dk, v_w), lambda s, i: (s, 0, 0))
    vec = lambda width: pl.BlockSpec((1, width), lambda s, i: (0, 0))
    ybuf, ws = pl.pallas_call(
        functools.partial(_gdn_kernel, kheads=kheads, vheads=vheads),
        grid=(nseq, nc),
        in_specs=[local(qk.shape[1]), local(v_w),
                  pl.BlockSpec((ln, v_w), lambda s, i: (off + s * nc + i, z_col_block)),
                  pl.BlockSpec((ln, 2 * vheads), lambda s, i: (off + s * nc + i, 0)),
                  vec(vheads), vec(vheads), vec(LANES),
                  pl.BlockSpec((vheads, v_w), lambda s, i: (0, 0)), state, ANY],
        out_specs=[pl.BlockSpec((ln, v_w), lambda s, i: (off + s * nc + i, 0)), state],
        out_shape=[jax.ShapeDtypeStruct(ybuf.shape, ybuf.dtype),
                   jax.ShapeDtypeStruct((nseq, dk, v_w), F32)],
        input_output_aliases={9: 0},
        compiler_params=_params(("parallel", "arbitrary")),
        name="gdn_scan",
    )(qk, v, proj, gates, a_log.reshape(1, vheads), dt_bias.reshape(1, vheads), gain.reshape(1, LANES),
      expand, ws0, ybuf)
    return ybuf, jnp.transpose(ws.reshape(nseq, dk, vheads, dv), (0, 2, 3, 1))


def gdn_layer(h, mp_rows, b, t, bs, ts, s0, conv0, w_in, slot, conv_w, a_log, dt_bias, gain):
    vheads = s0.shape[1]
    v_w = vheads * LANES
    taps, ch = conv_w.shape[1:]
    qk_w = ch - v_w
    proj = matmul(h, w_in, ch + v_w, layer=slot)
    gates = matmul(h, w_in[slot, :, ch + v_w:], 2 * vheads)
    scale = jnp.concatenate([jnp.full((1, qk_w // 2), float(LANES) ** -0.5, F32), jnp.ones((1, qk_w // 2), F32)], axis=1)
    ones = jnp.ones((1, v_w), F32)
    ybuf = jnp.zeros((h.shape[0], v_w), BF16)
    outs = []
    for row_off, nseq, tt, ln, init_rows, st0 in (
            (0, b, t, min(CHUNK, t), jnp.zeros((b, taps - 1, ch), F32), jnp.zeros((b,) + s0.shape[1:], F32)),
            (mp_rows, bs, ts, ts, conv0, s0)):
        init = jnp.pad(init_rows, ((0, 0), (8 - (taps - 1), 0), (0, 0)))
        qk = gdn_conv(proj, row_off, nseq, tt, 0, qk_w, init, conv_w, slot, scale, True, BF16)
        v = gdn_conv(proj, row_off, nseq, tt, qk_w, v_w, init, conv_w, slot, ones, False, F32)
        ybuf, s_new = gdn_scan(qk, v, proj, ch // v_w, gates, row_off, nseq, tt, ln, st0, a_log[slot], dt_bias[slot],
                               gain[slot], ybuf)
        conv_state = proj[row_off:row_off + nseq * tt, :ch].reshape(nseq, tt, ch)[:, tt - (taps - 1):, :]
        outs += [s_new, conv_state]
    return ybuf, tuple(outs)


def kernel(x_prompt, x_sample, cache_fox_k, cache_fox_v, cache_fox_logf, state_mlstm_c, state_mlstm_n, state_mlstm_m, state_gdn_s, state_gdn_conv, pre_norm, post_norm, fox_w_in, fox_b_f, fox_q_norm, fox_k_norm, fox_w_out, mlstm_w_in, mlstm_b_i, mlstm_b_f, mlstm_h_norm, mlstm_w_out, gdn_w_in, gdn_conv_w, gdn_a_log, gdn_dt_bias, gdn_o_norm, gdn_w_out):
    b, t, d = x_prompt.shape
    bs, ts, _ = x_sample.shape
    mp_rows = b * t
    n_fox = fox_w_in.shape[0]
    x = jnp.concatenate([x_prompt.reshape(mp_rows, d), x_sample.reshape(bs * ts, d)], axis=0)
    kbuf = jnp.zeros((n_fox, mp_rows, fox_w_out.shape[1]), F32)
    vbuf = jnp.zeros((n_fox, mp_rows, fox_w_out.shape[1]), F32)
    fox, ml, gd = [], [], []
    for layer in range(pre_norm.shape[0]):
        kind, j = layer % 3, layer // 3
        h = rmsnorm_cast(x, pre_norm[layer])
        if kind == 0:
            y, kbuf, vbuf, outs = fox_layer(h, mp_rows, b, t, bs, ts, cache_fox_k, cache_fox_v, cache_fox_logf[j],
                                            fox_w_in, j, fox_b_f[j], fox_q_norm[j], fox_k_norm[j], kbuf, vbuf)
            fox.append(outs)
            w_out = fox_w_out
        elif kind == 1:
            y, outs = mlstm_layer(h, mp_rows, b, t, bs, ts, state_mlstm_c[j], state_mlstm_n[j], state_mlstm_m[j],
                                  mlstm_w_in, j, mlstm_b_i[j], mlstm_b_f[j], mlstm_h_norm[j])
            ml.append(outs)
            w_out = mlstm_w_out
        else:
            y, outs = gdn_layer(h, mp_rows, b, t, bs, ts, state_gdn_s[j], state_gdn_conv[j], gdn_w_in, j,
                                gdn_conv_w, gdn_a_log, gdn_dt_bias, gdn_o_norm)
            gd.append(outs)
            w_out = gdn_w_out
        x = postnorm_residual(matmul(y, w_out, d, layer=j), x, post_norm[layer])
    stack = lambda group, idx: jnp.stack([o[idx] for o in group])
    nh = fox_b_f.shape[1]
    kv_shape = (n_fox, b, t, nh, LANES)
    return ((x[:mp_rows].reshape(b, t, d), x[mp_rows:].reshape(bs, ts, d),
             kbuf.reshape(kv_shape), vbuf.reshape(kv_shape))
            + tuple(stack(fox, i) for i in range(4))
            + tuple(stack(ml, i) for i in range(6))
            + tuple(stack(gd, i) for i in range(4)))
```

```python
import functools
import math

import numpy as np
import jax
import jax.numpy as jnp
from jax import lax
from jax.experimental import pallas as pl
from jax.experimental.pallas import tpu as pltpu

F32 = jnp.float32
BF16 = jnp.bfloat16
NORM_EPS = 1e-6
CHUNK = 64
LANES = 128
MXU_DIM = 256
NEG = -1e30
VMEM_LIMIT = 56 * 1024 * 1024
FOX_TQ = 512
FOX_HEADS_PER_STEP = 8
LOG2E = math.log2(math.e)
HIGHEST = lax.Precision.HIGHEST
ANY = pl.BlockSpec(memory_space=pl.ANY)


def _params(sem, vmem=VMEM_LIMIT):
    return pltpu.CompilerParams(dimension_semantics=sem, vmem_limit_bytes=vmem)


def _pick_tile(n, target, mult):
    best = None
    for t in range(mult, min(n, target) + 1, mult):
        if n % t == 0:
            best = t
    assert best is not None, (n, target, mult)
    return best


def _sigmoid(x):
    return 1.0 / (1.0 + jnp.exp(-x))


def _log_sigmoid(x):
    return jnp.minimum(x, 0.0) - jnp.log(1.0 + jnp.exp(-jnp.abs(x)))


def _softplus(x):
    return jnp.maximum(x, 0.0) + jnp.log(1.0 + jnp.exp(-jnp.abs(x)))


def _dot(a, b):
    return jnp.dot(a, b, preferred_element_type=F32)


def _dot_nt(a, b):
    return lax.dot_general(a, b, (((1,), (1,)), ((), ())), preferred_element_type=F32)


def _dot_tn(a, b):
    return lax.dot_general(a, b, (((0,), (0,)), ((), ())), preferred_element_type=F32)


def _bmm(a, b):
    return lax.dot_general(a, b, (((2,), (1,)), ((0,), (0,))), preferred_element_type=F32)


def _bmm_nt(a, b):
    return lax.dot_general(a, b, (((2,), (2,)), ((0,), (0,))), preferred_element_type=F32)


def _rmsnorm_cast_kernel(x_ref, g_ref, o_ref):
    x = x_ref[...]
    ms = jnp.mean(x * x, axis=-1, keepdims=True)
    o_ref[...] = (x * lax.rsqrt(ms + NORM_EPS) * g_ref[...]).astype(o_ref.dtype)


def rmsnorm_cast(x, g):
    m, d = x.shape
    tm = _pick_tile(m, 256, 16)
    return pl.pallas_call(
        _rmsnorm_cast_kernel,
        grid=(m // tm,),
        in_specs=[pl.BlockSpec((tm, d), lambda i: (i, 0)), pl.BlockSpec((1, d), lambda i: (0, 0))],
        out_specs=pl.BlockSpec((tm, d), lambda i: (i, 0)),
        out_shape=jax.ShapeDtypeStruct((m, d), BF16),
        compiler_params=_params(("parallel",)),
        name="rmsnorm_cast",
    )(x, g.reshape(1, d))


def _postnorm_residual_kernel(y_ref, x_ref, g_ref, o_ref):
    y = y_ref[...]
    ms = jnp.mean(y * y, axis=-1, keepdims=True)
    o_ref[...] = x_ref[...] + y * lax.rsqrt(ms + NORM_EPS) * g_ref[...]


def postnorm_residual(y, x, g):
    m, d = x.shape
    tm = _pick_tile(m, 256, 8)
    row = pl.BlockSpec((tm, d), lambda i: (i, 0))
    return pl.pallas_call(
        _postnorm_residual_kernel,
        grid=(m // tm,),
        in_specs=[row, row, pl.BlockSpec((1, d), lambda i: (0, 0))],
        out_specs=row,
        out_shape=jax.ShapeDtypeStruct((m, d), F32),
        compiler_params=_params(("parallel",)),
        name="postnorm_residual",
    )(y, x, g.reshape(1, d))


def _matmul_kernel(a_ref, w_ref, o_ref):
    o_ref[...] = _dot(a_ref[...], w_ref[...].astype(BF16)).astype(o_ref.dtype)


def matmul(a, w, n_cols, layer=None, tn_target=512, tm_target=1408):
    m, k = a.shape
    tm = _pick_tile(m, tm_target, 16)
    tn = n_cols if n_cols < LANES else _pick_tile(n_cols, tn_target, LANES)
    if layer is None:
        w_spec = pl.BlockSpec((k, tn), lambda i, j: (0, j))
    else:
        w_spec = pl.BlockSpec((None, k, tn), lambda i, j: (layer, 0, j))
    return pl.pallas_call(
        _matmul_kernel,
        grid=(m // tm, n_cols // tn),
        in_specs=[pl.BlockSpec((tm, k), lambda i, j: (i, 0), pipeline_mode=pl.Buffered(1)), w_spec],
        out_specs=pl.BlockSpec((tm, tn), lambda i, j: (i, j)),
        out_shape=jax.ShapeDtypeStruct((m, n_cols), F32),
        compiler_params=_params(("parallel", "arbitrary")),
        name="matmul",
    )(a, w)


def _logf_cumsum_kernel(f_ref, b_ref, init_ref, lf_ref, cum_ref, carry, *, apply_gate):
    @pl.when(pl.program_id(1) == 0)
    def _():
        carry[...] = init_ref[0]

    x = f_ref[...]
    if apply_gate:
        x = _log_sigmoid(x + b_ref[...])
    tb = x.shape[0]
    r = lax.broadcasted_iota(jnp.int32, (tb, tb), 0)
    c = lax.broadcasted_iota(jnp.int32, (tb, tb), 1)
    tri = (c <= r).astype(F32)
    cum = jnp.dot(tri, x, precision=HIGHEST, preferred_element_type=F32) + carry[...]
    lf_ref[...] = x
    cum_ref[...] = cum
    carry[...] = cum[tb - 1:tb, :]


def logf_cumsum(f, row_off, nseq, t, bias, init, apply_gate):
    h = f.shape[1]
    tb = _pick_tile(t, 512, 8)
    nt = t // tb
    off = row_off // tb
    assert row_off % tb == 0
    blk = pl.BlockSpec((tb, h), lambda s, i: (s * nt + i, 0))
    return pl.pallas_call(
        functools.partial(_logf_cumsum_kernel, apply_gate=apply_gate),
        grid=(nseq, nt),
        in_specs=[pl.BlockSpec((tb, h), lambda s, i: (off + s * nt + i, 0)),
                  pl.BlockSpec((1, h), lambda s, i: (0, 0)),
                  pl.BlockSpec((1, 1, h), lambda s, i: (s, 0, 0))],
        out_specs=[blk, blk],
        out_shape=[jax.ShapeDtypeStruct((nseq * t, h), F32)] * 2,
        scratch_shapes=[pltpu.VMEM((1, h), F32)],
        compiler_params=_params(("parallel", "arbitrary")),
        name="logf_cumsum",
    )(f, bias.reshape(1, h), init)


def _fox_select_matrices(h, k_side):
    sel = np.zeros((4, h, h * LANES), np.float32)
    part_lane0, ones_lane0 = (3, 0) if k_side else (0, 3)
    for hh in range(h):
        for p in range(3):
            sel[p, hh, hh * LANES + part_lane0 + p] = 1.0
            sel[3, hh, hh * LANES + ones_lane0 + p] = 1.0
    return jnp.asarray(sel.reshape(4 * h, h * LANES), BF16)


def _fox_build_kernel(x_ref, cum_ref, g_ref, sel_ref, *refs, normalize, mode, aliased, scale, negate, pitch,
                      work_step):
    v_ref = xn_ref = kout_ref = vout_ref = kscr = vscr = None
    if mode == "kv":
        v_ref = refs[0]
        refs = refs[3:] if aliased else refs[1:]
        xp_ref, kout_ref, vout_ref, kscr, vscr = refs

        @pl.when(pl.program_id(1) != work_step)
        def _():
            kout_ref[...] = jnp.zeros(kout_ref.shape, kout_ref.dtype)
            vout_ref[...] = jnp.zeros(vout_ref.shape, vout_ref.dtype)
    elif mode == "norm":
        xp_ref, xn_ref = refs
    else:
        xp_ref, = refs
    pl.when(pl.program_id(1) == work_step)(functools.partial(
        _fox_build_body, x_ref, cum_ref, g_ref, sel_ref, v_ref, xp_ref, xn_ref, kout_ref, vout_ref, kscr, vscr,
        normalize=normalize, mode=mode, scale=scale, negate=negate, pitch=pitch))


def _fox_build_body(x_ref, cum_ref, g_ref, sel_ref, v_ref, xp_ref, xn_ref, kout_ref, vout_ref, kscr, vscr, *,
                    normalize, mode, scale, negate, pitch):
    tm = x_ref.shape[0]
    c = cum_ref[...] * LOG2E
    if negate:
        c = -c
    c_hi = c.astype(BF16).astype(F32)
    r1 = c - c_hi
    c_mid = r1.astype(BF16).astype(F32)
    c_lo = (r1 - c_mid).astype(BF16).astype(F32)
    parts = jnp.concatenate([c_hi, c_mid, c_lo, jnp.ones_like(c_hi)], axis=1).astype(BF16)
    extra = _dot(parts, sel_ref[...])
    g = g_ref[...]
    for h in range(c.shape[1]):
        xs = x_ref[:, h * LANES:(h + 1) * LANES]
        if normalize:
            xs = xs * lax.rsqrt(jnp.mean(xs * xs, axis=-1, keepdims=True) + NORM_EPS) * g
        if mode == "norm":
            xn_ref[:, h * LANES:(h + 1) * LANES] = xs
        if mode == "kv":
            kscr[h * pitch:h * pitch + tm, :] = xs
            vscr[h * pitch:h * pitch + tm, :] = v_ref[:, h * LANES:(h + 1) * LANES]
        xp_ref[:, 2 * h * LANES:(2 * h + 1) * LANES] = (xs * scale).astype(BF16)
        xp_ref[:, (2 * h + 1) * LANES:(2 * h + 2) * LANES] = extra[:, h * LANES:(h + 1) * LANES].astype(BF16)
    if mode == "kv":
        def relay(t, _):
            for g8 in range(c.shape[1] // 8):
                rows8 = pl.ds(g8 * 8 * pitch + t, 8, stride=pitch)
                kout_ref[t, g8 * 8:(g8 + 1) * 8, :] = kscr[rows8, :]
                vout_ref[t, g8 * 8:(g8 + 1) * 8, :] = vscr[rows8, :]
            return 0
        lax.fori_loop(0, tm, relay, 0, unroll=8)


def fox_build(x, col_block, row_off, rows, cum, gain, k_side, normalize, scale, mode="plain", kv_bufs=None,
              n_layers=1, slot=0):
    h = cum.shape[1]
    w = h * LANES
    tm = _pick_tile(rows, 128, 16)
    assert row_off % tm == 0 and h % 8 == 0
    off = row_off // tm
    pitch = tm + 4
    if x.ndim == 3:
        x_spec = pl.BlockSpec((None, tm, w), lambda i, l: (slot, off + i, col_block))
    else:
        x_spec = pl.BlockSpec((tm, w), lambda i, l: (off + i, col_block))
    in_specs = [x_spec,
                pl.BlockSpec((tm, h), lambda i, l: (i, 0)),
                pl.BlockSpec((1, LANES), lambda i, l: (0, 0)),
                pl.BlockSpec((4 * h, w), lambda i, l: (0, 0))]
    args = [x, cum, gain.reshape(1, LANES), _fox_select_matrices(h, k_side)]
    out_specs = [pl.BlockSpec((tm, 2 * w), lambda i, l: (i, 0))]
    out_shape = [jax.ShapeDtypeStruct((rows, 2 * w), BF16)]
    aliases = {}
    scratch = []
    slot_steps, work_step = 1, 0
    if mode == "kv":
        in_specs.append(pl.BlockSpec((tm, w), lambda i, l: (off + i, col_block + 1)))
        args.append(x)
        if kv_bufs is not None:
            in_specs += [ANY, ANY]
            args += list(kv_bufs)
            aliases = {5: 1, 6: 2}
            kv_spec = pl.BlockSpec((None, tm, h, LANES), lambda i, l: (slot, i, 0, 0))
        else:
            slot_steps, work_step = n_layers, slot
            kv_spec = pl.BlockSpec((None, tm, h, LANES), lambda i, l: (l, i, 0, 0))
        out_specs += [kv_spec, kv_spec]
        out_shape += [jax.ShapeDtypeStruct((n_layers, rows, h, LANES), F32)] * 2
        scratch = [pltpu.VMEM((h * pitch, LANES), F32)] * 2
    elif mode == "norm":
        out_specs.append(pl.BlockSpec((tm, w), lambda i, l: (i, 0)))
        out_shape.append(jax.ShapeDtypeStruct((rows, w), F32))
    return pl.pallas_call(
        functools.partial(_fox_build_kernel, normalize=normalize, mode=mode, aliased=bool(aliases),
                          scale=scale, negate=k_side, pitch=pitch, work_step=work_step),
        grid=(rows // tm, slot_steps),
        in_specs=in_specs,
        out_specs=out_specs,
        out_shape=out_shape,
        scratch_shapes=scratch,
        input_output_aliases=aliases,
        compiler_params=_params(("parallel", "arbitrary")),
        name="fox_build",
    )(*args)


def _fox_attn_kernel(q_ref, k_ref, v_ref, z_ref, ybuf_ref, o_ref, vt_scr, s_scr, *, tq, tk):
    i = pl.program_id(1)

    @pl.when(i == 0)
    def _():
        vt_scr[...] = v_ref[...].astype(BF16).T

    row0 = i * tq
    n_full = (row0 + 1) // tk
    n_all = (row0 + tq + tk - 1) // tk
    q = q_ref[...]

    def step(j, carry, masked):
        m, l, acc = carry
        cur = pl.multiple_of(j * tk, tk)
        st = s_scr[...]
        nxt = pl.multiple_of(jnp.minimum(j + 1, n_all - 1) * tk, tk)
        st_next = _dot_nt(k_ref[pl.ds(nxt, tk), :], q)
        if masked:
            kidx = cur + lax.broadcasted_iota(jnp.int32, (tk, tq), 0)
            qidx = row0 + lax.broadcasted_iota(jnp.int32, (tk, tq), 1)
            st = jnp.where(kidx <= qidx, st, NEG)
        m_new = jnp.maximum(m, jnp.max(st, axis=0, keepdims=True))
        alpha = jnp.exp2(m - m_new)
        p = jnp.exp2(st - m_new)
        l = alpha * l + jnp.sum(p, axis=0, keepdims=True)
        acc = alpha * acc + _dot(vt_scr[:, pl.ds(cur, tk)], p.astype(BF16))
        s_scr[...] = st_next
        return m_new, l, acc

    s_scr[...] = _dot_nt(k_ref[0:tk, :], q)
    carry = (jnp.full((1, tq), NEG, F32), jnp.zeros((1, tq), F32), jnp.zeros((LANES, tq), F32))
    carry = lax.fori_loop(0, n_full, functools.partial(step, masked=False), carry)
    _, l, acc = lax.fori_loop(n_full, n_all, functools.partial(step, masked=True), carry)
    z = z_ref[...]
    o_ref[...] = ((acc / l).T * (z * _sigmoid(z))).astype(o_ref.dtype)


def fox_attention_prompt(qp, kp, proj, nseq, t, nheads, ybuf):
    tq = _pick_tile(t, FOX_TQ, LANES)
    nq = t // tq
    blk = lambda g, i: (g // nheads * nq + i, g % nheads)
    return pl.pallas_call(
        functools.partial(_fox_attn_kernel, tq=tq, tk=tq),
        grid=(nseq * nheads, nq),
        in_specs=[pl.BlockSpec((tq, 2 * LANES), blk),
                  pl.BlockSpec((t, 2 * LANES), lambda g, i: (g // nheads, g % nheads)),
                  pl.BlockSpec((t, LANES), lambda g, i: (g // nheads, 2 * nheads + g % nheads)),
                  pl.BlockSpec((tq, LANES), lambda g, i: (g // nheads * nq + i, 3 * nheads + g % nheads)),
                  ANY],
        out_specs=pl.BlockSpec((tq, LANES), blk),
        out_shape=jax.ShapeDtypeStruct(ybuf.shape, ybuf.dtype),
        scratch_shapes=[pltpu.VMEM((LANES, t), BF16), pltpu.VMEM((tq, tq), F32)],
        input_output_aliases={4: 0},
        compiler_params=_params(("parallel", "arbitrary")),
        name="fox_attention",
    )(qp, kp, proj, proj, ybuf)


def _fox_attn_sample_kernel(q_ref, kc_ref, kn_ref, vc_ref, vn_ref, z_ref, ybuf_ref, o_ref, *, nh):
    ts = q_ref.shape[0]
    r = lax.broadcasted_iota(jnp.int32, (ts, ts), 0)
    c = lax.broadcasted_iota(jnp.int32, (ts, ts), 1)
    for u in range(nh):
        q = q_ref[:, 2 * u * LANES:2 * (u + 1) * LANES]
        s_c = _dot_nt(q, kc_ref[:, 2 * u * LANES:2 * (u + 1) * LANES])
        s_n = jnp.where(c <= r, _dot_nt(q, kn_ref[:, 2 * u * LANES:2 * (u + 1) * LANES]), NEG)
        m = jnp.maximum(jnp.max(s_c, axis=-1, keepdims=True), jnp.max(s_n, axis=-1, keepdims=True))
        p_c = jnp.exp2(s_c - m)
        p_n = jnp.exp2(s_n - m)
        l = jnp.sum(p_c, axis=-1, keepdims=True) + jnp.sum(p_n, axis=-1, keepdims=True)
        acc = (_dot(p_c.astype(BF16), vc_ref[:, u * LANES:(u + 1) * LANES].astype(BF16))
               + _dot(p_n.astype(BF16), vn_ref[:, u * LANES:(u + 1) * LANES].astype(BF16)))
        z = z_ref[:, u * LANES:(u + 1) * LANES]
        o_ref[:, u * LANES:(u + 1) * LANES] = (acc / l * (z * _sigmoid(z))).astype(o_ref.dtype)


def fox_attention_sample(qp, kp_c, kp_n, cache_v, slot, proj, row_off, nseq, ts, p, nheads, ybuf):
    nh = _pick_tile(nheads, FOX_HEADS_PER_STEP, 1)
    ng = nheads // nh
    assert row_off % ts == 0
    off = row_off // ts
    return pl.pallas_call(
        functools.partial(_fox_attn_sample_kernel, nh=nh),
        grid=(nseq, ng),
        in_specs=[pl.BlockSpec((ts, 2 * nh * LANES), lambda s, g: (s, g)),
                  pl.BlockSpec((p, 2 * nh * LANES), lambda s, g: (s, g)),
                  pl.BlockSpec((ts, 2 * nh * LANES), lambda s, g: (s, g)),
                  pl.BlockSpec((None, p, nh * LANES), lambda s, g: (slot, s, g)),
                  pl.BlockSpec((ts, nh * LANES), lambda s, g: (off + s, 2 * ng + g)),
                  pl.BlockSpec((ts, nh * LANES), lambda s, g: (off + s, 3 * ng + g)),
                  ANY],
        out_specs=pl.BlockSpec((ts, nh * LANES), lambda s, g: (off + s, g)),
        out_shape=jax.ShapeDtypeStruct(ybuf.shape, ybuf.dtype),
        input_output_aliases={6: 0},
        compiler_params=_params(("parallel", "arbitrary")),
        name="fox_attention_sample",
    )(qp, kp_c, kp_n, cache_v, proj, proj, ybuf)


def fox_layer(h, mp_rows, b, t, bs, ts, cache_k, cache_v, cache_lf, w_in, slot, b_f, q_gain, k_gain, kv_bufs):
    nheads = b_f.shape[0]
    w = nheads * LANES
    n_layers, _, p = cache_k.shape[:3]
    qscale = float(LANES) ** -0.5 * LOG2E
    proj = matmul(h, w_in, 4 * w, layer=slot)
    fgate = matmul(h, w_in[slot, :, 4 * w:], nheads)
    ybuf = jnp.zeros((h.shape[0], w), BF16)
    lf_p, cum_p = logf_cumsum(fgate, 0, b, t, b_f, jnp.zeros((b, 1, nheads), F32), True)
    qp_p, = fox_build(proj, 0, 0, mp_rows, cum_p, q_gain, False, True, qscale)
    kp_p, kbuf, vbuf = fox_build(proj, 1, 0, mp_rows, cum_p, k_gain, True, True, 1.0, "kv", kv_bufs, n_layers, slot)
    ybuf = fox_attention_prompt(qp_p, kp_p, proj, b, t, nheads, ybuf)
    ms_rows = bs * ts
    _, cum_c = logf_cumsum(cache_lf.reshape(bs * p, nheads), 0, bs, p, b_f, jnp.zeros((bs, 1, nheads), F32), False)
    init_s = cum_c.reshape(bs, p, nheads)[:, p - 1:, :]
    lf_s, cum_s = logf_cumsum(fgate, mp_rows, bs, ts, b_f, init_s, True)
    qp_s, = fox_build(proj, 0, mp_rows, ms_rows, cum_s, q_gain, False, True, qscale)
    kp_s, kn_s = fox_build(proj, 1, mp_rows, ms_rows, cum_s, k_gain, True, True, 1.0, "norm")
    kp_c, = fox_build(cache_k.reshape(n_layers, bs * p, w), 0, 0, bs * p, cum_c, k_gain, True, False, 1.0, slot=slot)
    ybuf = fox_attention_sample(qp_s, kp_c, kp_s, cache_v.reshape(n_layers, bs * p, w), slot, proj, mp_rows, bs, ts, p,
                                nheads, ybuf)
    v_s = proj[mp_rows:, 2 * w:3 * w]
    hd = LANES
    outs = (lf_p.reshape(b, t, nheads), kn_s.reshape(bs, ts, nheads, hd), v_s.reshape(bs, ts, nheads, hd),
            lf_s.reshape(bs, ts, nheads))
    return ybuf, (kbuf, vbuf), outs


def _mlstm_kernel(q_ref, k_ref, v_ref, o_ref, z_ref, gate_ref, bi_ref, bf_ref, gain_ref,
                  c0_ref, n0_ref, m0_ref, ybuf_ref, y_ref, c_ref, n_ref, m_ref, *, nheads, dk, dv):
    @pl.when(pl.program_id(1) == 0)
    def _():
        c_ref[...] = c0_ref[...]
        n_ref[...] = n0_ref[...]
        m_ref[...] = m0_ref[...]

    ln = q_ref.shape[0]
    gates = gate_ref[...]
    log_i = gates[:, :nheads] + bi_ref[...]
    log_f = _log_sigmoid(gates[:, nheads:] + bf_ref[...])
    r = lax.broadcasted_iota(jnp.int32, (ln, ln), 0)
    c = lax.broadcasted_iota(jnp.int32, (ln, ln), 1)
    causal = c <= r
    eye = c == r
    b_all = jnp.dot(causal.astype(F32), log_f, precision=HIGHEST, preferred_element_type=F32)
    kscale = float(dk) ** -0.5
    for h in range(nheads):
        b_col = b_all[:, h:h + 1]
        li_col = log_i[:, h:h + 1]
        b_row = jnp.sum(jnp.where(eye, b_col, 0.0), axis=0, keepdims=True)
        li_row = jnp.sum(jnp.where(eye, li_col, 0.0), axis=0, keepdims=True)
        m_prev = m_ref[0, h:h + 1, 0:1]
        d = jnp.where(causal, b_col - b_row + li_row, -jnp.inf)
        inter = b_col + m_prev
        m_t = jnp.maximum(inter, jnp.max(d, axis=-1, keepdims=True))
        q = q_ref[:, h * dk:(h + 1) * dk]
        k = k_ref[:, h * dk:(h + 1) * dk] * kscale
        v = v_ref[:, h * dv:(h + 1) * dv]
        qb = q.astype(BF16)
        kb = k.astype(BF16)
        wmat = jnp.exp(d - m_t) * _dot_nt(qb, kb)
        inter_w = jnp.exp(inter - m_t)
        cst = c_ref[0, h]
        nst = n_ref[0, h:h + 1, :]
        num = _dot(wmat.astype(BF16), v.astype(BF16)) + inter_w * _dot_nt(qb, cst.astype(BF16))
        den = jnp.sum(wmat, axis=-1, keepdims=True) + inter_w * jnp.sum(q * nst, axis=-1, keepdims=True)
        hid = num / jnp.maximum(jnp.abs(den), jnp.exp(-m_t))
        b_last = b_col[ln - 1:ln, :]
        g_col = b_last - b_col + li_col
        m_new = jnp.maximum(b_last + m_prev, jnp.max(g_col, axis=0, keepdims=True))
        wk = jnp.exp(g_col - m_new)
        carry_w = jnp.exp(b_last + m_prev - m_new)
        c_ref[0, h] = carry_w * cst + _dot_tn((wk * v).astype(BF16), kb)
        n_ref[0, h:h + 1, :] = carry_w * nst + jnp.sum(wk * k, axis=0, keepdims=True)
        m_ref[0, h:h + 1, :] = jnp.broadcast_to(m_new, (1, LANES))
        hn = hid * lax.rsqrt(jnp.mean(hid * hid, axis=-1, keepdims=True) + NORM_EPS)
        hn = hn * gain_ref[:, h * dv:(h + 1) * dv]
        og = o_ref[:, h * dv:(h + 1) * dv]
        zg = z_ref[:, h * dv:(h + 1) * dv]
        y_ref[:, h * dv:(h + 1) * dv] = (hn * _sigmoid(og) * (zg * _sigmoid(zg))).astype(y_ref.dtype)


def mlstm_scan(proj, gates, row_off, nseq, t, ln, c0, n0, m0, b_i, b_f, gain, ybuf):
    nheads, dv, dk = c0.shape[1:]
    nc = t // ln
    assert row_off % ln == 0
    off = row_off // ln
    qk_w, v_w = nheads * dk, nheads * dv
    assert v_w == 2 * qk_w

    def rows(width, cb):
        return pl.BlockSpec((ln, width), lambda s, i: (off + s * nc + i, cb))

    m0b = jnp.broadcast_to(m0[:, :, None], (nseq, nheads, LANES))
    y, c, n, m = pl.pallas_call(
        functools.partial(_mlstm_kernel, nheads=nheads, dk=dk, dv=dv),
        grid=(nseq, nc),
        in_specs=[rows(qk_w, 0), rows(qk_w, 1), rows(v_w, 1), rows(v_w, 2), rows(v_w, 3),
                  rows(2 * nheads, 0),
                  pl.BlockSpec((1, nheads), lambda s, i: (0, 0)),
                  pl.BlockSpec((1, nheads), lambda s, i: (0, 0)),
                  pl.BlockSpec((1, v_w), lambda s, i: (0, 0)),
                  pl.BlockSpec((1, nheads, dv, dk), lambda s, i: (s, 0, 0, 0)),
                  pl.BlockSpec((1, nheads, dk), lambda s, i: (s, 0, 0)),
                  pl.BlockSpec((1, nheads, LANES), lambda s, i: (s, 0, 0)),
                  ANY],
        out_specs=[rows(v_w, 0),
                   pl.BlockSpec((1, nheads, dv, dk), lambda s, i: (s, 0, 0, 0)),
                   pl.BlockSpec((1, nheads, dk), lambda s, i: (s, 0, 0)),
                   pl.BlockSpec((1, nheads, LANES), lambda s, i: (s, 0, 0))],
        out_shape=[jax.ShapeDtypeStruct(ybuf.shape, ybuf.dtype),
                   jax.ShapeDtypeStruct((nseq, nheads, dv, dk), F32),
                   jax.ShapeDtypeStruct((nseq, nheads, dk), F32),
                   jax.ShapeDtypeStruct((nseq, nheads, LANES), F32)],
        input_output_aliases={12: 0},
        compiler_params=_params(("parallel", "arbitrary")),
        name="mlstm_scan",
    )(proj, proj, proj, proj, proj, gates, b_i.reshape(1, nheads), b_f.reshape(1, nheads),
      gain.reshape(1, v_w), c0, n0, m0b, ybuf)
    return y, c, n, m[:, :, 0]


def mlstm_layer(h, mp_rows, b, t, bs, ts, c0, n0, m0, w_in, slot, b_i, b_f, gain):
    nheads, dv, dk = c0.shape[1:]
    main = 2 * nheads * dk + 3 * nheads * dv
    proj = matmul(h, w_in, main, layer=slot)
    gates = matmul(h, w_in[slot, :, main:], 2 * nheads)
    ybuf = jnp.zeros((h.shape[0], nheads * dv), BF16)
    zc = jnp.zeros((b, nheads, dv, dk), F32)
    ybuf, c_p, n_p, m_p = mlstm_scan(proj, gates, 0, b, t, min(CHUNK, t), zc, zc[:, :, 0, :], zc[:, :, 0, 0],
                                     b_i, b_f, gain, ybuf)
    ybuf, c_s, n_s, m_s = mlstm_scan(proj, gates, mp_rows, bs, ts, ts, c0, n0, m0, b_i, b_f, gain, ybuf)
    return ybuf, (c_p, n_p, m_p, c_s, n_s, m_s)


def _gdn_conv_kernel(x_ref, halo_ref, init_ref, w_ref, sc_ref, o_ref, *, l2norm):
    tm = x_ref.shape[0]
    prev = jnp.where(pl.program_id(1) == 0, init_ref[0], halo_ref[...])
    xc = jnp.concatenate([prev, x_ref[...]], axis=0)
    w = w_ref[...]
    acc = w[3:4] * xc[8:8 + tm] + w[2:3] * xc[7:7 + tm] + w[1:2] * xc[6:6 + tm] + w[0:1] * xc[5:5 + tm]
    act = acc * _sigmoid(acc)
    if l2norm:
        for g in range(act.shape[1] // LANES):
            a = act[:, g * LANES:(g + 1) * LANES]
            a = a * lax.rsqrt(jnp.sum(a * a, axis=-1, keepdims=True) + NORM_EPS)
            o_ref[:, g * LANES:(g + 1) * LANES] = (a * sc_ref[:, g * LANES:(g + 1) * LANES]).astype(o_ref.dtype)
    else:
        o_ref[...] = act.astype(o_ref.dtype)


def gdn_conv(proj, row_off, nseq, t, ch_off, ch, init, conv_w, slot, scale, l2norm, out_dtype):
    tm = _pick_tile(t, 512, 8)
    tc = _pick_tile(ch, 1024, LANES)
    nt = t // tm
    assert row_off % tm == 0 and ch_off % tc == 0 and tm % 8 == 0
    off, coff, sub = row_off // tm, ch_off // tc, tm // 8
    return pl.pallas_call(
        functools.partial(_gdn_conv_kernel, l2norm=l2norm),
        grid=(nseq, nt, ch // tc),
        in_specs=[pl.BlockSpec((tm, tc), lambda s, i, j: (off + s * nt + i, coff + j)),
                  pl.BlockSpec((8, tc), lambda s, i, j: (jnp.maximum((off + s * nt + i) * sub - 1, 0), coff + j)),
                  pl.BlockSpec((1, 8, tc), lambda s, i, j: (s, 0, coff + j)),
                  pl.BlockSpec((None, conv_w.shape[1], tc), lambda s, i, j: (slot, 0, coff + j)),
                  pl.BlockSpec((1, tc), lambda s, i, j: (0, j))],
        out_specs=pl.BlockSpec((tm, tc), lambda s, i, j: (s * nt + i, j)),
        out_shape=jax.ShapeDtypeStruct((nseq * t, ch), out_dtype),
        compiler_params=_params(("parallel", "arbitrary", "arbitrary")),
        name="gdn_conv",
    )(proj, proj, init, conv_w, scale)


def _unit_lower_inverse(a, r, c, levels):
    x = jnp.where(r == c, 1.0, 0.0) - jnp.where(((r >> 1) == (c >> 1)) & (c < r), a, 0.0)
    for sh in range(1, levels):
        off = jnp.where(((r >> (sh + 1)) == (c >> (sh + 1))) & ((r >> sh) != (c >> sh)) & (c < r), a, 0.0)
        xb = x.astype(BF16)
        x = x - _bmm(_bmm(xb, off.astype(BF16)).astype(BF16), xb)
    return x


def _gdn_kernel(qk_ref, v_ref, z_ref, gate_ref, alog_ref, dt_ref, gain_ref, expand_ref, ws0_ref, ybuf_ref,
                y_ref, ws_ref, *, kheads, vheads):
    @pl.when(pl.program_id(1) == 0)
    def _():
        ws_ref[...] = ws0_ref[...]

    ln = v_ref.shape[0]
    hd = LANES
    pack = MXU_DIM // ln
    ngrp = vheads // pack
    rr = pack * ln
    lsh = ln.bit_length() - 1
    rep = vheads // kheads
    gates = gate_ref[...]
    g_all = -jnp.exp(alog_ref[...]) * _softplus(gates[:, :vheads] + dt_ref[...])
    beta_all = _sigmoid(gates[:, vheads:])
    rl = lax.broadcasted_iota(jnp.int32, (ln, ln), 0)
    cl = lax.broadcasted_iota(jnp.int32, (ln, ln), 1)
    gsum_all = jnp.dot((cl <= rl).astype(F32), g_all, precision=HIGHEST, preferred_element_type=F32)
    glast_row = gsum_all[ln - 1:ln, :]
    decay_row = jnp.exp(jnp.dot(glast_row, expand_ref[...], precision=HIGHEST, preferred_element_type=F32))

    def stack_cols(a):
        return jnp.stack([jnp.concatenate([a[:, g * pack + p:g * pack + p + 1] for p in range(pack)], axis=0)
                          for g in range(ngrp)], axis=0)

    def stack_heads(ref, head_of):
        return jnp.stack([jnp.concatenate([ref[:, head_of(g * pack + p) * hd:(head_of(g * pack + p) + 1) * hd]
                                           for p in range(pack)], axis=0) for g in range(ngrp)], axis=0)

    gcol = stack_cols(gsum_all)
    bt = stack_cols(beta_all)
    glast = stack_cols(jnp.broadcast_to(glast_row, (ln, vheads)))
    r = lax.broadcasted_iota(jnp.int32, (rr, rr), 0)
    c = lax.broadcasted_iota(jnp.int32, (rr, rr), 1)
    blk = (r >> lsh) == (c >> lsh)
    incl = blk & (c <= r)
    strict = blk & (c < r)
    grow = jnp.sum(jnp.where(r == c, gcol, 0.0), axis=1, keepdims=True)
    dincl = jnp.where(incl, jnp.exp(jnp.where(incl, gcol - grow, 0.0)), 0.0)
    qst = stack_heads(qk_ref, lambda h: h // rep)
    kst = stack_heads(qk_ref, lambda h: kheads + h // rep)
    vst = stack_heads(v_ref, lambda h: h)
    kk = _bmm_nt(kst, kst)
    qk = _bmm_nt(qst, kst)
    x = _unit_lower_inverse(jnp.where(strict, bt * dincl * kk, 0.0), r, c, lsh)
    eg = jnp.exp(gcol)
    wsb = ws_ref[0].astype(BF16)
    gw = pack * hd
    rowblk = lax.broadcasted_iota(jnp.int32, (rr, 1), 0) >> lsh

    def own_block(wide):
        out = jnp.where(rowblk == 0, wide[:, :hd], 0.0)
        for p in range(1, pack):
            out = out + jnp.where(rowblk == p, wide[:, p * hd:(p + 1) * hd], 0.0)
        return out

    ksd = jnp.stack([own_block(_dot(kst[g], wsb[:, g * gw:(g + 1) * gw])) for g in range(ngrp)], axis=0)
    qsd = jnp.stack([own_block(_dot(qst[g], wsb[:, g * gw:(g + 1) * gw])) for g in range(ngrp)], axis=0)
    rhs = bt * (vst - eg * ksd)
    u = _bmm(x.astype(BF16), rhs.astype(BF16))
    o = eg * qsd + _bmm((dincl * qk).astype(BF16), u.astype(BF16))
    uw = (jnp.exp(glast - gcol) * u).astype(BF16)
    own = ((lax.broadcasted_iota(jnp.int32, (rr, gw), 0) >> lsh)
           == (lax.broadcasted_iota(jnp.int32, (rr, gw), 1) >> (hd.bit_length() - 1)))
    gain = gain_ref[...]
    for g in range(ngrp):
        uwbd = jnp.where(own, jnp.concatenate([uw[g]] * pack, axis=1), jnp.zeros((), BF16))
        cols = slice(g * gw, (g + 1) * gw)
        ws_ref[0, :, cols] = decay_row[:, cols] * ws_ref[0, :, cols] + _dot_tn(kst[g], uwbd)
        for p in range(pack):
            h = g * pack + p
            oh = o[g, p * ln:(p + 1) * ln, :]
            on = oh * lax.rsqrt(jnp.mean(oh * oh, axis=-1, keepdims=True) + NORM_EPS) * gain
            zg = z_ref[:, h * hd:(h + 1) * hd]
            y_ref[:, h * hd:(h + 1) * hd] = (on * (zg * _sigmoid(zg))).astype(y_ref.dtype)


def gdn_scan(qk, v, proj, z_col_block, gates, row_off, nseq, t, ln, s0, a_log, dt_bias, gain, ybuf):
    vheads, dv, dk = s0.shape[1:]
    v_w = vheads * LANES
    kheads = qk.shape[1] // (2 * LANES)
    nc = t // ln
    assert row_off % ln == 0 and MXU_DIM % ln == 0 and vheads % (MXU_DIM // ln) == 0 and dv == LANES and dk == LANES
    off = row_off // ln
    ws0 = jnp.transpose(s0, (0, 3, 1, 2)).reshape(nseq, dk, v_w)
    expand = jnp.asarray(np.kron(np.eye(vheads, dtype=np.float32), np.ones((1, LANES), np.float32)))
    local = lambda width: pl.BlockSpec((ln, width), lambda s, i: (s * nc + i, 0))
    state = pl.BlockSpec((1, dk, v_w), lambda s, i: (s, 0, 0))
    vec = lambda width: pl.BlockSpec((1, width), lambda s, i: (0, 0))
    ybuf, ws = pl.pallas_call(
        functools.partial(_gdn_kernel, kheads=kheads, vheads=vheads),
        grid=(nseq, nc),
        in_specs=[local(qk.shape[1]), local(v_w),
                  pl.BlockSpec((ln, v_w), lambda s, i: (off + s * nc + i, z_col_block)),
                  pl.BlockSpec((ln, 2 * vheads), lambda s, i: (off + s * nc + i, 0)),
                  vec(vheads), vec(vheads), vec(LANES),
                  pl.BlockSpec((vheads, v_w), lambda s, i: (0, 0)), state, ANY],
        out_specs=[pl.BlockSpec((ln, v_w), lambda s, i: (off + s * nc + i, 0)), state],
        out_shape=[jax.ShapeDtypeStruct(ybuf.shape, ybuf.dtype),
                   jax.ShapeDtypeStruct((nseq, dk, v_w), F32)],
        input_output_aliases={9: 0},
        compiler_params=_params(("parallel", "arbitrary")),
        name="gdn_scan",
    )(qk, v, proj, gates, a_log.reshape(1, vheads), dt_bias.reshape(1, vheads), gain.reshape(1, LANES),
      expand, ws0, ybuf)
    return ybuf, jnp.transpose(ws.reshape(nseq, dk, vheads, dv), (0, 2, 3, 1))


def gdn_layer(h, mp_rows, b, t, bs, ts, s0, conv0, w_in, slot, conv_w, a_log, dt_bias, gain):
    vheads = s0.shape[1]
    v_w = vheads * LANES
    taps, ch = conv_w.shape[1:]
    qk_w = ch - v_w
    proj = matmul(h, w_in, ch + v_w, layer=slot)
    gates = matmul(h, w_in[slot, :, ch + v_w:], 2 * vheads)
    scale = jnp.concatenate([jnp.full((1, qk_w // 2), float(LANES) ** -0.5, F32), jnp.ones((1, qk_w // 2), F32)], axis=1)
    ones = jnp.ones((1, v_w), F32)
    ybuf = jnp.zeros((h.shape[0], v_w), BF16)
    outs = []
    for row_off, nseq, tt, ln, init_rows, st0 in (
            (0, b, t, min(CHUNK, t), jnp.zeros((b, taps - 1, ch), F32), jnp.zeros((b,) + s0.shape[1:], F32)),
            (mp_rows, bs, ts, ts, conv0, s0)):
        init = jnp.pad(init_rows, ((0, 0), (8 - (taps - 1), 0), (0, 0)))
        qk = gdn_conv(proj, row_off, nseq, tt, 0, qk_w, init, conv_w, slot, scale, True, BF16)
        v = gdn_conv(proj, row_off, nseq, tt, qk_w, v_w, init, conv_w, slot, ones, False, F32)
        ybuf, s_new = gdn_scan(qk, v, proj, ch // v_w, gates, row_off, nseq, tt, ln, st0, a_log[slot], dt_bias[slot],
                               gain[slot], ybuf)
        conv_state = jnp.stack([proj[row_off + (s + 1) * tt - (taps - 1):row_off + (s + 1) * tt, :ch]
                                for s in range(nseq)])
        outs += [s_new, conv_state]
    return ybuf, tuple(outs)


def kernel(x_prompt, x_sample, cache_fox_k, cache_fox_v, cache_fox_logf, state_mlstm_c, state_mlstm_n, state_mlstm_m, state_gdn_s, state_gdn_conv, pre_norm, post_norm, fox_w_in, fox_b_f, fox_q_norm, fox_k_norm, fox_w_out, mlstm_w_in, mlstm_b_i, mlstm_b_f, mlstm_h_norm, mlstm_w_out, gdn_w_in, gdn_conv_w, gdn_a_log, gdn_dt_bias, gdn_o_norm, gdn_w_out):
    b, t, d = x_prompt.shape
    bs, ts, _ = x_sample.shape
    mp_rows = b * t
    n_fox = fox_w_in.shape[0]
    x = jnp.concatenate([x_prompt.reshape(mp_rows, d), x_sample.reshape(bs * ts, d)], axis=0)
    kv_bufs = None
    fox, ml, gd = [], [], []
    for layer in range(pre_norm.shape[0]):
        kind, j = layer % 3, layer // 3
        h = rmsnorm_cast(x, pre_norm[layer])
        if kind == 0:
            y, kv_bufs, outs = fox_layer(h, mp_rows, b, t, bs, ts, cache_fox_k, cache_fox_v, cache_fox_logf[j],
                                         fox_w_in, j, fox_b_f[j], fox_q_norm[j], fox_k_norm[j], kv_bufs)
            fox.append(outs)
            w_out = fox_w_out
        elif kind == 1:
            y, outs = mlstm_layer(h, mp_rows, b, t, bs, ts, state_mlstm_c[j], state_mlstm_n[j], state_mlstm_m[j],
                                  mlstm_w_in, j, mlstm_b_i[j], mlstm_b_f[j], mlstm_h_norm[j])
            ml.append(outs)
            w_out = mlstm_w_out
        else:
            y, outs = gdn_layer(h, mp_rows, b, t, bs, ts, state_gdn_s[j], state_gdn_conv[j], gdn_w_in, j,
                                gdn_conv_w, gdn_a_log, gdn_dt_bias, gdn_o_norm)
            gd.append(outs)
            w_out = gdn_w_out
        x = postnorm_residual(matmul(y, w_out, d, layer=j), x, post_norm[layer])
    stack = lambda group, idx: jnp.stack([o[idx] for o in group])
    nh = fox_b_f.shape[1]
    kv_shape = (n_fox, b, t, nh, LANES)
    return ((x[:mp_rows].reshape(b, t, d), x[mp_rows:].reshape(bs, ts, d),
             kv_bufs[0].reshape(kv_shape), kv_bufs[1].reshape(kv_shape))
            + tuple(stack(fox, i) for i in range(4))
            + tuple(stack(ml, i) for i in range(6))
            + tuple(stack(gd, i) for i in range(4)))
```

```python
import functools
import math

import numpy as np
import jax
import jax.numpy as jnp
from jax import lax
from jax.experimental import pallas as pl
from jax.experimental.pallas import tpu as pltpu

F32 = jnp.float32
BF16 = jnp.bfloat16
NORM_EPS = 1e-6
CHUNK = 64
LANES = 128
MXU_DIM = 256
NEG = -1e30
VMEM_LIMIT = 56 * 1024 * 1024
FOX_TQ = 1024
FOX_TK = 512
FOX_HEADS_PER_STEP = 8
LOG2E = math.log2(math.e)
HIGHEST = lax.Precision.HIGHEST
ANY = pl.BlockSpec(memory_space=pl.ANY)


def _params(sem, vmem=VMEM_LIMIT):
    return pltpu.CompilerParams(dimension_semantics=sem, vmem_limit_bytes=vmem)


def _pick_tile(n, target, mult):
    best = None
    for t in range(mult, min(n, target) + 1, mult):
        if n % t == 0:
            best = t
    assert best is not None, (n, target, mult)
    return best


def _relayout_pitch(rows):
    assert rows % 8 == 0
    pitch = rows + 8
    return pitch if (pitch // 8) % 2 else pitch + 8


def _sigmoid(x):
    return 1.0 / (1.0 + jnp.exp(-x))


def _log_sigmoid(x):
    return jnp.minimum(x, 0.0) - jnp.log(1.0 + jnp.exp(-jnp.abs(x)))


def _softplus(x):
    return jnp.maximum(x, 0.0) + jnp.log(1.0 + jnp.exp(-jnp.abs(x)))


def _dot(a, b):
    return jnp.dot(a, b, preferred_element_type=F32)


def _dot_nt(a, b):
    return lax.dot_general(a, b, (((1,), (1,)), ((), ())), preferred_element_type=F32)


def _dot_tn(a, b):
    return lax.dot_general(a, b, (((0,), (0,)), ((), ())), preferred_element_type=F32)


def _bmm(a, b):
    return lax.dot_general(a, b, (((2,), (1,)), ((0,), (0,))), preferred_element_type=F32)


def _bmm_nt(a, b):
    return lax.dot_general(a, b, (((2,), (2,)), ((0,), (0,))), preferred_element_type=F32)


def _rmsnorm_cast_kernel(x_ref, g_ref, o_ref):
    x = x_ref[...]
    ms = jnp.mean(x * x, axis=-1, keepdims=True)
    o_ref[...] = (x * lax.rsqrt(ms + NORM_EPS) * g_ref[...]).astype(o_ref.dtype)


def rmsnorm_cast(x, g):
    m, d = x.shape
    tm = _pick_tile(m, 256, 16)
    return pl.pallas_call(
        _rmsnorm_cast_kernel,
        grid=(m // tm,),
        in_specs=[pl.BlockSpec((tm, d), lambda i: (i, 0)), pl.BlockSpec((1, d), lambda i: (0, 0))],
        out_specs=pl.BlockSpec((tm, d), lambda i: (i, 0)),
        out_shape=jax.ShapeDtypeStruct((m, d), BF16),
        compiler_params=_params(("parallel",)),
        name="rmsnorm_cast",
    )(x, g.reshape(1, d))


def _postnorm_residual_kernel(y_ref, x_ref, g_ref, *refs):
    y = y_ref[...]
    ms = jnp.mean(y * y, axis=-1, keepdims=True)
    x = x_ref[...] + y * lax.rsqrt(ms + NORM_EPS) * g_ref[...]
    if len(refs) == 1:
        refs[0][...] = x
    else:
        gn_ref, o_ref, h_ref = refs
        o_ref[...] = x
        h_ref[...] = (x * lax.rsqrt(jnp.mean(x * x, axis=-1, keepdims=True) + NORM_EPS) * gn_ref[...]).astype(h_ref.dtype)


def postnorm_residual(y, x, g, g_next=None):
    m, d = x.shape
    tm = _pick_tile(m, 256, 16)
    row = pl.BlockSpec((tm, d), lambda i: (i, 0))
    vec = pl.BlockSpec((1, d), lambda i: (0, 0))
    fused = g_next is not None
    return pl.pallas_call(
        _postnorm_residual_kernel,
        grid=(m // tm,),
        in_specs=[row, row, vec] + ([vec] if fused else []),
        out_specs=[row, row] if fused else row,
        out_shape=([jax.ShapeDtypeStruct((m, d), F32), jax.ShapeDtypeStruct((m, d), BF16)] if fused
                   else jax.ShapeDtypeStruct((m, d), F32)),
        compiler_params=_params(("parallel",)),
        name="postnorm_residual",
    )(y, x, g.reshape(1, d), *([g_next.reshape(1, d)] if fused else []))


def _matmul_kernel(a_ref, w_ref, o_ref):
    o_ref[...] = _dot(a_ref[...], w_ref[...].astype(BF16)).astype(o_ref.dtype)


def _matmul_nt_kernel(a_ref, wt_ref, o_ref):
    o_ref[...] = _dot_nt(a_ref[...], wt_ref[...].astype(BF16)).astype(o_ref.dtype)


def matmul(a, w, n_cols, layer=None, w_transposed=False, tn_target=512, tm_target=1408):
    m, k = a.shape
    tm = _pick_tile(m, tm_target, 16)
    tn = n_cols if n_cols < LANES else _pick_tile(n_cols, tn_target, LANES)
    blk = (tn, k) if w_transposed else (k, tn)
    pick = (lambda j: (j, 0)) if w_transposed else (lambda j: (0, j))
    if layer is None:
        w_spec = pl.BlockSpec(blk, lambda i, j: pick(j))
    else:
        w_spec = pl.BlockSpec((None,) + blk, lambda i, j: (layer,) + pick(j))
    return pl.pallas_call(
        _matmul_nt_kernel if w_transposed else _matmul_kernel,
        grid=(m // tm, n_cols // tn),
        in_specs=[pl.BlockSpec((tm, k), lambda i, j: (i, 0), pipeline_mode=pl.Buffered(1)), w_spec],
        out_specs=pl.BlockSpec((tm, tn), lambda i, j: (i, j)),
        out_shape=jax.ShapeDtypeStruct((m, n_cols), F32),
        compiler_params=_params(("parallel", "arbitrary")),
        name="matmul",
    )(a, w)


def _logf_cumsum_kernel(f_ref, b_ref, init_ref, lf_ref, cum_ref, carry, *, apply_gate):
    @pl.when(pl.program_id(1) == 0)
    def _():
        carry[...] = init_ref[0]

    x = f_ref[...]
    if apply_gate:
        x = _log_sigmoid(x + b_ref[...])
    tb = x.shape[0]
    r = lax.broadcasted_iota(jnp.int32, (tb, tb), 0)
    c = lax.broadcasted_iota(jnp.int32, (tb, tb), 1)
    tri = (c <= r).astype(F32)
    cum = jnp.dot(tri, x, precision=HIGHEST, preferred_element_type=F32) + carry[...]
    lf_ref[...] = x
    cum_ref[...] = cum
    carry[...] = cum[tb - 1:tb, :]


def logf_cumsum(f, row_off, nseq, t, bias, init, apply_gate):
    h = f.shape[1]
    tb = _pick_tile(t, 512, 8)
    nt = t // tb
    off = row_off // tb
    assert row_off % tb == 0
    blk = pl.BlockSpec((tb, h), lambda s, i: (s * nt + i, 0))
    return pl.pallas_call(
        functools.partial(_logf_cumsum_kernel, apply_gate=apply_gate),
        grid=(nseq, nt),
        in_specs=[pl.BlockSpec((tb, h), lambda s, i: (off + s * nt + i, 0)),
                  pl.BlockSpec((1, h), lambda s, i: (0, 0)),
                  pl.BlockSpec((1, 1, h), lambda s, i: (s, 0, 0))],
        out_specs=[blk, blk],
        out_shape=[jax.ShapeDtypeStruct((nseq * t, h), F32)] * 2,
        scratch_shapes=[pltpu.VMEM((1, h), F32)],
        compiler_params=_params(("parallel", "arbitrary")),
        name="logf_cumsum",
    )(f, bias.reshape(1, h), init)


def _fox_select_matrices(h, k_side):
    sel = np.zeros((4, h, h * LANES), np.float32)
    part_lane0, ones_lane0 = (3, 0) if k_side else (0, 3)
    for hh in range(h):
        for p in range(3):
            sel[p, hh, hh * LANES + part_lane0 + p] = 1.0
            sel[3, hh, hh * LANES + ones_lane0 + p] = 1.0
    return jnp.asarray(sel.reshape(4 * h, h * LANES), BF16)


def _fox_build_kernel(x_ref, cum_ref, g_ref, sel_ref, *refs, normalize, mode, aliased, scale, negate, pitch,
                      work_step):
    v_ref = xn_ref = kout_ref = vout_ref = kscr = vscr = xscr = None
    if len(x_ref.shape) == 3:
        refs, xscr = refs[:-1], refs[-1]
    if mode == "kv":
        v_ref = refs[0]
        refs = refs[3:] if aliased else refs[1:]
        xp_ref, kout_ref, vout_ref, kscr, vscr = refs

        @pl.when(pl.program_id(1) != work_step)
        def _():
            kout_ref[...] = jnp.zeros(kout_ref.shape, kout_ref.dtype)
            vout_ref[...] = jnp.zeros(vout_ref.shape, vout_ref.dtype)
    elif mode == "norm":
        xp_ref, xn_ref = refs
    else:
        xp_ref, = refs
    pl.when(pl.program_id(1) == work_step)(functools.partial(
        _fox_build_body, x_ref, cum_ref, g_ref, sel_ref, v_ref, xp_ref, xn_ref, kout_ref, vout_ref, kscr, vscr, xscr,
        normalize=normalize, mode=mode, scale=scale, negate=negate, pitch=pitch))


def _fox_build_body(x_ref, cum_ref, g_ref, sel_ref, v_ref, xp_ref, xn_ref, kout_ref, vout_ref, kscr, vscr, xscr, *,
                    normalize, mode, scale, negate, pitch):
    tm = x_ref.shape[0]
    if xscr is not None:
        def unpack(t, _):
            for g8 in range(x_ref.shape[1] // 8):
                xscr[pl.ds(g8 * 8 * pitch + t, 8, stride=pitch), :] = x_ref[t, g8 * 8:(g8 + 1) * 8, :]
            return 0
        lax.fori_loop(0, tm, unpack, 0, unroll=8)
    c = cum_ref[...] * LOG2E
    if negate:
        c = -c
    c_hi = c.astype(BF16).astype(F32)
    r1 = c - c_hi
    c_mid = r1.astype(BF16).astype(F32)
    c_lo = (r1 - c_mid).astype(BF16).astype(F32)
    parts = jnp.concatenate([c_hi, c_mid, c_lo, jnp.ones_like(c_hi)], axis=1).astype(BF16)
    extra = _dot(parts, sel_ref[...])
    g = g_ref[...]
    for h in range(c.shape[1]):
        xs = x_ref[:, h * LANES:(h + 1) * LANES] if xscr is None else xscr[h * pitch:h * pitch + tm, :]
        if normalize:
            xs = xs * lax.rsqrt(jnp.mean(xs * xs, axis=-1, keepdims=True) + NORM_EPS) * g
        if mode == "norm":
            xn_ref[:, h * LANES:(h + 1) * LANES] = xs
        if mode == "kv":
            kscr[h * pitch:h * pitch + tm, :] = xs
            vscr[h * pitch:h * pitch + tm, :] = v_ref[:, h * LANES:(h + 1) * LANES]
        xp_ref[:, 2 * h * LANES:(2 * h + 1) * LANES] = (xs * scale).astype(BF16)
        xp_ref[:, (2 * h + 1) * LANES:(2 * h + 2) * LANES] = extra[:, h * LANES:(h + 1) * LANES].astype(BF16)
    if mode == "kv":
        def relay(t, _):
            for g8 in range(c.shape[1] // 8):
                rows8 = pl.ds(g8 * 8 * pitch + t, 8, stride=pitch)
                kout_ref[t, g8 * 8:(g8 + 1) * 8, :] = kscr[rows8, :]
                vout_ref[t, g8 * 8:(g8 + 1) * 8, :] = vscr[rows8, :]
            return 0
        lax.fori_loop(0, tm, relay, 0, unroll=8)


def fox_build(x, col_block, row_off, rows, cum, gain, k_side, normalize, scale, mode="plain", kv_bufs=None,
              n_layers=1, slot=0):
    h = cum.shape[1]
    w = h * LANES
    tm = _pick_tile(rows, 128, 16)
    assert row_off % tm == 0 and h % 8 == 0
    off = row_off // tm
    pitch = _relayout_pitch(tm)
    if x.ndim == 4:
        x_spec = pl.BlockSpec((None, tm, h, LANES), lambda i, l: (slot, off + i, 0, 0))
    else:
        x_spec = pl.BlockSpec((tm, w), lambda i, l: (off + i, col_block))
    in_specs = [x_spec,
                pl.BlockSpec((tm, h), lambda i, l: (i, 0)),
                pl.BlockSpec((1, LANES), lambda i, l: (0, 0)),
                pl.BlockSpec((4 * h, w), lambda i, l: (0, 0))]
    args = [x, cum, gain.reshape(1, LANES), _fox_select_matrices(h, k_side)]
    out_specs = [pl.BlockSpec((tm, 2 * w), lambda i, l: (i, 0))]
    out_shape = [jax.ShapeDtypeStruct((rows, 2 * w), BF16)]
    aliases = {}
    scratch = []
    slot_steps, work_step = 1, 0
    if mode == "kv":
        in_specs.append(pl.BlockSpec((tm, w), lambda i, l: (off + i, col_block + 1)))
        args.append(x)
        if kv_bufs is not None:
            in_specs += [ANY, ANY]
            args += list(kv_bufs)
            aliases = {5: 1, 6: 2}
            kv_spec = pl.BlockSpec((None, tm, h, LANES), lambda i, l: (slot, i, 0, 0))
        else:
            slot_steps, work_step = n_layers, slot
            kv_spec = pl.BlockSpec((None, tm, h, LANES), lambda i, l: (l, i, 0, 0))
        out_specs += [kv_spec, kv_spec]
        out_shape += [jax.ShapeDtypeStruct((n_layers, rows, h, LANES), F32)] * 2
        scratch = [pltpu.VMEM((h * pitch, LANES), F32)] * 2
    elif mode == "norm":
        out_specs.append(pl.BlockSpec((tm, w), lambda i, l: (i, 0)))
        out_shape.append(jax.ShapeDtypeStruct((rows, w), F32))
    if x.ndim == 4:
        scratch = scratch + [pltpu.VMEM((h * pitch, LANES), F32)]
    return pl.pallas_call(
        functools.partial(_fox_build_kernel, normalize=normalize, mode=mode, aliased=bool(aliases),
                          scale=scale, negate=k_side, pitch=pitch, work_step=work_step),
        grid=(rows // tm, slot_steps),
        in_specs=in_specs,
        out_specs=out_specs,
        out_shape=out_shape,
        scratch_shapes=scratch,
        input_output_aliases=aliases,
        compiler_params=_params(("parallel", "arbitrary")),
        name="fox_build",
    )(*args)


def _fox_attn_kernel(q_ref, k_ref, v_ref, z_ref, ybuf_ref, o_ref, vt_scr, s_scr, *, tq, tk):
    i = pl.program_id(1)

    @pl.when(i == 0)
    def _():
        vt_scr[...] = v_ref[...].astype(BF16).T

    row0 = i * tq
    n_full = (row0 + 1) // tk
    n_all = (row0 + tq + tk - 1) // tk
    q = q_ref[...]

    def step(j, carry, masked):
        m, l, acc = carry
        cur = pl.multiple_of(j * tk, tk)
        st = s_scr[...]
        nxt = pl.multiple_of(jnp.minimum(j + 1, n_all - 1) * tk, tk)
        st_next = _dot_nt(k_ref[pl.ds(nxt, tk), :], q)
        if masked:
            kidx = cur + lax.broadcasted_iota(jnp.int32, (tk, tq), 0)
            qidx = row0 + lax.broadcasted_iota(jnp.int32, (tk, tq), 1)
            st = jnp.where(kidx <= qidx, st, NEG)
        m_new = jnp.maximum(m, jnp.max(st, axis=0, keepdims=True))
        alpha = jnp.exp2(m - m_new)
        p = jnp.exp2(st - m_new)
        l = alpha * l + jnp.sum(p, axis=0, keepdims=True)
        acc = alpha * acc + _dot(vt_scr[:, pl.ds(cur, tk)], p.astype(BF16))
        s_scr[...] = st_next
        return m_new, l, acc

    s_scr[...] = _dot_nt(k_ref[0:tk, :], q)
    carry = (jnp.full((1, tq), NEG, F32), jnp.zeros((1, tq), F32), jnp.zeros((LANES, tq), F32))
    carry = lax.fori_loop(0, n_full, functools.partial(step, masked=False), carry)
    _, l, acc = lax.fori_loop(n_full, n_all, functools.partial(step, masked=True), carry)
    z = z_ref[...]
    o_ref[...] = ((acc / l).T * (z * _sigmoid(z))).astype(o_ref.dtype)


def fox_attention_prompt(qp, kp, proj, nseq, t, nheads, ybuf):
    tq = _pick_tile(t, FOX_TQ, LANES)
    tk = _pick_tile(tq, FOX_TK, LANES)
    nq = t // tq
    blk = lambda g, i: (g // nheads * nq + i, g % nheads)
    return pl.pallas_call(
        functools.partial(_fox_attn_kernel, tq=tq, tk=tk),
        grid=(nseq * nheads, nq),
        in_specs=[pl.BlockSpec((tq, 2 * LANES), blk),
                  pl.BlockSpec((t, 2 * LANES), lambda g, i: (g // nheads, g % nheads)),
                  pl.BlockSpec((t, LANES), lambda g, i: (g // nheads, 2 * nheads + g % nheads)),
                  pl.BlockSpec((tq, LANES), lambda g, i: (g // nheads * nq + i, 3 * nheads + g % nheads)),
                  ANY],
        out_specs=pl.BlockSpec((tq, LANES), blk),
        out_shape=jax.ShapeDtypeStruct(ybuf.shape, ybuf.dtype),
        scratch_shapes=[pltpu.VMEM((LANES, t), BF16), pltpu.VMEM((tk, tq), F32)],
        input_output_aliases={4: 0},
        compiler_params=_params(("parallel", "arbitrary")),
        name="fox_attention",
    )(qp, kp, proj, proj, ybuf)


def _fox_attn_sample_kernel(q_ref, kc_ref, kn_ref, vc_ref, vn_ref, z_ref, ybuf_ref, o_ref, vscr, *, nh, pitch):
    ts = q_ref.shape[0]
    p = vc_ref.shape[0]

    def unpack(t, _):
        vscr[pl.ds(t, nh, stride=pitch), :] = vc_ref[t]
        return 0
    lax.fori_loop(0, p, unpack, 0, unroll=8)
    r = lax.broadcasted_iota(jnp.int32, (ts, ts), 0)
    c = lax.broadcasted_iota(jnp.int32, (ts, ts), 1)
    for u in range(nh):
        q = q_ref[:, 2 * u * LANES:2 * (u + 1) * LANES]
        s_c = _dot_nt(q, kc_ref[:, 2 * u * LANES:2 * (u + 1) * LANES])
        s_n = jnp.where(c <= r, _dot_nt(q, kn_ref[:, 2 * u * LANES:2 * (u + 1) * LANES]), NEG)
        m = jnp.maximum(jnp.max(s_c, axis=-1, keepdims=True), jnp.max(s_n, axis=-1, keepdims=True))
        p_c = jnp.exp2(s_c - m)
        p_n = jnp.exp2(s_n - m)
        l = jnp.sum(p_c, axis=-1, keepdims=True) + jnp.sum(p_n, axis=-1, keepdims=True)
        acc = (_dot(p_c.astype(BF16), vscr[u * pitch:u * pitch + p, :].astype(BF16))
               + _dot(p_n.astype(BF16), vn_ref[:, u * LANES:(u + 1) * LANES].astype(BF16)))
        z = z_ref[:, u * LANES:(u + 1) * LANES]
        o_ref[:, u * LANES:(u + 1) * LANES] = (acc / l * (z * _sigmoid(z))).astype(o_ref.dtype)


def fox_attention_sample(qp, kp_c, kp_n, cache_v, slot, proj, row_off, nseq, ts, p, nheads, ybuf):
    nh = FOX_HEADS_PER_STEP
    ng = nheads // nh
    assert row_off % ts == 0 and nheads % nh == 0
    off = row_off // ts
    pitch = _relayout_pitch(p)
    return pl.pallas_call(
        functools.partial(_fox_attn_sample_kernel, nh=nh, pitch=pitch),
        grid=(nseq, ng),
        in_specs=[pl.BlockSpec((ts, 2 * nh * LANES), lambda s, g: (s, g)),
                  pl.BlockSpec((p, 2 * nh * LANES), lambda s, g: (s, g)),
                  pl.BlockSpec((ts, 2 * nh * LANES), lambda s, g: (s, g)),
                  pl.BlockSpec((None, p, nh, LANES), lambda s, g: (slot, s, g, 0)),
                  pl.BlockSpec((ts, nh * LANES), lambda s, g: (off + s, 2 * ng + g)),
                  pl.BlockSpec((ts, nh * LANES), lambda s, g: (off + s, 3 * ng + g)),
                  ANY],
        out_specs=pl.BlockSpec((ts, nh * LANES), lambda s, g: (off + s, g)),
        out_shape=jax.ShapeDtypeStruct(ybuf.shape, ybuf.dtype),
        scratch_shapes=[pltpu.VMEM((nh * pitch, LANES), F32)],
        input_output_aliases={6: 0},
        compiler_params=_params(("parallel", "arbitrary")),
        name="fox_attention_sample",
    )(qp, kp_c, kp_n, cache_v, proj, proj, ybuf)


def fox_layer(h, mp_rows, b, t, bs, ts, cache_k, cache_v, cache_lf, w_in, slot, b_f, q_gain, k_gain, kv_bufs):
    nheads = b_f.shape[0]
    w = nheads * LANES
    n_layers, _, p = cache_k.shape[:3]
    qscale = float(LANES) ** -0.5 * LOG2E
    wt = jnp.swapaxes(w_in, 1, 2)
    proj = matmul(h, wt, 4 * w, slot, True)
    fgate = matmul(h, wt[slot, 4 * w:, :], nheads, None, True)
    ybuf = jnp.zeros((h.shape[0], w), BF16)
    lf_p, cum_p = logf_cumsum(fgate, 0, b, t, b_f, jnp.zeros((b, 1, nheads), F32), True)
    qp_p, = fox_build(proj, 0, 0, mp_rows, cum_p, q_gain, False, True, qscale)
    kp_p, kbuf, vbuf = fox_build(proj, 1, 0, mp_rows, cum_p, k_gain, True, True, 1.0, "kv", kv_bufs, n_layers, slot)
    ybuf = fox_attention_prompt(qp_p, kp_p, proj, b, t, nheads, ybuf)
    ms_rows = bs * ts
    _, cum_c = logf_cumsum(cache_lf.reshape(bs * p, nheads), 0, bs, p, b_f, jnp.zeros((bs, 1, nheads), F32), False)
    init_s = cum_c.reshape(bs, p, nheads)[:, p - 1:, :]
    lf_s, cum_s = logf_cumsum(fgate, mp_rows, bs, ts, b_f, init_s, True)
    qp_s, = fox_build(proj, 0, mp_rows, ms_rows, cum_s, q_gain, False, True, qscale)
    kp_s, kn_s = fox_build(proj, 1, mp_rows, ms_rows, cum_s, k_gain, True, True, 1.0, "norm")
    by_head = (n_layers, bs * p, nheads, LANES)
    kp_c, = fox_build(cache_k.reshape(by_head), 0, 0, bs * p, cum_c, k_gain, True, False, 1.0, slot=slot)
    ybuf = fox_attention_sample(qp_s, kp_c, kp_s, cache_v.reshape(by_head), slot, proj, mp_rows, bs, ts, p,
                                nheads, ybuf)
    v_s = proj[mp_rows:, 2 * w:3 * w]
    hd = LANES
    outs = (lf_p.reshape(b, t, nheads), kn_s.reshape(bs, ts, nheads, hd), v_s.reshape(bs, ts, nheads, hd),
            lf_s.reshape(bs, ts, nheads))
    return ybuf, (kbuf, vbuf), outs


def _mlstm_kernel(q_ref, k_ref, v_ref, o_ref, z_ref, gate_ref, bi_ref, bf_ref, gain_ref,
                  c0_ref, n0_ref, m0_ref, ybuf_ref, y_ref, c_ref, n_ref, m_ref, *, nheads, dk, dv):
    @pl.when(pl.program_id(1) == 0)
    def _():
        c_ref[...] = c0_ref[...]
        n_ref[...] = n0_ref[...]
        m_ref[...] = m0_ref[...]

    ln = q_ref.shape[0]
    gates = gate_ref[...]
    log_i = gates[:, :nheads] + bi_ref[...]
    log_f = _log_sigmoid(gates[:, nheads:] + bf_ref[...])
    r = lax.broadcasted_iota(jnp.int32, (ln, ln), 0)
    c = lax.broadcasted_iota(jnp.int32, (ln, ln), 1)
    causal = c <= r
    eye = c == r
    b_all = jnp.dot(causal.astype(F32), log_f, precision=HIGHEST, preferred_element_type=F32)
    kscale = float(dk) ** -0.5
    for h in range(nheads):
        b_col = b_all[:, h:h + 1]
        li_col = log_i[:, h:h + 1]
        b_row = jnp.sum(jnp.where(eye, b_col, 0.0), axis=0, keepdims=True)
        li_row = jnp.sum(jnp.where(eye, li_col, 0.0), axis=0, keepdims=True)
        m_prev = m_ref[0, h:h + 1, 0:1]
        d = jnp.where(causal, b_col - b_row + li_row, -jnp.inf)
        inter = b_col + m_prev
        m_t = jnp.maximum(inter, jnp.max(d, axis=-1, keepdims=True))
        q = q_ref[:, h * dk:(h + 1) * dk]
        k = k_ref[:, h * dk:(h + 1) * dk] * kscale
        v = v_ref[:, h * dv:(h + 1) * dv]
        qb = q.astype(BF16)
        kb = k.astype(BF16)
        wmat = jnp.exp(d - m_t) * _dot_nt(qb, kb)
        inter_w = jnp.exp(inter - m_t)
        cst = c_ref[0, h]
        nst = n_ref[0, h:h + 1, :]
        num = _dot(wmat.astype(BF16), v.astype(BF16)) + inter_w * _dot_nt(qb, cst.astype(BF16))
        den = jnp.sum(wmat, axis=-1, keepdims=True) + inter_w * jnp.sum(q * nst, axis=-1, keepdims=True)
        hid = num / jnp.maximum(jnp.abs(den), jnp.exp(-m_t))
        b_last = b_col[ln - 1:ln, :]
        g_col = b_last - b_col + li_col
        m_new = jnp.maximum(b_last + m_prev, jnp.max(g_col, axis=0, keepdims=True))
        wk = jnp.exp(g_col - m_new)
        carry_w = jnp.exp(b_last + m_prev - m_new)
        c_ref[0, h] = carry_w * cst + _dot_tn((wk * v).astype(BF16), kb)
        n_ref[0, h:h + 1, :] = carry_w * nst + jnp.sum(wk * k, axis=0, keepdims=True)
        m_ref[0, h:h + 1, :] = jnp.broadcast_to(m_new, (1, LANES))
        hn = hid * lax.rsqrt(jnp.mean(hid * hid, axis=-1, keepdims=True) + NORM_EPS)
        hn = hn * gain_ref[:, h * dv:(h + 1) * dv]
        og = o_ref[:, h * dv:(h + 1) * dv]
        zg = z_ref[:, h * dv:(h + 1) * dv]
        y_ref[:, h * dv:(h + 1) * dv] = (hn * _sigmoid(og) * (zg * _sigmoid(zg))).astype(y_ref.dtype)


def mlstm_scan(proj, gates, row_off, nseq, t, ln, c0, n0, m0, b_i, b_f, gain, ybuf):
    nheads, dv, dk = c0.shape[1:]
    nc = t // ln
    assert row_off % ln == 0
    off = row_off // ln
    qk_w, v_w = nheads * dk, nheads * dv
    assert v_w == 2 * qk_w

    def rows(width, cb):
        return pl.BlockSpec((ln, width), lambda s, i: (off + s * nc + i, cb))

    m0b = jnp.broadcast_to(m0[:, :, None], (nseq, nheads, LANES))
    y, c, n, m = pl.pallas_call(
        functools.partial(_mlstm_kernel, nheads=nheads, dk=dk, dv=dv),
        grid=(nseq, nc),
        in_specs=[rows(qk_w, 0), rows(qk_w, 1), rows(v_w, 1), rows(v_w, 2), rows(v_w, 3),
                  rows(2 * nheads, 0),
                  pl.BlockSpec((1, nheads), lambda s, i: (0, 0)),
                  pl.BlockSpec((1, nheads), lambda s, i: (0, 0)),
                  pl.BlockSpec((1, v_w), lambda s, i: (0, 0)),
                  pl.BlockSpec((1, nheads, dv, dk), lambda s, i: (s, 0, 0, 0)),
                  pl.BlockSpec((1, nheads, dk), lambda s, i: (s, 0, 0)),
                  pl.BlockSpec((1, nheads, LANES), lambda s, i: (s, 0, 0)),
                  ANY],
        out_specs=[rows(v_w, 0),
                   pl.BlockSpec((1, nheads, dv, dk), lambda s, i: (s, 0, 0, 0)),
                   pl.BlockSpec((1, nheads, dk), lambda s, i: (s, 0, 0)),
                   pl.BlockSpec((1, nheads, LANES), lambda s, i: (s, 0, 0))],
        out_shape=[jax.ShapeDtypeStruct(ybuf.shape, ybuf.dtype),
                   jax.ShapeDtypeStruct((nseq, nheads, dv, dk), F32),
                   jax.ShapeDtypeStruct((nseq, nheads, dk), F32),
                   jax.ShapeDtypeStruct((nseq, nheads, LANES), F32)],
        input_output_aliases={12: 0},
        compiler_params=_params(("parallel", "arbitrary")),
        name="mlstm_scan",
    )(proj, proj, proj, proj, proj, gates, b_i.reshape(1, nheads), b_f.reshape(1, nheads),
      gain.reshape(1, v_w), c0, n0, m0b, ybuf)
    return y, c, n, m[:, :, 0]


def mlstm_layer(h, mp_rows, b, t, bs, ts, c0, n0, m0, w_in, slot, b_i, b_f, gain):
    nheads, dv, dk = c0.shape[1:]
    main = 2 * nheads * dk + 3 * nheads * dv
    wt = jnp.swapaxes(w_in, 1, 2)
    proj = matmul(h, wt, main, slot, True)
    gates = matmul(h, wt[slot, main:, :], 2 * nheads, None, True)
    ybuf = jnp.zeros((h.shape[0], nheads * dv), BF16)
    zc = jnp.zeros((b, nheads, dv, dk), F32)
    ybuf, c_p, n_p, m_p = mlstm_scan(proj, gates, 0, b, t, min(CHUNK, t), zc, zc[:, :, 0, :], zc[:, :, 0, 0],
                                     b_i, b_f, gain, ybuf)
    ybuf, c_s, n_s, m_s = mlstm_scan(proj, gates, mp_rows, bs, ts, ts, c0, n0, m0, b_i, b_f, gain, ybuf)
    return ybuf, (c_p, n_p, m_p, c_s, n_s, m_s)


def _gdn_conv_kernel(x_ref, halo_ref, init_ref, w_ref, sc_ref, o_ref, *, l2norm):
    tm = x_ref.shape[0]
    prev = jnp.where(pl.program_id(1) == 0, init_ref[0], halo_ref[...])
    xc = jnp.concatenate([prev, x_ref[...]], axis=0)
    w = w_ref[...]
    acc = w[3:4] * xc[8:8 + tm] + w[2:3] * xc[7:7 + tm] + w[1:2] * xc[6:6 + tm] + w[0:1] * xc[5:5 + tm]
    act = acc * _sigmoid(acc)
    if l2norm:
        for g in range(act.shape[1] // LANES):
            a = act[:, g * LANES:(g + 1) * LANES]
            a = a * lax.rsqrt(jnp.sum(a * a, axis=-1, keepdims=True) + NORM_EPS)
            o_ref[:, g * LANES:(g + 1) * LANES] = (a * sc_ref[:, g * LANES:(g + 1) * LANES]).astype(o_ref.dtype)
    else:
        o_ref[...] = act.astype(o_ref.dtype)


def gdn_conv(proj, row_off, nseq, t, ch_off, ch, init, conv_w, slot, scale, l2norm, out_dtype):
    tm = _pick_tile(t, 512, 8)
    tc = _pick_tile(ch, 1024, LANES)
    nt = t // tm
    assert row_off % tm == 0 and ch_off % tc == 0 and tm % 8 == 0
    off, coff, sub = row_off // tm, ch_off // tc, tm // 8
    return pl.pallas_call(
        functools.partial(_gdn_conv_kernel, l2norm=l2norm),
        grid=(nseq, nt, ch // tc),
        in_specs=[pl.BlockSpec((tm, tc), lambda s, i, j: (off + s * nt + i, coff + j)),
                  pl.BlockSpec((8, tc), lambda s, i, j: (jnp.maximum((off + s * nt + i) * sub - 1, 0), coff + j)),
                  pl.BlockSpec((1, 8, tc), lambda s, i, j: (s, 0, coff + j)),
                  pl.BlockSpec((None, conv_w.shape[1], tc), lambda s, i, j: (slot, 0, coff + j)),
                  pl.BlockSpec((1, tc), lambda s, i, j: (0, j))],
        out_specs=pl.BlockSpec((tm, tc), lambda s, i, j: (s * nt + i, j)),
        out_shape=jax.ShapeDtypeStruct((nseq * t, ch), out_dtype),
        compiler_params=_params(("parallel", "arbitrary", "arbitrary")),
        name="gdn_conv",
    )(proj, proj, init, conv_w, scale)


def _unit_lower_inverse(a, r, c, levels):
    x = jnp.where(r == c, 1.0, 0.0) - jnp.where(((r >> 1) == (c >> 1)) & (c < r), a, 0.0)
    for sh in range(1, levels):
        off = jnp.where(((r >> (sh + 1)) == (c >> (sh + 1))) & ((r >> sh) != (c >> sh)) & (c < r), a, 0.0)
        xb = x.astype(BF16)
        x = x - _bmm(_bmm(xb, off.astype(BF16)).astype(BF16), xb)
    return x


def _gdn_kernel(qk_ref, v_ref, z_ref, gate_ref, alog_ref, dt_ref, gain_ref, expand_ref, ws0_ref, ybuf_ref,
                y_ref, ws_ref, *, kheads, vheads):
    @pl.when(pl.program_id(1) == 0)
    def _():
        ws_ref[...] = ws0_ref[...]

    ln = v_ref.shape[0]
    hd = LANES
    pack = MXU_DIM // ln
    ngrp = vheads // pack
    rr = pack * ln
    lsh = ln.bit_length() - 1
    rep = vheads // kheads
    gates = gate_ref[...]
    g_all = -jnp.exp(alog_ref[...]) * _softplus(gates[:, :vheads] + dt_ref[...])
    beta_all = _sigmoid(gates[:, vheads:])
    rl = lax.broadcasted_iota(jnp.int32, (ln, ln), 0)
    cl = lax.broadcasted_iota(jnp.int32, (ln, ln), 1)
    gsum_all = jnp.dot((cl <= rl).astype(F32), g_all, precision=HIGHEST, preferred_element_type=F32)
    glast_row = gsum_all[ln - 1:ln, :]
    decay_row = jnp.exp(jnp.dot(glast_row, expand_ref[...], precision=HIGHEST, preferred_element_type=F32))

    def stack_cols(a):
        return jnp.stack([jnp.concatenate([a[:, g * pack + p:g * pack + p + 1] for p in range(pack)], axis=0)
                          for g in range(ngrp)], axis=0)

    def stack_heads(ref, head_of):
        return jnp.stack([jnp.concatenate([ref[:, head_of(g * pack + p) * hd:(head_of(g * pack + p) + 1) * hd]
                                           for p in range(pack)], axis=0) for g in range(ngrp)], axis=0)

    gcol = stack_cols(gsum_all)
    bt = stack_cols(beta_all)
    glast = stack_cols(jnp.broadcast_to(glast_row, (ln, vheads)))
    r = lax.broadcasted_iota(jnp.int32, (rr, rr), 0)
    c = lax.broadcasted_iota(jnp.int32, (rr, rr), 1)
    blk = (r >> lsh) == (c >> lsh)
    incl = blk & (c <= r)
    strict = blk & (c < r)
    grow = jnp.sum(jnp.where(r == c, gcol, 0.0), axis=1, keepdims=True)
    dincl = jnp.where(incl, jnp.exp(jnp.where(incl, gcol - grow, 0.0)), 0.0)
    qst = stack_heads(qk_ref, lambda h: h // rep)
    kst = stack_heads(qk_ref, lambda h: kheads + h // rep)
    vst = stack_heads(v_ref, lambda h: h)
    kk = _bmm_nt(kst, kst)
    qk = _bmm_nt(qst, kst)
    x = _unit_lower_inverse(jnp.where(strict, bt * dincl * kk, 0.0), r, c, lsh)
    eg = jnp.exp(gcol)
    wsb = ws_ref[0].astype(BF16)
    gw = pack * hd
    rowblk = lax.broadcasted_iota(jnp.int32, (rr, 1), 0) >> lsh

    def own_block(wide):
        out = jnp.where(rowblk == 0, wide[:, :hd], 0.0)
        for p in range(1, pack):
            out = out + jnp.where(rowblk == p, wide[:, p * hd:(p + 1) * hd], 0.0)
        return out

    ksd = jnp.stack([own_block(_dot(kst[g], wsb[:, g * gw:(g + 1) * gw])) for g in range(ngrp)], axis=0)
    qsd = jnp.stack([own_block(_dot(qst[g], wsb[:, g * gw:(g + 1) * gw])) for g in range(ngrp)], axis=0)
    rhs = bt * (vst - eg * ksd)
    u = _bmm(x.astype(BF16), rhs.astype(BF16))
    o = eg * qsd + _bmm((dincl * qk).astype(BF16), u.astype(BF16))
    uw = (jnp.exp(glast - gcol) * u).astype(BF16)
    own = ((lax.broadcasted_iota(jnp.int32, (rr, gw), 0) >> lsh)
           == (lax.broadcasted_iota(jnp.int32, (rr, gw), 1) >> (hd.bit_length() - 1)))
    gain = gain_ref[...]
    for g in range(ngrp):
        uwbd = jnp.where(own, jnp.concatenate([uw[g]] * pack, axis=1), jnp.zeros((), BF16))
        cols = slice(g * gw, (g + 1) * gw)
        ws_ref[0, :, cols] = decay_row[:, cols] * ws_ref[0, :, cols] + _dot_tn(kst[g], uwbd)
        for p in range(pack):
            h = g * pack + p
            oh = o[g, p * ln:(p + 1) * ln, :]
            on = oh * lax.rsqrt(jnp.mean(oh * oh, axis=-1, keepdims=True) + NORM_EPS) * gain
            zg = z_ref[:, h * hd:(h + 1) * hd]
            y_ref[:, h * hd:(h + 1) * hd] = (on * (zg * _sigmoid(zg))).astype(y_ref.dtype)


def gdn_scan(qk, v, proj, z_col_block, gates, row_off, nseq, t, ln, s0, a_log, dt_bias, gain, ybuf):
    vheads, dv, dk = s0.shape[1:]
    v_w = vheads * LANES
    kheads = qk.shape[1] // (2 * LANES)
    nc = t // ln
    assert row_off % ln == 0 and MXU_DIM % ln == 0 and vheads % (MXU_DIM // ln) == 0 and dv == LANES and dk == LANES
    off = row_off // ln
    ws0 = jnp.transpose(s0, (0, 3, 1, 2)).reshape(nseq, dk, v_w)
    expand = jnp.asarray(np.kron(np.eye(vheads, dtype=np.float32), np.ones((1, LANES), np.float32)))
    local = lambda width: pl.BlockSpec((ln, width), lambda s, i: (s * nc + i, 0))
    state = pl.BlockSpec((1, dk, v_w), lambda s, i: (s, 0, 0))
    vec = lambda width: pl.BlockSpec((1, width), lambda s, i: (0, 0))
    ybuf, ws = pl.pallas_call(
        functools.partial(_gdn_kernel, kheads=kheads, vheads=vheads),
        grid=(nseq, nc),
        in_specs=[local(qk.shape[1]), local(v_w),
                  pl.BlockSpec((ln, v_w), lambda s, i: (off + s * nc + i, z_col_block)),
                  pl.BlockSpec((ln, 2 * vheads), lambda s, i: (off + s * nc + i, 0)),
                  vec(vheads), vec(vheads), vec(LANES),
                  pl.BlockSpec((vheads, v_w), lambda s, i: (0, 0)), state, ANY],
        out_specs=[pl.BlockSpec((ln, v_w), lambda s, i: (off + s * nc + i, 0)), state],
        out_shape=[jax.ShapeDtypeStruct(ybuf.shape, ybuf.dtype),
                   jax.ShapeDtypeStruct((nseq, dk, v_w), F32)],
        input_output_aliases={9: 0},
        compiler_params=_params(("parallel", "arbitrary")),
        name="gdn_scan",
    )(qk, v, proj, gates, a_log.reshape(1, vheads), dt_bias.reshape(1, vheads), gain.reshape(1, LANES),
      expand, ws0, ybuf)
    return ybuf, jnp.transpose(ws.reshape(nseq, dk, vheads, dv), (0, 2, 3, 1))


def gdn_layer(h, mp_rows, b, t, bs, ts, s0, conv0, w_in, slot, conv_w, a_log, dt_bias, gain):
    vheads = s0.shape[1]
    v_w = vheads * LANES
    taps, ch = conv_w.shape[1:]
    qk_w = ch - v_w
    wt = jnp.swapaxes(w_in, 1, 2)
    proj = matmul(h, wt, ch + v_w, slot, True)
    gates = matmul(h, wt[slot, ch + v_w:, :], 2 * vheads, None, True)
    scale = jnp.concatenate([jnp.full((1, qk_w // 2), float(LANES) ** -0.5, F32), jnp.ones((1, qk_w // 2), F32)], axis=1)
    ones = jnp.ones((1, v_w), F32)
    ybuf = jnp.zeros((h.shape[0], v_w), BF16)
    outs = []
    for row_off, nseq, tt, ln, init_rows, st0 in (
            (0, b, t, min(CHUNK, t), jnp.zeros((b, taps - 1, ch), F32), jnp.zeros((b,) + s0.shape[1:], F32)),
            (mp_rows, bs, ts, ts, conv0, s0)):
        init = jnp.pad(init_rows, ((0, 0), (8 - (taps - 1), 0), (0, 0)))
        qk = gdn_conv(proj, row_off, nseq, tt, 0, qk_w, init, conv_w, slot, scale, True, BF16)
        v = gdn_conv(proj, row_off, nseq, tt, qk_w, v_w, init, conv_w, slot, ones, False, F32)
        ybuf, s_new = gdn_scan(qk, v, proj, ch // v_w, gates, row_off, nseq, tt, ln, st0, a_log[slot], dt_bias[slot],
                               gain[slot], ybuf)
        conv_state = jnp.stack([proj[row_off + (s + 1) * tt - (taps - 1):row_off + (s + 1) * tt, :ch]
                                for s in range(nseq)])
        outs += [s_new, conv_state]
    return ybuf, tuple(outs)


def kernel(x_prompt, x_sample, cache_fox_k, cache_fox_v, cache_fox_logf, state_mlstm_c, state_mlstm_n, state_mlstm_m, state_gdn_s, state_gdn_conv, pre_norm, post_norm, fox_w_in, fox_b_f, fox_q_norm, fox_k_norm, fox_w_out, mlstm_w_in, mlstm_b_i, mlstm_b_f, mlstm_h_norm, mlstm_w_out, gdn_w_in, gdn_conv_w, gdn_a_log, gdn_dt_bias, gdn_o_norm, gdn_w_out):
    b, t, d = x_prompt.shape
    bs, ts, _ = x_sample.shape
    mp_rows = b * t
    n_fox = fox_w_in.shape[0]
    x = jnp.concatenate([x_prompt.reshape(mp_rows, d), x_sample.reshape(bs * ts, d)], axis=0)
    kv_bufs = None
    fox, ml, gd = [], [], []
    depth = pre_norm.shape[0]
    h = rmsnorm_cast(x, pre_norm[0])
    for layer in range(depth):
        kind, j = layer % 3, layer // 3
        if kind == 0:
            y, kv_bufs, outs = fox_layer(h, mp_rows, b, t, bs, ts, cache_fox_k, cache_fox_v, cache_fox_logf[j],
                                         fox_w_in, j, fox_b_f[j], fox_q_norm[j], fox_k_norm[j], kv_bufs)
            fox.append(outs)
            w_out = fox_w_out
        elif kind == 1:
            y, outs = mlstm_layer(h, mp_rows, b, t, bs, ts, state_mlstm_c[j], state_mlstm_n[j], state_mlstm_m[j],
                                  mlstm_w_in, j, mlstm_b_i[j], mlstm_b_f[j], mlstm_h_norm[j])
            ml.append(outs)
            w_out = mlstm_w_out
        else:
            y, outs = gdn_layer(h, mp_rows, b, t, bs, ts, state_gdn_s[j], state_gdn_conv[j], gdn_w_in, j,
                                gdn_conv_w, gdn_a_log, gdn_dt_bias, gdn_o_norm)
            gd.append(outs)
            w_out = gdn_w_out
        mixed = matmul(y, w_out, d, layer=j)
        if layer + 1 < depth:
            x, h = postnorm_residual(mixed, x, post_norm[layer], pre_norm[layer + 1])
        else:
            x = postnorm_residual(mixed, x, post_norm[layer])
    stack = lambda group, idx: jnp.stack([o[idx] for o in group])
    nh = fox_b_f.shape[1]
    kv_shape = (n_fox, b, t, nh, LANES)
    return ((x[:mp_rows].reshape(b, t, d), x[mp_rows:].reshape(bs, ts, d),
             kv_bufs[0].reshape(kv_shape), kv_bufs[1].reshape(kv_shape))
            + tuple(stack(fox, i) for i in range(4))
            + tuple(stack(ml, i) for i in range(6))
            + tuple(stack(gd, i) for i in range(4)))
```

```python
import functools
import math

import numpy as np
import jax
import jax.numpy as jnp
from jax import lax
from jax.experimental import pallas as pl
from jax.experimental.pallas import tpu as pltpu

F32 = jnp.float32
BF16 = jnp.bfloat16
NORM_EPS = 1e-6
CHUNK = 64
LANES = 128
MXU_DIM = 256
NEG = -1e30
VMEM_LIMIT = 56 * 1024 * 1024
FOX_TQ = 512
FOX_UNROLL = 4
FOX_HEADS_PER_STEP = 8
LOG2E = math.log2(math.e)
HIGHEST = lax.Precision.HIGHEST
ANY = pl.BlockSpec(memory_space=pl.ANY)


def _params(sem, vmem=VMEM_LIMIT):
    return pltpu.CompilerParams(dimension_semantics=sem, vmem_limit_bytes=vmem)


def _pick_tile(n, target, mult):
    best = None
    for t in range(mult, min(n, target) + 1, mult):
        if n % t == 0:
            best = t
    assert best is not None, (n, target, mult)
    return best


def _relayout_pitch(rows):
    assert rows % 8 == 0
    pitch = rows + 8
    return pitch if (pitch // 8) % 2 else pitch + 8


def _sigmoid(x):
    return 0.5 * jnp.tanh(0.5 * x) + 0.5


def _log_sigmoid(x):
    return jnp.minimum(x, 0.0) - jnp.log(1.0 + jnp.exp(-jnp.abs(x)))


def _softplus(x):
    return jnp.maximum(x, 0.0) + jnp.log(1.0 + jnp.exp(-jnp.abs(x)))


def _dot(a, b):
    return jnp.dot(a, b, preferred_element_type=F32)


def _dot_nt(a, b):
    return lax.dot_general(a, b, (((1,), (1,)), ((), ())), preferred_element_type=F32)


def _dot_tn(a, b):
    return lax.dot_general(a, b, (((0,), (0,)), ((), ())), preferred_element_type=F32)


def _bmm(a, b):
    return lax.dot_general(a, b, (((2,), (1,)), ((0,), (0,))), preferred_element_type=F32)


def _bmm_nt(a, b):
    return lax.dot_general(a, b, (((2,), (2,)), ((0,), (0,))), preferred_element_type=F32)


def _rmsnorm_cast_kernel(x_ref, g_ref, o_ref):
    x = x_ref[...]
    ms = jnp.mean(x * x, axis=-1, keepdims=True)
    o_ref[...] = (x * lax.rsqrt(ms + NORM_EPS) * g_ref[...]).astype(o_ref.dtype)


def rmsnorm_cast(x, g):
    m, d = x.shape
    tm = _pick_tile(m, 256, 16)
    return pl.pallas_call(
        _rmsnorm_cast_kernel,
        grid=(m // tm,),
        in_specs=[pl.BlockSpec((tm, d), lambda i: (i, 0)), pl.BlockSpec((1, d), lambda i: (0, 0))],
        out_specs=pl.BlockSpec((tm, d), lambda i: (i, 0)),
        out_shape=jax.ShapeDtypeStruct((m, d), BF16),
        compiler_params=_params(("parallel",)),
        name="rmsnorm_cast",
    )(x, g.reshape(1, d))


def _postnorm_residual_kernel(y_ref, x_ref, g_ref, *refs):
    y = y_ref[...]
    ms = jnp.mean(y * y, axis=-1, keepdims=True)
    x = x_ref[...] + y * lax.rsqrt(ms + NORM_EPS) * g_ref[...]
    if len(refs) == 1:
        refs[0][...] = x
    else:
        gn_ref, o_ref, h_ref = refs
        o_ref[...] = x
        h_ref[...] = (x * lax.rsqrt(jnp.mean(x * x, axis=-1, keepdims=True) + NORM_EPS) * gn_ref[...]).astype(h_ref.dtype)


def postnorm_residual(y, x, g, g_next=None):
    m, d = x.shape
    tm = _pick_tile(m, 256, 16)
    row = pl.BlockSpec((tm, d), lambda i: (i, 0))
    vec = pl.BlockSpec((1, d), lambda i: (0, 0))
    fused = g_next is not None
    return pl.pallas_call(
        _postnorm_residual_kernel,
        grid=(m // tm,),
        in_specs=[row, row, vec] + ([vec] if fused else []),
        out_specs=[row, row] if fused else row,
        out_shape=([jax.ShapeDtypeStruct((m, d), F32), jax.ShapeDtypeStruct((m, d), BF16)] if fused
                   else jax.ShapeDtypeStruct((m, d), F32)),
        compiler_params=_params(("parallel",)),
        name="postnorm_residual",
    )(y, x, g.reshape(1, d), *([g_next.reshape(1, d)] if fused else []))


def _matmul_kernel(a_ref, w_ref, o_ref):
    o_ref[...] = _dot(a_ref[...], w_ref[...].astype(BF16)).astype(o_ref.dtype)


def _matmul_nt_kernel(a_ref, wt_ref, o_ref):
    o_ref[...] = _dot_nt(a_ref[...], wt_ref[...].astype(BF16)).astype(o_ref.dtype)


def matmul(a, w, n_cols, layer=None, w_transposed=False, tn_target=512, tm_target=1408):
    m, k = a.shape
    tm = _pick_tile(m, tm_target, 16)
    tn = n_cols if n_cols < LANES else _pick_tile(n_cols, tn_target, LANES)
    blk = (tn, k) if w_transposed else (k, tn)
    pick = (lambda j: (j, 0)) if w_transposed else (lambda j: (0, j))
    if layer is None:
        w_spec = pl.BlockSpec(blk, lambda i, j: pick(j))
    else:
        w_spec = pl.BlockSpec((None,) + blk, lambda i, j: (layer,) + pick(j))
    return pl.pallas_call(
        _matmul_nt_kernel if w_transposed else _matmul_kernel,
        grid=(m // tm, n_cols // tn),
        in_specs=[pl.BlockSpec((tm, k), lambda i, j: (i, 0), pipeline_mode=pl.Buffered(1)), w_spec],
        out_specs=pl.BlockSpec((tm, tn), lambda i, j: (i, j)),
        out_shape=jax.ShapeDtypeStruct((m, n_cols), F32),
        compiler_params=_params(("parallel", "arbitrary")),
        name="matmul",
    )(a, w)


def _logf_cumsum_kernel(f_ref, b_ref, init_ref, lf_ref, cum_ref, carry, *, apply_gate):
    @pl.when(pl.program_id(1) == 0)
    def _():
        carry[...] = init_ref[0]

    x = f_ref[...]
    if apply_gate:
        x = _log_sigmoid(x + b_ref[...])
    tb = x.shape[0]
    r = lax.broadcasted_iota(jnp.int32, (tb, tb), 0)
    c = lax.broadcasted_iota(jnp.int32, (tb, tb), 1)
    tri = (c <= r).astype(F32)
    cum = jnp.dot(tri, x, precision=HIGHEST, preferred_element_type=F32) + carry[...]
    lf_ref[...] = x
    cum_ref[...] = cum
    carry[...] = cum[tb - 1:tb, :]


def logf_cumsum(f, row_off, nseq, t, bias, init, apply_gate):
    h = f.shape[1]
    tb = _pick_tile(t, 512, 8)
    nt = t // tb
    off = row_off // tb
    assert row_off % tb == 0
    blk = pl.BlockSpec((tb, h), lambda s, i: (s * nt + i, 0))
    return pl.pallas_call(
        functools.partial(_logf_cumsum_kernel, apply_gate=apply_gate),
        grid=(nseq, nt),
        in_specs=[pl.BlockSpec((tb, h), lambda s, i: (off + s * nt + i, 0)),
                  pl.BlockSpec((1, h), lambda s, i: (0, 0)),
                  pl.BlockSpec((1, 1, h), lambda s, i: (s, 0, 0))],
        out_specs=[blk, blk],
        out_shape=[jax.ShapeDtypeStruct((nseq * t, h), F32)] * 2,
        scratch_shapes=[pltpu.VMEM((1, h), F32)],
        compiler_params=_params(("parallel", "arbitrary")),
        name="logf_cumsum",
    )(f, bias.reshape(1, h), init)


def _fox_select_matrices(h, k_side):
    sel = np.zeros((4, h, h * LANES), np.float32)
    part_lane0, ones_lane0 = (3, 0) if k_side else (0, 3)
    for hh in range(h):
        for p in range(3):
            sel[p, hh, hh * LANES + part_lane0 + p] = 1.0
            sel[3, hh, hh * LANES + ones_lane0 + p] = 1.0
    return jnp.asarray(sel.reshape(4 * h, h * LANES), BF16)


def _fox_build_kernel(x_ref, cum_ref, g_ref, sel_ref, *refs, normalize, mode, aliased, scale, negate, pitch,
                      work_step):
    v_ref = xn_ref = kout_ref = vout_ref = kscr = vscr = xscr = None
    if len(x_ref.shape) == 3:
        refs, xscr = refs[:-1], refs[-1]
    if mode == "kv":
        v_ref = refs[0]
        refs = refs[3:] if aliased else refs[1:]
        xp_ref, kout_ref, vout_ref, kscr, vscr = refs

        @pl.when(pl.program_id(1) != work_step)
        def _():
            kout_ref[...] = jnp.zeros(kout_ref.shape, kout_ref.dtype)
            vout_ref[...] = jnp.zeros(vout_ref.shape, vout_ref.dtype)
    elif mode == "norm":
        xp_ref, xn_ref = refs
    else:
        xp_ref, = refs
    pl.when(pl.program_id(1) == work_step)(functools.partial(
        _fox_build_body, x_ref, cum_ref, g_ref, sel_ref, v_ref, xp_ref, xn_ref, kout_ref, vout_ref, kscr, vscr, xscr,
        normalize=normalize, mode=mode, scale=scale, negate=negate, pitch=pitch))


def _fox_build_body(x_ref, cum_ref, g_ref, sel_ref, v_ref, xp_ref, xn_ref, kout_ref, vout_ref, kscr, vscr, xscr, *,
                    normalize, mode, scale, negate, pitch):
    tm = x_ref.shape[0]
    if xscr is not None:
        def unpack(t, _):
            for g8 in range(x_ref.shape[1] // 8):
                xscr[pl.ds(g8 * 8 * pitch + t, 8, stride=pitch), :] = x_ref[t, g8 * 8:(g8 + 1) * 8, :]
            return 0
        lax.fori_loop(0, tm, unpack, 0, unroll=8)
    c = cum_ref[...] * LOG2E
    if negate:
        c = -c
    c_hi = c.astype(BF16).astype(F32)
    r1 = c - c_hi
    c_mid = r1.astype(BF16).astype(F32)
    c_lo = (r1 - c_mid).astype(BF16).astype(F32)
    parts = jnp.concatenate([c_hi, c_mid, c_lo, jnp.ones_like(c_hi)], axis=1).astype(BF16)
    extra = _dot(parts, sel_ref[...])
    g = g_ref[...]
    for h in range(c.shape[1]):
        xs = x_ref[:, h * LANES:(h + 1) * LANES] if xscr is None else xscr[h * pitch:h * pitch + tm, :]
        if normalize:
            xs = xs * lax.rsqrt(jnp.mean(xs * xs, axis=-1, keepdims=True) + NORM_EPS) * g
        if mode == "norm":
            xn_ref[:, h * LANES:(h + 1) * LANES] = xs
        if mode == "kv":
            kscr[h * pitch:h * pitch + tm, :] = xs
            vscr[h * pitch:h * pitch + tm, :] = v_ref[:, h * LANES:(h + 1) * LANES]
        xp_ref[:, 2 * h * LANES:(2 * h + 1) * LANES] = (xs * scale).astype(BF16)
        xp_ref[:, (2 * h + 1) * LANES:(2 * h + 2) * LANES] = extra[:, h * LANES:(h + 1) * LANES].astype(BF16)
    if mode == "kv":
        def relay(t, _):
            for g8 in range(c.shape[1] // 8):
                rows8 = pl.ds(g8 * 8 * pitch + t, 8, stride=pitch)
                kout_ref[t, g8 * 8:(g8 + 1) * 8, :] = kscr[rows8, :]
                vout_ref[t, g8 * 8:(g8 + 1) * 8, :] = vscr[rows8, :]
            return 0
        lax.fori_loop(0, tm, relay, 0, unroll=8)


def fox_build(x, col_block, row_off, rows, cum, gain, k_side, normalize, scale, mode="plain", kv_bufs=None,
              n_layers=1, slot=0):
    h = cum.shape[1]
    w = h * LANES
    tm = _pick_tile(rows, 128, 16)
    assert row_off % tm == 0 and h % 8 == 0
    off = row_off // tm
    pitch = _relayout_pitch(tm)
    if x.ndim == 4:
        x_spec = pl.BlockSpec((None, tm, h, LANES), lambda i, l: (slot, off + i, 0, 0))
    else:
        x_spec = pl.BlockSpec((tm, w), lambda i, l: (off + i, col_block))
    in_specs = [x_spec,
                pl.BlockSpec((tm, h), lambda i, l: (i, 0)),
                pl.BlockSpec((1, LANES), lambda i, l: (0, 0)),
                pl.BlockSpec((4 * h, w), lambda i, l: (0, 0))]
    args = [x, cum, gain.reshape(1, LANES), _fox_select_matrices(h, k_side)]
    out_specs = [pl.BlockSpec((tm, 2 * w), lambda i, l: (i, 0))]
    out_shape = [jax.ShapeDtypeStruct((rows, 2 * w), BF16)]
    aliases = {}
    scratch = []
    slot_steps, work_step = 1, 0
    if mode == "kv":
        in_specs.append(pl.BlockSpec((tm, w), lambda i, l: (off + i, col_block + 1)))
        args.append(x)
        if kv_bufs is not None:
            in_specs += [ANY, ANY]
            args += list(kv_bufs)
            aliases = {5: 1, 6: 2}
            kv_spec = pl.BlockSpec((None, tm, h, LANES), lambda i, l: (slot, i, 0, 0))
        else:
            slot_steps, work_step = n_layers, slot
            kv_spec = pl.BlockSpec((None, tm, h, LANES), lambda i, l: (l, i, 0, 0))
        out_specs += [kv_spec, kv_spec]
        out_shape += [jax.ShapeDtypeStruct((n_layers, rows, h, LANES), F32)] * 2
        scratch = [pltpu.VMEM((h * pitch, LANES), F32)] * 2
    elif mode == "norm":
        out_specs.append(pl.BlockSpec((tm, w), lambda i, l: (i, 0)))
        out_shape.append(jax.ShapeDtypeStruct((rows, w), F32))
    if x.ndim == 4:
        scratch = scratch + [pltpu.VMEM((h * pitch, LANES), F32)]
    return pl.pallas_call(
        functools.partial(_fox_build_kernel, normalize=normalize, mode=mode, aliased=bool(aliases),
                          scale=scale, negate=k_side, pitch=pitch, work_step=work_step),
        grid=(rows // tm, slot_steps),
        in_specs=in_specs,
        out_specs=out_specs,
        out_shape=out_shape,
        scratch_shapes=scratch,
        input_output_aliases=aliases,
        compiler_params=_params(("parallel", "arbitrary")),
        name="fox_build",
    )(*args)


def _fox_attn_kernel(q_ref, k_ref, v_ref, z_ref, ybuf_ref, o_ref, vt_scr, s_a, s_b, *, tq, unroll):
    i = pl.program_id(1)

    @pl.when(i == 0)
    def _():
        vt_scr[...] = v_ref[...].astype(BF16).T

    q = q_ref[...]
    bufs = (s_a, s_b)

    def logits(j):
        return _dot_nt(k_ref[pl.ds(pl.multiple_of(j * tq, tq), tq), :], q)

    def step(u, j, carry, diagonal):
        m, l, acc = carry
        st = bufs[u % 2][...]
        if not diagonal:
            st_next = logits(j + 1)
        else:
            kidx = lax.broadcasted_iota(jnp.int32, (tq, tq), 0)
            qidx = lax.broadcasted_iota(jnp.int32, (tq, tq), 1)
            st = jnp.where(kidx <= qidx, st, NEG)
        m_new = jnp.maximum(m, jnp.max(st, axis=0, keepdims=True))
        alpha = jnp.exp2(m - m_new)
        p = jnp.exp2(st - m_new)
        l = alpha * l + jnp.sum(p, axis=0, keepdims=True)
        acc = alpha * acc + _dot(vt_scr[:, pl.ds(pl.multiple_of(j * tq, tq), tq)], p.astype(BF16))
        if not diagonal:
            bufs[(u + 1) % 2][...] = st_next
        return m_new, l, acc

    def trip(jj, carry):
        for u in range(unroll):
            carry = step(u, unroll * jj + u, carry, False)
        return carry

    def finish(rem, j0, carry):
        for u in range(rem):
            carry = step(u, j0 + u, carry, False)
        _, l, acc = step(rem, j0 + rem, carry, True)
        z = z_ref[...]
        o_ref[...] = ((acc / l).T * (z * _sigmoid(z))).astype(o_ref.dtype)

    s_a[...] = logits(0)
    carry = (jnp.full((1, tq), NEG, F32), jnp.zeros((1, tq), F32), jnp.zeros((LANES, tq), F32))
    trips = i // unroll
    carry = lax.fori_loop(0, trips, trip, carry)
    for rem in range(unroll):
        pl.when(i - trips * unroll == rem)(functools.partial(finish, rem, trips * unroll, carry))


def fox_attention_prompt(qp, kp, proj, nseq, t, nheads, ybuf):
    tq = _pick_tile(t, FOX_TQ, LANES)
    assert FOX_UNROLL % 2 == 0
    nq = t // tq
    blk = lambda g, i: (g // nheads * nq + i, g % nheads)
    return pl.pallas_call(
        functools.partial(_fox_attn_kernel, tq=tq, unroll=FOX_UNROLL),
        grid=(nseq * nheads, nq),
        in_specs=[pl.BlockSpec((tq, 2 * LANES), blk),
                  pl.BlockSpec((t, 2 * LANES), lambda g, i: (g // nheads, g % nheads)),
                  pl.BlockSpec((t, LANES), lambda g, i: (g // nheads, 2 * nheads + g % nheads)),
                  pl.BlockSpec((tq, LANES), lambda g, i: (g // nheads * nq + i, 3 * nheads + g % nheads)),
                  ANY],
        out_specs=pl.BlockSpec((tq, LANES), blk),
        out_shape=jax.ShapeDtypeStruct(ybuf.shape, ybuf.dtype),
        scratch_shapes=[pltpu.VMEM((LANES, t), BF16), pltpu.VMEM((tq, tq), F32), pltpu.VMEM((tq, tq), F32)],
        input_output_aliases={4: 0},
        compiler_params=_params(("parallel", "arbitrary")),
        name="fox_attention",
    )(qp, kp, proj, proj, ybuf)


def _fox_attn_sample_kernel(q_ref, kc_ref, kn_ref, vc_ref, vn_ref, z_ref, ybuf_ref, o_ref, vscr, *, nh, pitch):
    ts = q_ref.shape[0]
    p = vc_ref.shape[0]

    def unpack(t, _):
        vscr[pl.ds(t, nh, stride=pitch), :] = vc_ref[t]
        return 0
    lax.fori_loop(0, p, unpack, 0, unroll=8)
    r = lax.broadcasted_iota(jnp.int32, (ts, ts), 0)
    c = lax.broadcasted_iota(jnp.int32, (ts, ts), 1)
    for u in range(nh):
        q = q_ref[:, 2 * u * LANES:2 * (u + 1) * LANES]
        s_c = _dot_nt(q, kc_ref[:, 2 * u * LANES:2 * (u + 1) * LANES])
        s_n = jnp.where(c <= r, _dot_nt(q, kn_ref[:, 2 * u * LANES:2 * (u + 1) * LANES]), NEG)
        m = jnp.maximum(jnp.max(s_c, axis=-1, keepdims=True), jnp.max(s_n, axis=-1, keepdims=True))
        p_c = jnp.exp2(s_c - m)
        p_n = jnp.exp2(s_n - m)
        l = jnp.sum(p_c, axis=-1, keepdims=True) + jnp.sum(p_n, axis=-1, keepdims=True)
        acc = (_dot(p_c.astype(BF16), vscr[u * pitch:u * pitch + p, :].astype(BF16))
               + _dot(p_n.astype(BF16), vn_ref[:, u * LANES:(u + 1) * LANES].astype(BF16)))
        z = z_ref[:, u * LANES:(u + 1) * LANES]
        o_ref[:, u * LANES:(u + 1) * LANES] = (acc / l * (z * _sigmoid(z))).astype(o_ref.dtype)


def fox_attention_sample(qp, kp_c, kp_n, cache_v, slot, proj, row_off, nseq, ts, p, nheads, ybuf):
    nh = FOX_HEADS_PER_STEP
    ng = nheads // nh
    assert row_off % ts == 0 and nheads % nh == 0
    off = row_off // ts
    pitch = _relayout_pitch(p)
    return pl.pallas_call(
        functools.partial(_fox_attn_sample_kernel, nh=nh, pitch=pitch),
        grid=(nseq, ng),
        in_specs=[pl.BlockSpec((ts, 2 * nh * LANES), lambda s, g: (s, g)),
                  pl.BlockSpec((p, 2 * nh * LANES), lambda s, g: (s, g)),
                  pl.BlockSpec((ts, 2 * nh * LANES), lambda s, g: (s, g)),
                  pl.BlockSpec((None, p, nh, LANES), lambda s, g: (slot, s, g, 0)),
                  pl.BlockSpec((ts, nh * LANES), lambda s, g: (off + s, 2 * ng + g)),
                  pl.BlockSpec((ts, nh * LANES), lambda s, g: (off + s, 3 * ng + g)),
                  ANY],
        out_specs=pl.BlockSpec((ts, nh * LANES), lambda s, g: (off + s, g)),
        out_shape=jax.ShapeDtypeStruct(ybuf.shape, ybuf.dtype),
        scratch_shapes=[pltpu.VMEM((nh * pitch, LANES), F32)],
        input_output_aliases={6: 0},
        compiler_params=_params(("parallel", "arbitrary")),
        name="fox_attention_sample",
    )(qp, kp_c, kp_n, cache_v, proj, proj, ybuf)


def fox_layer(h, mp_rows, b, t, bs, ts, cache_k, cache_v, cache_lf, w_in, slot, b_f, q_gain, k_gain, kv_bufs):
    nheads = b_f.shape[0]
    w = nheads * LANES
    n_layers, _, p = cache_k.shape[:3]
    qscale = float(LANES) ** -0.5 * LOG2E
    wt = jnp.swapaxes(w_in, 1, 2)
    proj = matmul(h, wt, 4 * w, slot, True)
    fgate = matmul(h, wt[slot, 4 * w:, :], nheads, None, True)
    ybuf = jnp.zeros((h.shape[0], w), BF16)
    lf_p, cum_p = logf_cumsum(fgate, 0, b, t, b_f, jnp.zeros((b, 1, nheads), F32), True)
    qp_p, = fox_build(proj, 0, 0, mp_rows, cum_p, q_gain, False, True, qscale)
    kp_p, kbuf, vbuf = fox_build(proj, 1, 0, mp_rows, cum_p, k_gain, True, True, 1.0, "kv", kv_bufs, n_layers, slot)
    ybuf = fox_attention_prompt(qp_p, kp_p, proj, b, t, nheads, ybuf)
    ms_rows = bs * ts
    _, cum_c = logf_cumsum(cache_lf.reshape(bs * p, nheads), 0, bs, p, b_f, jnp.zeros((bs, 1, nheads), F32), False)
    init_s = cum_c.reshape(bs, p, nheads)[:, p - 1:, :]
    lf_s, cum_s = logf_cumsum(fgate, mp_rows, bs, ts, b_f, init_s, True)
    qp_s, = fox_build(proj, 0, mp_rows, ms_rows, cum_s, q_gain, False, True, qscale)
    kp_s, kn_s = fox_build(proj, 1, mp_rows, ms_rows, cum_s, k_gain, True, True, 1.0, "norm")
    by_head = (n_layers, bs * p, nheads, LANES)
    kp_c, = fox_build(cache_k.reshape(by_head), 0, 0, bs * p, cum_c, k_gain, True, False, 1.0, slot=slot)
    ybuf = fox_attention_sample(qp_s, kp_c, kp_s, cache_v.reshape(by_head), slot, proj, mp_rows, bs, ts, p,
                                nheads, ybuf)
    v_s = proj[mp_rows:, 2 * w:3 * w]
    hd = LANES
    outs = (lf_p.reshape(b, t, nheads), kn_s.reshape(bs, ts, nheads, hd), v_s.reshape(bs, ts, nheads, hd),
            lf_s.reshape(bs, ts, nheads))
    return ybuf, (kbuf, vbuf), outs


def _mlstm_kernel(q_ref, k_ref, v_ref, o_ref, z_ref, gate_ref, bi_ref, bf_ref, gain_ref, expand_ref,
                  ct0_ref, n0_ref, m0_ref, ybuf_ref, y_ref, ct_ref, n_ref, m_ref, *, nheads, dk, dv):
    @pl.when(pl.program_id(1) == 0)
    def _():
        ct_ref[...] = ct0_ref[...]
        n_ref[...] = n0_ref[...]
        m_ref[...] = m0_ref[...]

    ln = q_ref.shape[0]
    pack = MXU_DIM // ln
    ngrp = nheads // pack
    rr = pack * ln
    lsh = ln.bit_length() - 1
    gw = pack * dv
    gates = gate_ref[...]
    log_i = gates[:, :nheads] + bi_ref[...]
    log_f = _log_sigmoid(gates[:, nheads:] + bf_ref[...])
    rl = lax.broadcasted_iota(jnp.int32, (ln, ln), 0)
    cl = lax.broadcasted_iota(jnp.int32, (ln, ln), 1)
    b_all = jnp.dot((cl <= rl).astype(F32), log_f, precision=HIGHEST, preferred_element_type=F32)
    m_row = m_ref[0, :, 0:nheads]
    b_last = b_all[ln - 1:ln, :]
    g_all = b_last - b_all + log_i
    m_new = jnp.maximum(b_last + m_row, jnp.max(g_all, axis=0, keepdims=True))
    wk_all = jnp.exp(g_all - m_new)
    carry_w = jnp.exp(b_last + m_row - m_new)
    carry_wide = jnp.dot(carry_w, expand_ref[...], precision=HIGHEST, preferred_element_type=F32)
    rh = lax.broadcasted_iota(jnp.int32, (nheads, nheads), 0)
    ch = lax.broadcasted_iota(jnp.int32, (nheads, nheads), 1)
    carry_col = jnp.sum(jnp.where(rh == ch, carry_w, 0.0), axis=1, keepdims=True)

    def stack_cols(a):
        return jnp.stack([jnp.concatenate([a[:, g * pack + p:g * pack + p + 1] for p in range(pack)], axis=0)
                          for g in range(ngrp)], axis=0)

    def stack_heads(ref, width):
        return jnp.stack([jnp.concatenate([ref[:, (g * pack + p) * width:(g * pack + p + 1) * width]
                                           for p in range(pack)], axis=0) for g in range(ngrp)], axis=0)

    bcol = stack_cols(b_all)
    licol = stack_cols(log_i)
    mprev = stack_cols(jnp.broadcast_to(m_row, (ln, nheads)))
    wkcol = stack_cols(wk_all)
    r = lax.broadcasted_iota(jnp.int32, (rr, rr), 0)
    c = lax.broadcasted_iota(jnp.int32, (rr, rr), 1)
    incl = ((r >> lsh) == (c >> lsh)) & (c <= r)
    brow = jnp.sum(jnp.where(r == c, bcol, 0.0), axis=1, keepdims=True)
    lirow = jnp.sum(jnp.where(r == c, licol, 0.0), axis=1, keepdims=True)
    d = jnp.where(incl, bcol - brow + lirow, -jnp.inf)
    inter = bcol + mprev
    m_t = jnp.maximum(inter, jnp.max(d, axis=-1, keepdims=True))
    qf = stack_heads(q_ref, dk)
    kf = stack_heads(k_ref, dk) * (float(dk) ** -0.5)
    vf = stack_heads(v_ref, dv)
    qb, kb, vb = qf.astype(BF16), kf.astype(BF16), vf.astype(BF16)
    wmat = jnp.exp(d - m_t) * _bmm_nt(qb, kb)
    inter_w = jnp.exp(inter - m_t)
    ctb = ct_ref[0].astype(BF16)
    rowblk = lax.broadcasted_iota(jnp.int32, (rr, 1), 0) >> lsh

    def own_block(wide):
        out = jnp.where(rowblk == 0, wide[:, :dv], 0.0)
        for p in range(1, pack):
            out = out + jnp.where(rowblk == p, wide[:, p * dv:(p + 1) * dv], 0.0)
        return out

    qc = jnp.stack([own_block(_dot(qb[g], ctb[:, g * gw:(g + 1) * gw])) for g in range(ngrp)], axis=0)
    num = _bmm(wmat.astype(BF16), vb) + inter_w * qc
    nst = jnp.stack([jnp.concatenate([jnp.broadcast_to(n_ref[0, g * pack + p:g * pack + p + 1, :], (ln, dk))
                                      for p in range(pack)], axis=0) for g in range(ngrp)], axis=0)
    den = jnp.sum(wmat, axis=-1, keepdims=True) + inter_w * jnp.sum(qf * nst, axis=-1, keepdims=True)
    hid = num / jnp.maximum(jnp.abs(den), jnp.exp(-m_t))
    own = ((lax.broadcasted_iota(jnp.int32, (rr, gw), 0) >> lsh)
           == (lax.broadcasted_iota(jnp.int32, (rr, gw), 1) >> (dv.bit_length() - 1)))
    blocksum = ((lax.broadcasted_iota(jnp.int32, (pack, rr), 1) >> lsh)
                == lax.broadcasted_iota(jnp.int32, (pack, rr), 0)).astype(F32)
    for g in range(ngrp):
        wv = (wkcol[g] * vf[g]).astype(BF16)
        vwbd = jnp.where(own, jnp.concatenate([wv] * pack, axis=1), jnp.zeros((), BF16))
        cols = slice(g * gw, (g + 1) * gw)
        ct_ref[0, :, cols] = carry_wide[:, cols] * ct_ref[0, :, cols] + _dot_tn(kb[g], vwbd)
        heads = slice(g * pack, (g + 1) * pack)
        ksum = jnp.dot(blocksum, wkcol[g] * kf[g], precision=HIGHEST, preferred_element_type=F32)
        n_ref[0, heads, :] = carry_col[heads, :] * n_ref[0, heads, :] + ksum
        for p in range(pack):
            h = g * pack + p
            hh = hid[g, p * ln:(p + 1) * ln, :]
            hn = hh * lax.rsqrt(jnp.mean(hh * hh, axis=-1, keepdims=True) + NORM_EPS)
            hn = hn * gain_ref[:, h * dv:(h + 1) * dv]
            og = o_ref[:, h * dv:(h + 1) * dv]
            zg = z_ref[:, h * dv:(h + 1) * dv]
            y_ref[:, h * dv:(h + 1) * dv] = (hn * _sigmoid(og) * (zg * _sigmoid(zg))).astype(y_ref.dtype)
    m_ref[0, :, 0:nheads] = m_new


def mlstm_scan(proj, gates, row_off, nseq, t, ln, c0, n0, m0, b_i, b_f, gain, ybuf):
    nheads, dv, dk = c0.shape[1:]
    nc = t // ln
    assert row_off % ln == 0
    off = row_off // ln
    qk_w, v_w = nheads * dk, nheads * dv
    assert v_w == 2 * qk_w

    def rows(width, cb):
        return pl.BlockSpec((ln, width), lambda s, i: (off + s * nc + i, cb))

    assert MXU_DIM % ln == 0 and nheads % (MXU_DIM // ln) == 0 and nheads <= LANES
    ct0 = jnp.transpose(c0, (0, 3, 1, 2)).reshape(nseq, dk, v_w)
    m0p = jnp.pad(m0[:, None, :], ((0, 0), (0, 0), (0, LANES - nheads)))
    expand = jnp.asarray(np.kron(np.eye(nheads, dtype=np.float32), np.ones((1, dv), np.float32)))
    state = [pl.BlockSpec((1, dk, v_w), lambda s, i: (s, 0, 0)),
             pl.BlockSpec((1, nheads, dk), lambda s, i: (s, 0, 0)),
             pl.BlockSpec((1, 1, LANES), lambda s, i: (s, 0, 0))]
    y, ct, n, m = pl.pallas_call(
        functools.partial(_mlstm_kernel, nheads=nheads, dk=dk, dv=dv),
        grid=(nseq, nc),
        in_specs=[rows(qk_w, 0), rows(qk_w, 1), rows(v_w, 1), rows(v_w, 2), rows(v_w, 3),
                  rows(2 * nheads, 0),
                  pl.BlockSpec((1, nheads), lambda s, i: (0, 0)),
                  pl.BlockSpec((1, nheads), lambda s, i: (0, 0)),
                  pl.BlockSpec((1, v_w), lambda s, i: (0, 0)),
                  pl.BlockSpec((nheads, v_w), lambda s, i: (0, 0))] + state + [ANY],
        out_specs=[rows(v_w, 0)] + state,
        out_shape=[jax.ShapeDtypeStruct(ybuf.shape, ybuf.dtype),
                   jax.ShapeDtypeStruct((nseq, dk, v_w), F32),
                   jax.ShapeDtypeStruct((nseq, nheads, dk), F32),
                   jax.ShapeDtypeStruct((nseq, 1, LANES), F32)],
        input_output_aliases={13: 0},
        compiler_params=_params(("parallel", "arbitrary")),
        name="mlstm_scan",
    )(proj, proj, proj, proj, proj, gates, b_i.reshape(1, nheads), b_f.reshape(1, nheads),
      gain.reshape(1, v_w), expand, ct0, n0, m0p, ybuf)
    c = jnp.transpose(ct.reshape(nseq, dk, nheads, dv), (0, 2, 3, 1))
    return y, c, n, m[:, 0, :nheads]


def mlstm_layer(h, mp_rows, b, t, bs, ts, c0, n0, m0, w_in, slot, b_i, b_f, gain):
    nheads, dv, dk = c0.shape[1:]
    main = 2 * nheads * dk + 3 * nheads * dv
    wt = jnp.swapaxes(w_in, 1, 2)
    proj = matmul(h, wt, main, slot, True)
    gates = matmul(h, wt[slot, main:, :], 2 * nheads, None, True)
    ybuf = jnp.zeros((h.shape[0], nheads * dv), BF16)
    zc = jnp.zeros((b, nheads, dv, dk), F32)
    ybuf, c_p, n_p, m_p = mlstm_scan(proj, gates, 0, b, t, min(CHUNK, t), zc, zc[:, :, 0, :], zc[:, :, 0, 0],
                                     b_i, b_f, gain, ybuf)
    ybuf, c_s, n_s, m_s = mlstm_scan(proj, gates, mp_rows, bs, ts, ts, c0, n0, m0, b_i, b_f, gain, ybuf)
    return ybuf, (c_p, n_p, m_p, c_s, n_s, m_s)


def _gdn_conv_kernel(x_ref, halo_ref, init_ref, w_ref, sc_ref, o_ref, *, l2norm):
    tm = x_ref.shape[0]
    prev = jnp.where(pl.program_id(1) == 0, init_ref[0], halo_ref[...])
    xc = jnp.concatenate([prev, x_ref[...]], axis=0)
    w = w_ref[...]
    acc = w[3:4] * xc[8:8 + tm] + w[2:3] * xc[7:7 + tm] + w[1:2] * xc[6:6 + tm] + w[0:1] * xc[5:5 + tm]
    act = acc * _sigmoid(acc)
    if l2norm:
        for g in range(act.shape[1] // LANES):
            a = act[:, g * LANES:(g + 1) * LANES]
            a = a * lax.rsqrt(jnp.sum(a * a, axis=-1, keepdims=True) + NORM_EPS)
            o_ref[:, g * LANES:(g + 1) * LANES] = (a * sc_ref[:, g * LANES:(g + 1) * LANES]).astype(o_ref.dtype)
    else:
        o_ref[...] = act.astype(o_ref.dtype)


def gdn_conv(proj, row_off, nseq, t, ch_off, ch, init, conv_w, slot, scale, l2norm, out_dtype):
    tm = _pick_tile(t, 512, 8)
    tc = _pick_tile(ch, 1024, LANES)
    nt = t // tm
    assert row_off % tm == 0 and ch_off % tc == 0 and tm % 8 == 0
    off, coff, sub = row_off // tm, ch_off // tc, tm // 8
    return pl.pallas_call(
        functools.partial(_gdn_conv_kernel, l2norm=l2norm),
        grid=(nseq, nt, ch // tc),
        in_specs=[pl.BlockSpec((tm, tc), lambda s, i, j: (off + s * nt + i, coff + j)),
                  pl.BlockSpec((8, tc), lambda s, i, j: (jnp.maximum((off + s * nt + i) * sub - 1, 0), coff + j)),
                  pl.BlockSpec((1, 8, tc), lambda s, i, j: (s, 0, coff + j)),
                  pl.BlockSpec((None, conv_w.shape[1], tc), lambda s, i, j: (slot, 0, coff + j)),
                  pl.BlockSpec((1, tc), lambda s, i, j: (0, j))],
        out_specs=pl.BlockSpec((tm, tc), lambda s, i, j: (s * nt + i, j)),
        out_shape=jax.ShapeDtypeStruct((nseq * t, ch), out_dtype),
        compiler_params=_params(("parallel", "arbitrary", "arbitrary")),
        name="gdn_conv",
    )(proj, proj, init, conv_w, scale)


def _unit_lower_inverse(a, r, c, levels):
    x = jnp.where(r == c, 1.0, 0.0) - jnp.where(((r >> 1) == (c >> 1)) & (c < r), a, 0.0)
    for sh in range(1, levels):
        off = jnp.where(((r >> (sh + 1)) == (c >> (sh + 1))) & ((r >> sh) != (c >> sh)) & (c < r), a, 0.0)
        xb = x.astype(BF16)
        x = x - _bmm(_bmm(xb, off.astype(BF16)).astype(BF16), xb)
    return x


def _gdn_kernel(qk_ref, v_ref, z_ref, gate_ref, alog_ref, dt_ref, gain_ref, expand_ref, ws0_ref, ybuf_ref,
                y_ref, ws_ref, *, kheads, vheads):
    @pl.when(pl.program_id(1) == 0)
    def _():
        ws_ref[...] = ws0_ref[...]

    ln = v_ref.shape[0]
    hd = LANES
    pack = MXU_DIM // ln
    ngrp = vheads // pack
    rr = pack * ln
    lsh = ln.bit_length() - 1
    rep = vheads // kheads
    gates = gate_ref[...]
    g_all = -jnp.exp(alog_ref[...]) * _softplus(gates[:, :vheads] + dt_ref[...])
    beta_all = _sigmoid(gates[:, vheads:])
    rl = lax.broadcasted_iota(jnp.int32, (ln, ln), 0)
    cl = lax.broadcasted_iota(jnp.int32, (ln, ln), 1)
    gsum_all = jnp.dot((cl <= rl).astype(F32), g_all, precision=HIGHEST, preferred_element_type=F32)
    glast_row = gsum_all[ln - 1:ln, :]
    decay_row = jnp.exp(jnp.dot(glast_row, expand_ref[...], precision=HIGHEST, preferred_element_type=F32))

    def stack_cols(a):
        return jnp.stack([jnp.concatenate([a[:, g * pack + p:g * pack + p + 1] for p in range(pack)], axis=0)
                          for g in range(ngrp)], axis=0)

    def stack_heads(ref, head_of):
        return jnp.stack([jnp.concatenate([ref[:, head_of(g * pack + p) * hd:(head_of(g * pack + p) + 1) * hd]
                                           for p in range(pack)], axis=0) for g in range(ngrp)], axis=0)

    gcol = stack_cols(gsum_all)
    bt = stack_cols(beta_all)
    glast = stack_cols(jnp.broadcast_to(glast_row, (ln, vheads)))
    r = lax.broadcasted_iota(jnp.int32, (rr, rr), 0)
    c = lax.broadcasted_iota(jnp.int32, (rr, rr), 1)
    blk = (r >> lsh) == (c >> lsh)
    incl = blk & (c <= r)
    strict = blk & (c < r)
    grow = jnp.sum(jnp.where(r == c, gcol, 0.0), axis=1, keepdims=True)
    dincl = jnp.where(incl, jnp.exp(jnp.where(incl, gcol - grow, 0.0)), 0.0)
    qst = stack_heads(qk_ref, lambda h: h // rep)
    kst = stack_heads(qk_ref, lambda h: kheads + h // rep)
    vst = stack_heads(v_ref, lambda h: h)
    kk = _bmm_nt(kst, kst)
    qk = _bmm_nt(qst, kst)
    x = _unit_lower_inverse(jnp.where(strict, bt * dincl * kk, 0.0), r, c, lsh)
    eg = jnp.exp(gcol)
    wsb = ws_ref[0].astype(BF16)
    gw = pack * hd
    rowblk = lax.broadcasted_iota(jnp.int32, (rr, 1), 0) >> lsh

    def own_block(wide):
        out = jnp.where(rowblk == 0, wide[:, :hd], 0.0)
        for p in range(1, pack):
            out = out + jnp.where(rowblk == p, wide[:, p * hd:(p + 1) * hd], 0.0)
        return out

    ksd = jnp.stack([own_block(_dot(kst[g], wsb[:, g * gw:(g + 1) * gw])) for g in range(ngrp)], axis=0)
    qsd = jnp.stack([own_block(_dot(qst[g], wsb[:, g * gw:(g + 1) * gw])) for g in range(ngrp)], axis=0)
    rhs = bt * (vst - eg * ksd)
    u = _bmm(x.astype(BF16), rhs.astype(BF16))
    o = eg * qsd + _bmm((dincl * qk).astype(BF16), u.astype(BF16))
    uw = (jnp.exp(glast - gcol) * u).astype(BF16)
    own = ((lax.broadcasted_iota(jnp.int32, (rr, gw), 0) >> lsh)
           == (lax.broadcasted_iota(jnp.int32, (rr, gw), 1) >> (hd.bit_length() - 1)))
    gain = gain_ref[...]
    for g in range(ngrp):
        uwbd = jnp.where(own, jnp.concatenate([uw[g]] * pack, axis=1), jnp.zeros((), BF16))
        cols = slice(g * gw, (g + 1) * gw)
        ws_ref[0, :, cols] = decay_row[:, cols] * ws_ref[0, :, cols] + _dot_tn(kst[g], uwbd)
        for p in range(pack):
            h = g * pack + p
            oh = o[g, p * ln:(p + 1) * ln, :]
            on = oh * lax.rsqrt(jnp.mean(oh * oh, axis=-1, keepdims=True) + NORM_EPS) * gain
            zg = z_ref[:, h * hd:(h + 1) * hd]
            y_ref[:, h * hd:(h + 1) * hd] = (on * (zg * _sigmoid(zg))).astype(y_ref.dtype)


def gdn_scan(qk, v, proj, z_col_block, gates, row_off, nseq, t, ln, s0, a_log, dt_bias, gain, ybuf):
    vheads, dv, dk = s0.shape[1:]
    v_w = vheads * LANES
    kheads = qk.shape[1] // (2 * LANES)
    nc = t // ln
    assert row_off % ln == 0 and MXU_DIM % ln == 0 and vheads % (MXU_DIM // ln) == 0 and dv == LANES and dk == LANES
    off = row_off // ln
    ws0 = jnp.transpose(s0, (0, 3, 1, 2)).reshape(nseq, dk, v_w)
    expand = jnp.asarray(np.kron(np.eye(vheads, dtype=np.float32), np.ones((1, LANES), np.float32)))
    local = lambda width: pl.BlockSpec((ln, width), lambda s, i: (s * nc + i, 0))
    state = pl.BlockSpec((1, dk, v_w), lambda s, i: (s, 0, 0))
    vec = lambda width: pl.BlockSpec((1, width), lambda s, i: (0, 0))
    ybuf, ws = pl.pallas_call(
        functools.partial(_gdn_kernel, kheads=kheads, vheads=vheads),
        grid=(nseq, nc),
        in_specs=[local(qk.shape[1]), local(v_w),
                  pl.BlockSpec((ln, v_w), lambda s, i: (off + s * nc + i, z_col_block)),
                  pl.BlockSpec((ln, 2 * vheads), lambda s, i: (off + s * nc + i, 0)),
                  vec(vheads), vec(vheads), vec(LANES),
                  pl.BlockSpec((vheads, v_w), lambda s, i: (0, 0)), state, ANY],
        out_specs=[pl.BlockSpec((ln, v_w), lambda s, i: (off + s * nc + i, 0)), state],
        out_shape=[jax.ShapeDtypeStruct(ybuf.shape, ybuf.dtype),
                   jax.ShapeDtypeStruct((nseq, dk, v_w), F32)],
        input_output_aliases={9: 0},
        compiler_params=_params(("parallel", "arbitrary")),
        name="gdn_scan",
    )(qk, v, proj, gates, a_log.reshape(1, vheads), dt_bias.reshape(1, vheads), gain.reshape(1, LANES),
      expand, ws0, ybuf)
    return ybuf, jnp.transpose(ws.reshape(nseq, dk, vheads, dv), (0, 2, 3, 1))


def gdn_layer(h, mp_rows, b, t, bs, ts, s0, conv0, w_in, slot, conv_w, a_log, dt_bias, gain):
    vheads = s0.shape[1]
    v_w = vheads * LANES
    taps, ch = conv_w.shape[1:]
    qk_w = ch - v_w
    wt = jnp.swapaxes(w_in, 1, 2)
    proj = matmul(h, wt, ch + v_w, slot, True)
    gates = matmul(h, wt[slot, ch + v_w:, :], 2 * vheads, None, True)
    scale = jnp.concatenate([jnp.full((1, qk_w // 2), float(LANES) ** -0.5, F32), jnp.ones((1, qk_w // 2), F32)], axis=1)
    ones = jnp.ones((1, v_w), F32)
    ybuf = jnp.zeros((h.shape[0], v_w), BF16)
    outs = []
    for row_off, nseq, tt, ln, init_rows, st0 in (
            (0, b, t, min(CHUNK, t), jnp.zeros((b, taps - 1, ch), F32), jnp.zeros((b,) + s0.shape[1:], F32)),
            (mp_rows, bs, ts, ts, conv0, s0)):
        init = jnp.pad(init_rows, ((0, 0), (8 - (taps - 1), 0), (0, 0)))
        qk = gdn_conv(proj, row_off, nseq, tt, 0, qk_w, init, conv_w, slot, scale, True, BF16)
        v = gdn_conv(proj, row_off, nseq, tt, qk_w, v_w, init, conv_w, slot, ones, False, F32)
        ybuf, s_new = gdn_scan(qk, v, proj, ch // v_w, gates, row_off, nseq, tt, ln, st0, a_log[slot], dt_bias[slot],
                               gain[slot], ybuf)
        conv_state = jnp.stack([proj[row_off + (s + 1) * tt - (taps - 1):row_off + (s + 1) * tt, :ch]
                                for s in range(nseq)])
        outs += [s_new, conv_state]
    return ybuf, tuple(outs)


def kernel(x_prompt, x_sample, cache_fox_k, cache_fox_v, cache_fox_logf, state_mlstm_c, state_mlstm_n, state_mlstm_m, state_gdn_s, state_gdn_conv, pre_norm, post_norm, fox_w_in, fox_b_f, fox_q_norm, fox_k_norm, fox_w_out, mlstm_w_in, mlstm_b_i, mlstm_b_f, mlstm_h_norm, mlstm_w_out, gdn_w_in, gdn_conv_w, gdn_a_log, gdn_dt_bias, gdn_o_norm, gdn_w_out):
    b, t, d = x_prompt.shape
    bs, ts, _ = x_sample.shape
    mp_rows = b * t
    n_fox = fox_w_in.shape[0]
    x = jnp.concatenate([x_prompt.reshape(mp_rows, d), x_sample.reshape(bs * ts, d)], axis=0)
    kv_bufs = None
    fox, ml, gd = [], [], []
    depth = pre_norm.shape[0]
    h = rmsnorm_cast(x, pre_norm[0])
    for layer in range(depth):
        kind, j = layer % 3, layer // 3
        if kind == 0:
            y, kv_bufs, outs = fox_layer(h, mp_rows, b, t, bs, ts, cache_fox_k, cache_fox_v, cache_fox_logf[j],
                                         fox_w_in, j, fox_b_f[j], fox_q_norm[j], fox_k_norm[j], kv_bufs)
            fox.append(outs)
            w_out = fox_w_out
        elif kind == 1:
            y, outs = mlstm_layer(h, mp_rows, b, t, bs, ts, state_mlstm_c[j], state_mlstm_n[j], state_mlstm_m[j],
                                  mlstm_w_in, j, mlstm_b_i[j], mlstm_b_f[j], mlstm_h_norm[j])
            ml.append(outs)
            w_out = mlstm_w_out
        else:
            y, outs = gdn_layer(h, mp_rows, b, t, bs, ts, state_gdn_s[j], state_gdn_conv[j], gdn_w_in, j,
                                gdn_conv_w, gdn_a_log, gdn_dt_bias, gdn_o_norm)
            gd.append(outs)
            w_out = gdn_w_out
        mixed = matmul(y, w_out, d, layer=j)
        if layer + 1 < depth:
            x, h = postnorm_residual(mixed, x, post_norm[layer], pre_norm[layer + 1])
        else:
            x = postnorm_residual(mixed, x, post_norm[layer])
    stack = lambda group, idx: jnp.stack([o[idx] for o in group])
    nh = fox_b_f.shape[1]
    kv_shape = (n_fox, b, t, nh, LANES)
    return ((x[:mp_rows].reshape(b, t, d), x[mp_rows:].reshape(bs, ts, d),
             kv_bufs[0].reshape(kv_shape), kv_bufs[1].reshape(kv_shape))
            + tuple(stack(fox, i) for i in range(4))
            + tuple(stack(ml, i) for i in range(6))
            + tuple(stack(gd, i) for i in range(4)))
```

```python
import functools
import math

import numpy as np
import jax
import jax.numpy as jnp
from jax import lax
from jax.experimental import pallas as pl
from jax.experimental.pallas import tpu as pltpu

F32 = jnp.float32
BF16 = jnp.bfloat16
NORM_EPS = 1e-6
CHUNK = 64
LANES = 128
MXU_DIM = 256
NEG = -1e30
VMEM_LIMIT = 56 * 1024 * 1024
FOX_TQ = 512
FOX_UNROLL = 4
FOX_HEADS_PER_STEP = 8
LOG2E = math.log2(math.e)
HIGHEST = lax.Precision.HIGHEST
ANY = pl.BlockSpec(memory_space=pl.ANY)


def _params(sem, vmem=VMEM_LIMIT):
    return pltpu.CompilerParams(dimension_semantics=sem, vmem_limit_bytes=vmem)


def _pick_tile(n, target, mult):
    best = None
    for t in range(mult, min(n, target) + 1, mult):
        if n % t == 0:
            best = t
    assert best is not None, (n, target, mult)
    return best


def _relayout_pitch(rows):
    assert rows % 8 == 0
    pitch = rows + 8
    return pitch if (pitch // 8) % 2 else pitch + 8


def _sigmoid(x):
    return 0.5 * jnp.tanh(0.5 * x) + 0.5


def _log_sigmoid(x):
    return jnp.minimum(x, 0.0) - jnp.log(1.0 + jnp.exp(-jnp.abs(x)))


def _softplus(x):
    return jnp.maximum(x, 0.0) + jnp.log(1.0 + jnp.exp(-jnp.abs(x)))


def _dot(a, b):
    return jnp.dot(a, b, preferred_element_type=F32)


def _dot_nt(a, b):
    return lax.dot_general(a, b, (((1,), (1,)), ((), ())), preferred_element_type=F32)


def _dot_tn(a, b):
    return lax.dot_general(a, b, (((0,), (0,)), ((), ())), preferred_element_type=F32)


def _bmm(a, b):
    return lax.dot_general(a, b, (((2,), (1,)), ((0,), (0,))), preferred_element_type=F32)


def _bmm_nt(a, b):
    return lax.dot_general(a, b, (((2,), (2,)), ((0,), (0,))), preferred_element_type=F32)


def _pair_specs(tm, d, npb):
    return [pl.BlockSpec((tm, d), lambda i: (jnp.minimum(i, npb - 1), 0)),
            pl.BlockSpec((tm, d), lambda i: (jnp.maximum(i - npb, 0), 0))]


def _rmsnorm_cast_kernel(xp_ref, xs_ref, g_ref, o_ref, *, npb):
    x = jnp.where(pl.program_id(0) < npb, xp_ref[...], xs_ref[...])
    ms = jnp.mean(x * x, axis=-1, keepdims=True)
    o_ref[...] = (x * lax.rsqrt(ms + NORM_EPS) * g_ref[...]).astype(o_ref.dtype)


def rmsnorm_cast(xp, xs, g):
    (mp, d), ms = xp.shape, xs.shape[0]
    tm = _pick_tile(math.gcd(mp, ms), 256, 16)
    npb = mp // tm
    return pl.pallas_call(
        functools.partial(_rmsnorm_cast_kernel, npb=npb),
        grid=((mp + ms) // tm,),
        in_specs=_pair_specs(tm, d, npb) + [pl.BlockSpec((1, d), lambda i: (0, 0))],
        out_specs=pl.BlockSpec((tm, d), lambda i: (i, 0)),
        out_shape=jax.ShapeDtypeStruct((mp + ms, d), BF16),
        compiler_params=_params(("arbitrary",)),
        name="rmsnorm_cast",
    )(xp, xs, g.reshape(1, d))


def _postnorm_residual_kernel(*refs, npb, pair_in, pair_out, fused):
    y_ref, refs = refs[0], refs[1:]
    first = pl.program_id(0) < npb
    if pair_in:
        x_old, refs = jnp.where(first, refs[0][...], refs[1][...]), refs[2:]
    else:
        x_old, refs = refs[0][...], refs[1:]
    g_ref, refs = refs[0], refs[1:]
    y = y_ref[...]
    x = x_old + y * lax.rsqrt(jnp.mean(y * y, axis=-1, keepdims=True) + NORM_EPS) * g_ref[...]
    if fused:
        gn_ref, refs = refs[0], refs[1:]
        refs[-1][...] = (x * lax.rsqrt(jnp.mean(x * x, axis=-1, keepdims=True) + NORM_EPS)
                         * gn_ref[...]).astype(refs[-1].dtype)
    if pair_out:
        @pl.when(first)
        def _():
            refs[0][...] = x

        @pl.when(jnp.logical_not(first))
        def _():
            refs[1][...] = x
    else:
        refs[0][...] = x


def postnorm_residual(y, x, g, mp, g_next=None, pair_out=False):
    m, d = y.shape
    pair_in = isinstance(x, tuple)
    tm = _pick_tile(math.gcd(mp, m - mp), 256, 16)
    npb = mp // tm
    row = pl.BlockSpec((tm, d), lambda i: (i, 0))
    vec = pl.BlockSpec((1, d), lambda i: (0, 0))
    fused = g_next is not None
    in_specs = [row] + (_pair_specs(tm, d, npb) if pair_in else [row]) + [vec] + ([vec] if fused else [])
    args = [y] + (list(x) if pair_in else [x]) + [g.reshape(1, d)] + ([g_next.reshape(1, d)] if fused else [])
    if pair_out:
        out_specs = _pair_specs(tm, d, npb)
        out_shape = [jax.ShapeDtypeStruct((mp, d), F32), jax.ShapeDtypeStruct((m - mp, d), F32)]
    else:
        out_specs, out_shape = [row], [jax.ShapeDtypeStruct((m, d), F32)]
    if fused:
        out_specs, out_shape = out_specs + [row], out_shape + [jax.ShapeDtypeStruct((m, d), BF16)]
    return pl.pallas_call(
        functools.partial(_postnorm_residual_kernel, npb=npb, pair_in=pair_in, pair_out=pair_out, fused=fused),
        grid=(m // tm,),
        in_specs=in_specs,
        out_specs=out_specs,
        out_shape=out_shape,
        compiler_params=_params(("arbitrary",)),
        name="postnorm_residual",
    )(*args)


def _matmul_kernel(a_ref, w_ref, o_ref, wb_ref, *, w_transposed):
    @pl.when(pl.program_id(1) == 0)
    def _():
        wb_ref[...] = w_ref[...].astype(BF16)

    dot = _dot_nt if w_transposed else _dot
    o_ref[...] = dot(a_ref[...], wb_ref[...]).astype(o_ref.dtype)


def matmul(a, w, n_cols, layer=None, w_transposed=False, tn_target=512, tm_target=1408):
    m, k = a.shape
    tm = _pick_tile(m, tm_target, 16)
    tn = n_cols if n_cols < LANES else _pick_tile(n_cols, tn_target, LANES)
    blk = (tn, k) if w_transposed else (k, tn)
    pick = (lambda j: (j, 0)) if w_transposed else (lambda j: (0, j))
    if layer is None:
        w_spec = pl.BlockSpec(blk, lambda j, i: pick(j))
    else:
        w_spec = pl.BlockSpec((None,) + blk, lambda j, i: (layer,) + pick(j))
    return pl.pallas_call(
        functools.partial(_matmul_kernel, w_transposed=w_transposed),
        grid=(n_cols // tn, m // tm),
        in_specs=[pl.BlockSpec((tm, k), lambda j, i: (i, 0)), w_spec],
        out_specs=pl.BlockSpec((tm, tn), lambda j, i: (i, j)),
        out_shape=jax.ShapeDtypeStruct((m, n_cols), F32),
        scratch_shapes=[pltpu.VMEM(blk, BF16)],
        compiler_params=_params(("parallel", "arbitrary")),
        name="matmul",
    )(a, w)


def _logf_cumsum_kernel(f_ref, b_ref, init_ref, lf_ref, cum_ref, carry, *, apply_gate):
    @pl.when(pl.program_id(1) == 0)
    def _():
        carry[...] = init_ref[0]

    x = f_ref[...]
    if apply_gate:
        x = _log_sigmoid(x + b_ref[...])
    tb = x.shape[0]
    r = lax.broadcasted_iota(jnp.int32, (tb, tb), 0)
    c = lax.broadcasted_iota(jnp.int32, (tb, tb), 1)
    tri = (c <= r).astype(F32)
    cum = jnp.dot(tri, x, precision=HIGHEST, preferred_element_type=F32) + carry[...]
    lf_ref[...] = x
    cum_ref[...] = cum
    carry[...] = cum[tb - 1:tb, :]


def logf_cumsum(f, row_off, nseq, t, bias, init, apply_gate):
    h = f.shape[1]
    tb = _pick_tile(t, 512, 8)
    nt = t // tb
    off = row_off // tb
    assert row_off % tb == 0
    blk = pl.BlockSpec((tb, h), lambda s, i: (s * nt + i, 0))
    return pl.pallas_call(
        functools.partial(_logf_cumsum_kernel, apply_gate=apply_gate),
        grid=(nseq, nt),
        in_specs=[pl.BlockSpec((tb, h), lambda s, i: (off + s * nt + i, 0)),
                  pl.BlockSpec((1, h), lambda s, i: (0, 0)),
                  pl.BlockSpec((1, 1, h), lambda s, i: (s, 0, 0))],
        out_specs=[blk, blk],
        out_shape=[jax.ShapeDtypeStruct((nseq * t, h), F32)] * 2,
        scratch_shapes=[pltpu.VMEM((1, h), F32)],
        compiler_params=_params(("parallel", "arbitrary")),
        name="logf_cumsum",
    )(f, bias.reshape(1, h), init)


def _fox_select_matrices(h, k_side):
    sel = np.zeros((4, h, h * LANES), np.float32)
    part_lane0, ones_lane0 = (3, 0) if k_side else (0, 3)
    for hh in range(h):
        for p in range(3):
            sel[p, hh, hh * LANES + part_lane0 + p] = 1.0
            sel[3, hh, hh * LANES + ones_lane0 + p] = 1.0
    return jnp.asarray(sel.reshape(4 * h, h * LANES), BF16)


def _fox_build_kernel(x_ref, cum_ref, g_ref, sel_ref, *refs, normalize, mode, aliased, scale, negate, pitch,
                      work_step):
    v_ref = xn_ref = kout_ref = vout_ref = kscr = vscr = xscr = None
    if len(x_ref.shape) == 3:
        refs, xscr = refs[:-1], refs[-1]
    if mode == "kv":
        v_ref = refs[0]
        refs = refs[3:] if aliased else refs[1:]
        xp_ref, kout_ref, vout_ref, kscr, vscr = refs

        @pl.when(pl.program_id(1) != work_step)
        def _():
            kout_ref[...] = jnp.zeros(kout_ref.shape, kout_ref.dtype)
            vout_ref[...] = jnp.zeros(vout_ref.shape, vout_ref.dtype)
    elif mode == "norm":
        xp_ref, xn_ref = refs
    else:
        xp_ref, = refs
    pl.when(pl.program_id(1) == work_step)(functools.partial(
        _fox_build_body, x_ref, cum_ref, g_ref, sel_ref, v_ref, xp_ref, xn_ref, kout_ref, vout_ref, kscr, vscr, xscr,
        normalize=normalize, mode=mode, scale=scale, negate=negate, pitch=pitch))


def _fox_build_body(x_ref, cum_ref, g_ref, sel_ref, v_ref, xp_ref, xn_ref, kout_ref, vout_ref, kscr, vscr, xscr, *,
                    normalize, mode, scale, negate, pitch):
    tm = x_ref.shape[0]
    if xscr is not None:
        def unpack(t, _):
            for g8 in range(x_ref.shape[1] // 8):
                xscr[pl.ds(g8 * 8 * pitch + t, 8, stride=pitch), :] = x_ref[t, g8 * 8:(g8 + 1) * 8, :]
            return 0
        lax.fori_loop(0, tm, unpack, 0, unroll=8)
    c = cum_ref[...] * LOG2E
    if negate:
        c = -c
    c_hi = c.astype(BF16).astype(F32)
    r1 = c - c_hi
    c_mid = r1.astype(BF16).astype(F32)
    c_lo = (r1 - c_mid).astype(BF16).astype(F32)
    parts = jnp.concatenate([c_hi, c_mid, c_lo, jnp.ones_like(c_hi)], axis=1).astype(BF16)
    extra = _dot(parts, sel_ref[...])
    g = g_ref[...]
    for h in range(c.shape[1]):
        xs = x_ref[:, h * LANES:(h + 1) * LANES] if xscr is None else xscr[h * pitch:h * pitch + tm, :]
        if normalize:
            xs = xs * lax.rsqrt(jnp.mean(xs * xs, axis=-1, keepdims=True) + NORM_EPS) * g
        if mode == "norm":
            xn_ref[:, h * LANES:(h + 1) * LANES] = xs
        if mode == "kv":
            kscr[h * pitch:h * pitch + tm, :] = xs
            vscr[h * pitch:h * pitch + tm, :] = v_ref[:, h * LANES:(h + 1) * LANES]
        xp_ref[:, 2 * h * LANES:(2 * h + 1) * LANES] = (xs * scale).astype(BF16)
        xp_ref[:, (2 * h + 1) * LANES:(2 * h + 2) * LANES] = extra[:, h * LANES:(h + 1) * LANES].astype(BF16)
    if mode == "kv":
        def relay(t, _):
            for g8 in range(c.shape[1] // 8):
                rows8 = pl.ds(g8 * 8 * pitch + t, 8, stride=pitch)
                kout_ref[t, g8 * 8:(g8 + 1) * 8, :] = kscr[rows8, :]
                vout_ref[t, g8 * 8:(g8 + 1) * 8, :] = vscr[rows8, :]
            return 0
        lax.fori_loop(0, tm, relay, 0, unroll=8)


def fox_build(x, col_block, row_off, rows, cum, gain, k_side, normalize, scale, mode="plain", kv_bufs=None,
              n_layers=1, slot=0):
    h = cum.shape[1]
    w = h * LANES
    tm = _pick_tile(rows, 128, 16)
    assert row_off % tm == 0 and h % 8 == 0
    off = row_off // tm
    pitch = _relayout_pitch(tm)
    if x.ndim == 4:
        x_spec = pl.BlockSpec((None, tm, h, LANES), lambda i, l: (slot, off + i, 0, 0))
    else:
        x_spec = pl.BlockSpec((tm, w), lambda i, l: (off + i, col_block))
    in_specs = [x_spec,
                pl.BlockSpec((tm, h), lambda i, l: (i, 0)),
                pl.BlockSpec((1, LANES), lambda i, l: (0, 0)),
                pl.BlockSpec((4 * h, w), lambda i, l: (0, 0))]
    args = [x, cum, gain.reshape(1, LANES), _fox_select_matrices(h, k_side)]
    out_specs = [pl.BlockSpec((tm, 2 * w), lambda i, l: (i, 0))]
    out_shape = [jax.ShapeDtypeStruct((rows, 2 * w), BF16)]
    aliases = {}
    scratch = []
    slot_steps, work_step = 1, 0
    if mode == "kv":
        in_specs.append(pl.BlockSpec((tm, w), lambda i, l: (off + i, col_block + 1)))
        args.append(x)
        if kv_bufs is not None:
            in_specs += [ANY, ANY]
            args += list(kv_bufs)
            aliases = {5: 1, 6: 2}
            kv_spec = pl.BlockSpec((None, tm, h, LANES), lambda i, l: (slot, i, 0, 0))
        else:
            slot_steps, work_step = n_layers, slot
            kv_spec = pl.BlockSpec((None, tm, h, LANES), lambda i, l: (l, i, 0, 0))
        out_specs += [kv_spec, kv_spec]
        out_shape += [jax.ShapeDtypeStruct((n_layers, rows, h, LANES), F32)] * 2
        scratch = [pltpu.VMEM((h * pitch, LANES), F32)] * 2
    elif mode == "norm":
        out_specs.append(pl.BlockSpec((tm, w), lambda i, l: (i, 0)))
        out_shape.append(jax.ShapeDtypeStruct((rows, w), F32))
    if x.ndim == 4:
        scratch = scratch + [pltpu.VMEM((h * pitch, LANES), F32)]
    return pl.pallas_call(
        functools.partial(_fox_build_kernel, normalize=normalize, mode=mode, aliased=bool(aliases),
                          scale=scale, negate=k_side, pitch=pitch, work_step=work_step),
        grid=(rows // tm, slot_steps),
        in_specs=in_specs,
        out_specs=out_specs,
        out_shape=out_shape,
        scratch_shapes=scratch,
        input_output_aliases=aliases,
        compiler_params=_params(("parallel", "arbitrary")),
        name="fox_build",
    )(*args)


def _fox_attn_kernel(q_ref, k_ref, v_ref, z_ref, ybuf_ref, o_ref, vt_scr, s_a, s_b, *, tq, unroll):
    i = pl.program_id(1)

    @pl.when(i == 0)
    def _():
        vt_scr[...] = v_ref[...].astype(BF16).T

    q = q_ref[...]
    bufs = (s_a, s_b)

    def logits(j):
        return _dot_nt(k_ref[pl.ds(pl.multiple_of(j * tq, tq), tq), :], q)

    def step(u, j, carry, diagonal):
        m, l, acc = carry
        st = bufs[u % 2][...]
        if not diagonal:
            st_next = logits(j + 1)
        else:
            kidx = lax.broadcasted_iota(jnp.int32, (tq, tq), 0)
            qidx = lax.broadcasted_iota(jnp.int32, (tq, tq), 1)
            st = jnp.where(kidx <= qidx, st, NEG)
        m_new = jnp.maximum(m, jnp.max(st, axis=0, keepdims=True))
        alpha = jnp.exp2(m - m_new)
        p = jnp.exp2(st - m_new)
        l = alpha * l + jnp.sum(p, axis=0, keepdims=True)
        acc = alpha * acc + _dot(vt_scr[:, pl.ds(pl.multiple_of(j * tq, tq), tq)], p.astype(BF16))
        if not diagonal:
            bufs[(u + 1) % 2][...] = st_next
        return m_new, l, acc

    def trip(jj, carry):
        for u in range(unroll):
            carry = step(u, unroll * jj + u, carry, False)
        return carry

    def finish(rem, j0, carry):
        for u in range(rem):
            carry = step(u, j0 + u, carry, False)
        _, l, acc = step(rem, j0 + rem, carry, True)
        z = z_ref[...]
        o_ref[...] = ((acc / l).T * (z * _sigmoid(z))).astype(o_ref.dtype)

    s_a[...] = logits(0)
    carry = (jnp.full((1, tq), NEG, F32), jnp.zeros((1, tq), F32), jnp.zeros((LANES, tq), F32))
    trips = i // unroll
    carry = lax.fori_loop(0, trips, trip, carry)
    for rem in range(unroll):
        pl.when(i - trips * unroll == rem)(functools.partial(finish, rem, trips * unroll, carry))


def fox_attention_prompt(qp, kp, proj, nseq, t, nheads, ybuf):
    tq = _pick_tile(t, FOX_TQ, LANES)
    assert FOX_UNROLL % 2 == 0
    nq = t // tq
    blk = lambda g, i: (g // nheads * nq + i, g % nheads)
    return pl.pallas_call(
        functools.partial(_fox_attn_kernel, tq=tq, unroll=FOX_UNROLL),
        grid=(nseq * nheads, nq),
        in_specs=[pl.BlockSpec((tq, 2 * LANES), blk),
                  pl.BlockSpec((t, 2 * LANES), lambda g, i: (g // nheads, g % nheads)),
                  pl.BlockSpec((t, LANES), lambda g, i: (g // nheads, 2 * nheads + g % nheads)),
                  pl.BlockSpec((tq, LANES), lambda g, i: (g // nheads * nq + i, 3 * nheads + g % nheads)),
                  ANY],
        out_specs=pl.BlockSpec((tq, LANES), blk),
        out_shape=jax.ShapeDtypeStruct(ybuf.shape, ybuf.dtype),
        scratch_shapes=[pltpu.VMEM((LANES, t), BF16), pltpu.VMEM((tq, tq), F32), pltpu.VMEM((tq, tq), F32)],
        input_output_aliases={4: 0},
        compiler_params=_params(("parallel", "arbitrary")),
        name="fox_attention",
    )(qp, kp, proj, proj, ybuf)


def _fox_attn_sample_kernel(q_ref, kc_ref, kn_ref, vc_ref, vn_ref, z_ref, ybuf_ref, o_ref, vscr, *, nh, pitch):
    ts = q_ref.shape[0]
    p = vc_ref.shape[0]

    def unpack(t, _):
        vscr[pl.ds(t, nh, stride=pitch), :] = vc_ref[t]
        return 0
    lax.fori_loop(0, p, unpack, 0, unroll=8)
    r = lax.broadcasted_iota(jnp.int32, (ts, ts), 0)
    c = lax.broadcasted_iota(jnp.int32, (ts, ts), 1)
    for u in range(nh):
        q = q_ref[:, 2 * u * LANES:2 * (u + 1) * LANES]
        s_c = _dot_nt(q, kc_ref[:, 2 * u * LANES:2 * (u + 1) * LANES])
        s_n = jnp.where(c <= r, _dot_nt(q, kn_ref[:, 2 * u * LANES:2 * (u + 1) * LANES]), NEG)
        m = jnp.maximum(jnp.max(s_c, axis=-1, keepdims=True), jnp.max(s_n, axis=-1, keepdims=True))
        p_c = jnp.exp2(s_c - m)
        p_n = jnp.exp2(s_n - m)
        l = jnp.sum(p_c, axis=-1, keepdims=True) + jnp.sum(p_n, axis=-1, keepdims=True)
        acc = (_dot(p_c.astype(BF16), vscr[u * pitch:u * pitch + p, :].astype(BF16))
               + _dot(p_n.astype(BF16), vn_ref[:, u * LANES:(u + 1) * LANES].astype(BF16)))
        z = z_ref[:, u * LANES:(u + 1) * LANES]
        o_ref[:, u * LANES:(u + 1) * LANES] = (acc / l * (z * _sigmoid(z))).astype(o_ref.dtype)


def fox_attention_sample(qp, kp_c, kp_n, cache_v, slot, proj, row_off, nseq, ts, p, nheads, ybuf):
    nh = FOX_HEADS_PER_STEP
    ng = nheads // nh
    assert row_off % ts == 0 and nheads % nh == 0
    off = row_off // ts
    pitch = _relayout_pitch(p)
    return pl.pallas_call(
        functools.partial(_fox_attn_sample_kernel, nh=nh, pitch=pitch),
        grid=(nseq, ng),
        in_specs=[pl.BlockSpec((ts, 2 * nh * LANES), lambda s, g: (s, g)),
                  pl.BlockSpec((p, 2 * nh * LANES), lambda s, g: (s, g)),
                  pl.BlockSpec((ts, 2 * nh * LANES), lambda s, g: (s, g)),
                  pl.BlockSpec((None, p, nh, LANES), lambda s, g: (slot, s, g, 0)),
                  pl.BlockSpec((ts, nh * LANES), lambda s, g: (off + s, 2 * ng + g)),
                  pl.BlockSpec((ts, nh * LANES), lambda s, g: (off + s, 3 * ng + g)),
                  ANY],
        out_specs=pl.BlockSpec((ts, nh * LANES), lambda s, g: (off + s, g)),
        out_shape=jax.ShapeDtypeStruct(ybuf.shape, ybuf.dtype),
        scratch_shapes=[pltpu.VMEM((nh * pitch, LANES), F32)],
        input_output_aliases={6: 0},
        compiler_params=_params(("parallel", "arbitrary")),
        name="fox_attention_sample",
    )(qp, kp_c, kp_n, cache_v, proj, proj, ybuf)


def fox_layer(h, mp_rows, b, t, bs, ts, cache_k, cache_v, cache_lf, w_in, slot, b_f, q_gain, k_gain, kv_bufs):
    nheads = b_f.shape[0]
    w = nheads * LANES
    n_layers, _, p = cache_k.shape[:3]
    qscale = float(LANES) ** -0.5 * LOG2E
    wt = jnp.swapaxes(w_in, 1, 2)
    proj = matmul(h, wt, 4 * w, slot, True)
    fgate = matmul(h, wt[slot, 4 * w:, :], nheads, None, True)
    ybuf = jnp.zeros((h.shape[0], w), BF16)
    lf_p, cum_p = logf_cumsum(fgate, 0, b, t, b_f, jnp.zeros((b, 1, nheads), F32), True)
    qp_p, = fox_build(proj, 0, 0, mp_rows, cum_p, q_gain, False, True, qscale)
    kp_p, kbuf, vbuf = fox_build(proj, 1, 0, mp_rows, cum_p, k_gain, True, True, 1.0, "kv", kv_bufs, n_layers, slot)
    ybuf = fox_attention_prompt(qp_p, kp_p, proj, b, t, nheads, ybuf)
    ms_rows = bs * ts
    _, cum_c = logf_cumsum(cache_lf.reshape(bs * p, nheads), 0, bs, p, b_f, jnp.zeros((bs, 1, nheads), F32), False)
    init_s = cum_c.reshape(bs, p, nheads)[:, p - 1:, :]
    lf_s, cum_s = logf_cumsum(fgate, mp_rows, bs, ts, b_f, init_s, True)
    qp_s, = fox_build(proj, 0, mp_rows, ms_rows, cum_s, q_gain, False, True, qscale)
    kp_s, kn_s = fox_build(proj, 1, mp_rows, ms_rows, cum_s, k_gain, True, True, 1.0, "norm")
    by_head = (n_layers, bs * p, nheads, LANES)
    kp_c, = fox_build(cache_k.reshape(by_head), 0, 0, bs * p, cum_c, k_gain, True, False, 1.0, slot=slot)
    ybuf = fox_attention_sample(qp_s, kp_c, kp_s, cache_v.reshape(by_head), slot, proj, mp_rows, bs, ts, p,
                                nheads, ybuf)
    v_s = proj[mp_rows:, 2 * w:3 * w]
    hd = LANES
    outs = (lf_p.reshape(b, t, nheads), kn_s.reshape(bs, ts, nheads, hd), v_s.reshape(bs, ts, nheads, hd),
            lf_s.reshape(bs, ts, nheads))
    return ybuf, (kbuf, vbuf), outs


def _mlstm_kernel(q_ref, k_ref, v_ref, o_ref, z_ref, gate_ref, bi_ref, bf_ref, gain_ref, expand_ref,
                  ct0_ref, n0_ref, m0_ref, ybuf_ref, y_ref, ct_ref, n_ref, m_ref, *, nheads, dk, dv):
    @pl.when(pl.program_id(1) == 0)
    def _():
        ct_ref[...] = ct0_ref[...]
        n_ref[...] = n0_ref[...]
        m_ref[...] = m0_ref[...]

    ln = q_ref.shape[0]
    pack = MXU_DIM // ln
    ngrp = nheads // pack
    rr = pack * ln
    lsh = ln.bit_length() - 1
    gw = pack * dv
    gates = gate_ref[...]
    log_i = gates[:, :nheads] + bi_ref[...]
    log_f = _log_sigmoid(gates[:, nheads:] + bf_ref[...])
    rl = lax.broadcasted_iota(jnp.int32, (ln, ln), 0)
    cl = lax.broadcasted_iota(jnp.int32, (ln, ln), 1)
    b_all = jnp.dot((cl <= rl).astype(F32), log_f, precision=HIGHEST, preferred_element_type=F32)
    m_row = m_ref[0, :, 0:nheads]
    b_last = b_all[ln - 1:ln, :]
    g_all = b_last - b_all + log_i
    m_new = jnp.maximum(b_last + m_row, jnp.max(g_all, axis=0, keepdims=True))
    wk_all = jnp.exp(g_all - m_new)
    carry_w = jnp.exp(b_last + m_row - m_new)
    carry_wide = jnp.dot(carry_w, expand_ref[...], precision=HIGHEST, preferred_element_type=F32)
    rh = lax.broadcasted_iota(jnp.int32, (nheads, nheads), 0)
    ch = lax.broadcasted_iota(jnp.int32, (nheads, nheads), 1)
    carry_col = jnp.sum(jnp.where(rh == ch, carry_w, 0.0), axis=1, keepdims=True)

    def stack_cols(a):
        return jnp.stack([jnp.concatenate([a[:, g * pack + p:g * pack + p + 1] for p in range(pack)], axis=0)
                          for g in range(ngrp)], axis=0)

    def stack_heads(ref, width):
        return jnp.stack([jnp.concatenate([ref[:, (g * pack + p) * width:(g * pack + p + 1) * width]
                                           for p in range(pack)], axis=0) for g in range(ngrp)], axis=0)

    bcol = stack_cols(b_all)
    licol = stack_cols(log_i)
    mprev = stack_cols(jnp.broadcast_to(m_row, (ln, nheads)))
    wkcol = stack_cols(wk_all)
    r = lax.broadcasted_iota(jnp.int32, (rr, rr), 0)
    c = lax.broadcasted_iota(jnp.int32, (rr, rr), 1)
    incl = ((r >> lsh) == (c >> lsh)) & (c <= r)
    brow = jnp.sum(jnp.where(r == c, bcol, 0.0), axis=1, keepdims=True)
    lirow = jnp.sum(jnp.where(r == c, licol, 0.0), axis=1, keepdims=True)
    d = jnp.where(incl, bcol - brow + lirow, -jnp.inf)
    inter = bcol + mprev
    m_t = jnp.maximum(inter, jnp.max(d, axis=-1, keepdims=True))
    qf = stack_heads(q_ref, dk)
    kf = stack_heads(k_ref, dk) * (float(dk) ** -0.5)
    vf = stack_heads(v_ref, dv)
    qb, kb, vb = qf.astype(BF16), kf.astype(BF16), vf.astype(BF16)
    wmat = jnp.exp(d - m_t) * _bmm_nt(qb, kb)
    inter_w = jnp.exp(inter - m_t)
    ctb = ct_ref[0].astype(BF16)
    rowblk = lax.broadcasted_iota(jnp.int32, (rr, 1), 0) >> lsh

    def own_block(wide):
        out = jnp.where(rowblk == 0, wide[:, :dv], 0.0)
        for p in range(1, pack):
            out = out + jnp.where(rowblk == p, wide[:, p * dv:(p + 1) * dv], 0.0)
        return out

    qc = jnp.stack([own_block(_dot(qb[g], ctb[:, g * gw:(g + 1) * gw])) for g in range(ngrp)], axis=0)
    num = _bmm(wmat.astype(BF16), vb) + inter_w * qc
    nst = jnp.stack([jnp.concatenate([jnp.broadcast_to(n_ref[0, g * pack + p:g * pack + p + 1, :], (ln, dk))
                                      for p in range(pack)], axis=0) for g in range(ngrp)], axis=0)
    den = jnp.sum(wmat, axis=-1, keepdims=True) + inter_w * jnp.sum(qf * nst, axis=-1, keepdims=True)
    hid = num / jnp.maximum(jnp.abs(den), jnp.exp(-m_t))
    own = ((lax.broadcasted_iota(jnp.int32, (rr, gw), 0) >> lsh)
           == (lax.broadcasted_iota(jnp.int32, (rr, gw), 1) >> (dv.bit_length() - 1)))
    blocksum = ((lax.broadcasted_iota(jnp.int32, (pack, rr), 1) >> lsh)
                == lax.broadcasted_iota(jnp.int32, (pack, rr), 0)).astype(F32)
    for g in range(ngrp):
        wv = (wkcol[g] * vf[g]).astype(BF16)
        vwbd = jnp.where(own, jnp.concatenate([wv] * pack, axis=1), jnp.zeros((), BF16))
        cols = slice(g * gw, (g + 1) * gw)
        ct_ref[0, :, cols] = carry_wide[:, cols] * ct_ref[0, :, cols] + _dot_tn(kb[g], vwbd)
        heads = slice(g * pack, (g + 1) * pack)
        ksum = jnp.dot(blocksum, wkcol[g] * kf[g], precision=HIGHEST, preferred_element_type=F32)
        n_ref[0, heads, :] = carry_col[heads, :] * n_ref[0, heads, :] + ksum
        for p in range(pack):
            h = g * pack + p
            hh = hid[g, p * ln:(p + 1) * ln, :]
            hn = hh * lax.rsqrt(jnp.mean(hh * hh, axis=-1, keepdims=True) + NORM_EPS)
            hn = hn * gain_ref[:, h * dv:(h + 1) * dv]
            og = o_ref[:, h * dv:(h + 1) * dv]
            zg = z_ref[:, h * dv:(h + 1) * dv]
            y_ref[:, h * dv:(h + 1) * dv] = (hn * _sigmoid(og) * (zg * _sigmoid(zg))).astype(y_ref.dtype)
    m_ref[0, :, 0:nheads] = m_new


def mlstm_scan(proj, gates, row_off, nseq, t, ln, c0, n0, m0, b_i, b_f, gain, ybuf):
    nheads, dv, dk = c0.shape[1:]
    nc = t // ln
    assert row_off % ln == 0
    off = row_off // ln
    qk_w, v_w = nheads * dk, nheads * dv
    assert v_w == 2 * qk_w

    def rows(width, cb):
        return pl.BlockSpec((ln, width), lambda s, i: (off + s * nc + i, cb))

    assert MXU_DIM % ln == 0 and nheads % (MXU_DIM // ln) == 0 and nheads <= LANES
    ct0 = jnp.transpose(c0, (0, 3, 1, 2)).reshape(nseq, dk, v_w)
    m0p = jnp.pad(m0[:, None, :], ((0, 0), (0, 0), (0, LANES - nheads)))
    expand = jnp.asarray(np.kron(np.eye(nheads, dtype=np.float32), np.ones((1, dv), np.float32)))
    state = [pl.BlockSpec((1, dk, v_w), lambda s, i: (s, 0, 0)),
             pl.BlockSpec((1, nheads, dk), lambda s, i: (s, 0, 0)),
             pl.BlockSpec((1, 1, LANES), lambda s, i: (s, 0, 0))]
    y, ct, n, m = pl.pallas_call(
        functools.partial(_mlstm_kernel, nheads=nheads, dk=dk, dv=dv),
        grid=(nseq, nc),
        in_specs=[rows(qk_w, 0), rows(qk_w, 1), rows(v_w, 1), rows(v_w, 2), rows(v_w, 3),
                  rows(2 * nheads, 0),
                  pl.BlockSpec((1, nheads), lambda s, i: (0, 0)),
                  pl.BlockSpec((1, nheads), lambda s, i: (0, 0)),
                  pl.BlockSpec((1, v_w), lambda s, i: (0, 0)),
                  pl.BlockSpec((nheads, v_w), lambda s, i: (0, 0))] + state + [ANY],
        out_specs=[rows(v_w, 0)] + state,
        out_shape=[jax.ShapeDtypeStruct(ybuf.shape, ybuf.dtype),
                   jax.ShapeDtypeStruct((nseq, dk, v_w), F32),
                   jax.ShapeDtypeStruct((nseq, nheads, dk), F32),
                   jax.ShapeDtypeStruct((nseq, 1, LANES), F32)],
        input_output_aliases={13: 0},
        compiler_params=_params(("parallel", "arbitrary")),
        name="mlstm_scan",
    )(proj, proj, proj, proj, proj, gates, b_i.reshape(1, nheads), b_f.reshape(1, nheads),
      gain.reshape(1, v_w), expand, ct0, n0, m0p, ybuf)
    c = jnp.transpose(ct.reshape(nseq, dk, nheads, dv), (0, 2, 3, 1))
    return y, c, n, m[:, 0, :nheads]


def mlstm_layer(h, mp_rows, b, t, bs, ts, c0, n0, m0, w_in, slot, b_i, b_f, gain):
    nheads, dv, dk = c0.shape[1:]
    main = 2 * nheads * dk + 3 * nheads * dv
    wt = jnp.swapaxes(w_in, 1, 2)
    proj = matmul(h, wt, main, slot, True)
    gates = matmul(h, wt[slot, main:, :], 2 * nheads, None, True)
    ybuf = jnp.zeros((h.shape[0], nheads * dv), BF16)
    zc = jnp.zeros((b, nheads, dv, dk), F32)
    ybuf, c_p, n_p, m_p = mlstm_scan(proj, gates, 0, b, t, min(CHUNK, t), zc, zc[:, :, 0, :], zc[:, :, 0, 0],
                                     b_i, b_f, gain, ybuf)
    ybuf, c_s, n_s, m_s = mlstm_scan(proj, gates, mp_rows, bs, ts, ts, c0, n0, m0, b_i, b_f, gain, ybuf)
    return ybuf, (c_p, n_p, m_p, c_s, n_s, m_s)


def _gdn_conv_kernel(x_ref, halo_ref, init_ref, w_ref, sc_ref, o_ref, *, l2norm):
    tm = x_ref.shape[0]
    prev = jnp.where(pl.program_id(1) == 0, init_ref[0], halo_ref[...])
    xc = jnp.concatenate([prev, x_ref[...]], axis=0)
    w = w_ref[...]
    acc = w[3:4] * xc[8:8 + tm] + w[2:3] * xc[7:7 + tm] + w[1:2] * xc[6:6 + tm] + w[0:1] * xc[5:5 + tm]
    act = acc * _sigmoid(acc)
    if l2norm:
        for g in range(act.shape[1] // LANES):
            a = act[:, g * LANES:(g + 1) * LANES]
            a = a * lax.rsqrt(jnp.sum(a * a, axis=-1, keepdims=True) + NORM_EPS)
            o_ref[:, g * LANES:(g + 1) * LANES] = (a * sc_ref[:, g * LANES:(g + 1) * LANES]).astype(o_ref.dtype)
    else:
        o_ref[...] = act.astype(o_ref.dtype)


def gdn_conv(proj, row_off, nseq, t, ch_off, ch, init, conv_w, slot, scale, l2norm, out_dtype):
    tm = _pick_tile(t, 512, 8)
    tc = _pick_tile(ch, 1024, LANES)
    nt = t // tm
    assert row_off % tm == 0 and ch_off % tc == 0 and tm % 8 == 0
    off, coff, sub = row_off // tm, ch_off // tc, tm // 8
    return pl.pallas_call(
        functools.partial(_gdn_conv_kernel, l2norm=l2norm),
        grid=(nseq, nt, ch // tc),
        in_specs=[pl.BlockSpec((tm, tc), lambda s, i, j: (off + s * nt + i, coff + j)),
                  pl.BlockSpec((8, tc), lambda s, i, j: (jnp.maximum((off + s * nt + i) * sub - 1, 0), coff + j)),
                  pl.BlockSpec((1, 8, tc), lambda s, i, j: (s, 0, coff + j)),
                  pl.BlockSpec((None, conv_w.shape[1], tc), lambda s, i, j: (slot, 0, coff + j)),
                  pl.BlockSpec((1, tc), lambda s, i, j: (0, j))],
        out_specs=pl.BlockSpec((tm, tc), lambda s, i, j: (s * nt + i, j)),
        out_shape=jax.ShapeDtypeStruct((nseq * t, ch), out_dtype),
        compiler_params=_params(("parallel", "arbitrary", "arbitrary")),
        name="gdn_conv",
    )(proj, proj, init, conv_w, scale)


def _unit_lower_inverse(a, r, c, levels):
    x = jnp.where(r == c, 1.0, 0.0) - jnp.where(((r >> 1) == (c >> 1)) & (c < r), a, 0.0)
    for sh in range(1, levels):
        off = jnp.where(((r >> (sh + 1)) == (c >> (sh + 1))) & ((r >> sh) != (c >> sh)) & (c < r), a, 0.0)
        xb = x.astype(BF16)
        x = x - _bmm(_bmm(xb, off.astype(BF16)).astype(BF16), xb)
    return x


def _gdn_kernel(qk_ref, v_ref, z_ref, gate_ref, alog_ref, dt_ref, gain_ref, expand_ref, ws0_ref, ybuf_ref,
                y_ref, ws_ref, *, kheads, vheads):
    @pl.when(pl.program_id(1) == 0)
    def _():
        ws_ref[...] = ws0_ref[...]

    ln = v_ref.shape[0]
    hd = LANES
    pack = MXU_DIM // ln
    ngrp = vheads // pack
    rr = pack * ln
    lsh = ln.bit_length() - 1
    rep = vheads // kheads
    gates = gate_ref[...]
    g_all = -jnp.exp(alog_ref[...]) * _softplus(gates[:, :vheads] + dt_ref[...])
    beta_all = _sigmoid(gates[:, vheads:])
    rl = lax.broadcasted_iota(jnp.int32, (ln, ln), 0)
    cl = lax.broadcasted_iota(jnp.int32, (ln, ln), 1)
    gsum_all = jnp.dot((cl <= rl).astype(F32), g_all, precision=HIGHEST, preferred_element_type=F32)
    glast_row = gsum_all[ln - 1:ln, :]
    decay_row = jnp.exp(jnp.dot(glast_row, expand_ref[...], precision=HIGHEST, preferred_element_type=F32))

    def stack_cols(a):
        return jnp.stack([jnp.concatenate([a[:, g * pack + p:g * pack + p + 1] for p in range(pack)], axis=0)
                          for g in range(ngrp)], axis=0)

    def stack_heads(ref, head_of):
        return jnp.stack([jnp.concatenate([ref[:, head_of(g * pack + p) * hd:(head_of(g * pack + p) + 1) * hd]
                                           for p in range(pack)], axis=0) for g in range(ngrp)], axis=0)

    gcol = stack_cols(gsum_all)
    bt = stack_cols(beta_all)
    glast = stack_cols(jnp.broadcast_to(glast_row, (ln, vheads)))
    r = lax.broadcasted_iota(jnp.int32, (rr, rr), 0)
    c = lax.broadcasted_iota(jnp.int32, (rr, rr), 1)
    blk = (r >> lsh) == (c >> lsh)
    incl = blk & (c <= r)
    strict = blk & (c < r)
    grow = jnp.sum(jnp.where(r == c, gcol, 0.0), axis=1, keepdims=True)
    dincl = jnp.where(incl, jnp.exp(jnp.where(incl, gcol - grow, 0.0)), 0.0)
    qst = stack_heads(qk_ref, lambda h: h // rep)
    kst = stack_heads(qk_ref, lambda h: kheads + h // rep)
    vst = stack_heads(v_ref, lambda h: h)
    kk = _bmm_nt(kst, kst)
    qk = _bmm_nt(qst, kst)
    x = _unit_lower_inverse(jnp.where(strict, bt * dincl * kk, 0.0), r, c, lsh)
    eg = jnp.exp(gcol)
    wsb = ws_ref[0].astype(BF16)
    gw = pack * hd
    rowblk = lax.broadcasted_iota(jnp.int32, (rr, 1), 0) >> lsh

    def own_block(wide):
        out = jnp.where(rowblk == 0, wide[:, :hd], 0.0)
        for p in range(1, pack):
            out = out + jnp.where(rowblk == p, wide[:, p * hd:(p + 1) * hd], 0.0)
        return out

    ksd = jnp.stack([own_block(_dot(kst[g], wsb[:, g * gw:(g + 1) * gw])) for g in range(ngrp)], axis=0)
    qsd = jnp.stack([own_block(_dot(qst[g], wsb[:, g * gw:(g + 1) * gw])) for g in range(ngrp)], axis=0)
    rhs = bt * (vst - eg * ksd)
    u = _bmm(x.astype(BF16), rhs.astype(BF16))
    o = eg * qsd + _bmm((dincl * qk).astype(BF16), u.astype(BF16))
    uw = (jnp.exp(glast - gcol) * u).astype(BF16)
    own = ((lax.broadcasted_iota(jnp.int32, (rr, gw), 0) >> lsh)
           == (lax.broadcasted_iota(jnp.int32, (rr, gw), 1) >> (hd.bit_length() - 1)))
    gain = gain_ref[...]
    for g in range(ngrp):
        uwbd = jnp.where(own, jnp.concatenate([uw[g]] * pack, axis=1), jnp.zeros((), BF16))
        cols = slice(g * gw, (g + 1) * gw)
        ws_ref[0, :, cols] = decay_row[:, cols] * ws_ref[0, :, cols] + _dot_tn(kst[g], uwbd)
        for p in range(pack):
            h = g * pack + p
            oh = o[g, p * ln:(p + 1) * ln, :]
            on = oh * lax.rsqrt(jnp.mean(oh * oh, axis=-1, keepdims=True) + NORM_EPS) * gain
            zg = z_ref[:, h * hd:(h + 1) * hd]
            y_ref[:, h * hd:(h + 1) * hd] = (on * (zg * _sigmoid(zg))).astype(y_ref.dtype)


def gdn_scan(qk, v, proj, z_col_block, gates, row_off, nseq, t, ln, s0, a_log, dt_bias, gain, ybuf):
    vheads, dv, dk = s0.shape[1:]
    v_w = vheads * LANES
    kheads = qk.shape[1] // (2 * LANES)
    nc = t // ln
    assert row_off % ln == 0 and MXU_DIM % ln == 0 and vheads % (MXU_DIM // ln) == 0 and dv == LANES and dk == LANES
    off = row_off // ln
    ws0 = jnp.transpose(s0, (0, 3, 1, 2)).reshape(nseq, dk, v_w)
    expand = jnp.asarray(np.kron(np.eye(vheads, dtype=np.float32), np.ones((1, LANES), np.float32)))
    local = lambda width: pl.BlockSpec((ln, width), lambda s, i: (s * nc + i, 0))
    state = pl.BlockSpec((1, dk, v_w), lambda s, i: (s, 0, 0))
    vec = lambda width: pl.BlockSpec((1, width), lambda s, i: (0, 0))
    ybuf, ws = pl.pallas_call(
        functools.partial(_gdn_kernel, kheads=kheads, vheads=vheads),
        grid=(nseq, nc),
        in_specs=[local(qk.shape[1]), local(v_w),
                  pl.BlockSpec((ln, v_w), lambda s, i: (off + s * nc + i, z_col_block)),
                  pl.BlockSpec((ln, 2 * vheads), lambda s, i: (off + s * nc + i, 0)),
                  vec(vheads), vec(vheads), vec(LANES),
                  pl.BlockSpec((vheads, v_w), lambda s, i: (0, 0)), state, ANY],
        out_specs=[pl.BlockSpec((ln, v_w), lambda s, i: (off + s * nc + i, 0)), state],
        out_shape=[jax.ShapeDtypeStruct(ybuf.shape, ybuf.dtype),
                   jax.ShapeDtypeStruct((nseq, dk, v_w), F32)],
        input_output_aliases={9: 0},
        compiler_params=_params(("parallel", "arbitrary")),
        name="gdn_scan",
    )(qk, v, proj, gates, a_log.reshape(1, vheads), dt_bias.reshape(1, vheads), gain.reshape(1, LANES),
      expand, ws0, ybuf)
    return ybuf, jnp.transpose(ws.reshape(nseq, dk, vheads, dv), (0, 2, 3, 1))


def gdn_layer(h, mp_rows, b, t, bs, ts, s0, conv0, w_in, slot, conv_w, a_log, dt_bias, gain):
    vheads = s0.shape[1]
    v_w = vheads * LANES
    taps, ch = conv_w.shape[1:]
    qk_w = ch - v_w
    wt = jnp.swapaxes(w_in, 1, 2)
    proj = matmul(h, wt, ch + v_w, slot, True)
    gates = matmul(h, wt[slot, ch + v_w:, :], 2 * vheads, None, True)
    scale = jnp.concatenate([jnp.full((1, qk_w // 2), float(LANES) ** -0.5, F32), jnp.ones((1, qk_w // 2), F32)], axis=1)
    ones = jnp.ones((1, v_w), F32)
    ybuf = jnp.zeros((h.shape[0], v_w), BF16)
    outs = []
    for row_off, nseq, tt, ln, init_rows, st0 in (
            (0, b, t, min(CHUNK, t), jnp.zeros((b, taps - 1, ch), F32), jnp.zeros((b,) + s0.shape[1:], F32)),
            (mp_rows, bs, ts, ts, conv0, s0)):
        init = jnp.pad(init_rows, ((0, 0), (8 - (taps - 1), 0), (0, 0)))
        qk = gdn_conv(proj, row_off, nseq, tt, 0, qk_w, init, conv_w, slot, scale, True, BF16)
        v = gdn_conv(proj, row_off, nseq, tt, qk_w, v_w, init, conv_w, slot, ones, False, F32)
        ybuf, s_new = gdn_scan(qk, v, proj, ch // v_w, gates, row_off, nseq, tt, ln, st0, a_log[slot], dt_bias[slot],
                               gain[slot], ybuf)
        conv_state = jnp.stack([proj[row_off + (s + 1) * tt - (taps - 1):row_off + (s + 1) * tt, :ch]
                                for s in range(nseq)])
        outs += [s_new, conv_state]
    return ybuf, tuple(outs)


def kernel(x_prompt, x_sample, cache_fox_k, cache_fox_v, cache_fox_logf, state_mlstm_c, state_mlstm_n, state_mlstm_m, state_gdn_s, state_gdn_conv, pre_norm, post_norm, fox_w_in, fox_b_f, fox_q_norm, fox_k_norm, fox_w_out, mlstm_w_in, mlstm_b_i, mlstm_b_f, mlstm_h_norm, mlstm_w_out, gdn_w_in, gdn_conv_w, gdn_a_log, gdn_dt_bias, gdn_o_norm, gdn_w_out):
    b, t, d = x_prompt.shape
    bs, ts, _ = x_sample.shape
    mp_rows = b * t
    n_fox = fox_w_in.shape[0]
    x = (x_prompt.reshape(mp_rows, d), x_sample.reshape(bs * ts, d))
    kv_bufs = None
    fox, ml, gd = [], [], []
    depth = pre_norm.shape[0]
    h = rmsnorm_cast(x[0], x[1], pre_norm[0])
    for layer in range(depth):
        kind, j = layer % 3, layer // 3
        if kind == 0:
            y, kv_bufs, outs = fox_layer(h, mp_rows, b, t, bs, ts, cache_fox_k, cache_fox_v, cache_fox_logf[j],
                                         fox_w_in, j, fox_b_f[j], fox_q_norm[j], fox_k_norm[j], kv_bufs)
            fox.append(outs)
            w_out = fox_w_out
        elif kind == 1:
            y, outs = mlstm_layer(h, mp_rows, b, t, bs, ts, state_mlstm_c[j], state_mlstm_n[j], state_mlstm_m[j],
                                  mlstm_w_in, j, mlstm_b_i[j], mlstm_b_f[j], mlstm_h_norm[j])
            ml.append(outs)
            w_out = mlstm_w_out
        else:
            y, outs = gdn_layer(h, mp_rows, b, t, bs, ts, state_gdn_s[j], state_gdn_conv[j], gdn_w_in, j,
                                gdn_conv_w, gdn_a_log, gdn_dt_bias, gdn_o_norm)
            gd.append(outs)
            w_out = gdn_w_out
        mixed = matmul(y, w_out, d, layer=j)
        if layer + 1 < depth:
            x, h = postnorm_residual(mixed, x, post_norm[layer], mp_rows, g_next=pre_norm[layer + 1])
        else:
            x = postnorm_residual(mixed, x, post_norm[layer], mp_rows, pair_out=True)
    stack = lambda group, idx: jnp.stack([o[idx] for o in group])
    nh = fox_b_f.shape[1]
    kv_shape = (n_fox, b, t, nh, LANES)
    return ((x[0].reshape(b, t, d), x[1].reshape(bs, ts, d),
             kv_bufs[0].reshape(kv_shape), kv_bufs[1].reshape(kv_shape))
            + tuple(stack(fox, i) for i in range(4))
            + tuple(stack(ml, i) for i in range(6))
            + tuple(stack(gd, i) for i in range(4)))
```

```python
import functools
import math

import numpy as np
import jax
import jax.numpy as jnp
from jax import lax
from jax.experimental import pallas as pl
from jax.experimental.pallas import tpu as pltpu

F32 = jnp.float32
BF16 = jnp.bfloat16
NORM_EPS = 1e-6
CHUNK = 64
LANES = 128
MXU_DIM = 256
NEG = -1e30
VMEM_LIMIT = 56 * 1024 * 1024
FOX_TQ = 512
FOX_UNROLL = 4
FOX_HEADS_PER_STEP = 8
LOG2E = math.log2(math.e)
HIGHEST = lax.Precision.HIGHEST
ANY = pl.BlockSpec(memory_space=pl.ANY)


def _params(sem, vmem=VMEM_LIMIT):
    return pltpu.CompilerParams(dimension_semantics=sem, vmem_limit_bytes=vmem)


def _pick_tile(n, target, mult):
    best = None
    for t in range(mult, min(n, target) + 1, mult):
        if n % t == 0:
            best = t
    assert best is not None, (n, target, mult)
    return best


def _relayout_pitch(rows):
    assert rows % 8 == 0
    pitch = rows + 8
    return pitch if (pitch // 8) % 2 else pitch + 8


def _sigmoid(x):
    return 0.5 * jnp.tanh(0.5 * x) + 0.5


def _log_sigmoid(x):
    return jnp.minimum(x, 0.0) - jnp.log(1.0 + jnp.exp(-jnp.abs(x)))


def _softplus(x):
    return jnp.maximum(x, 0.0) + jnp.log(1.0 + jnp.exp(-jnp.abs(x)))


def _dot(a, b):
    return jnp.dot(a, b, preferred_element_type=F32)


def _dot_nt(a, b):
    return lax.dot_general(a, b, (((1,), (1,)), ((), ())), preferred_element_type=F32)


def _dot_tn(a, b):
    return lax.dot_general(a, b, (((0,), (0,)), ((), ())), preferred_element_type=F32)


def _bmm(a, b):
    return lax.dot_general(a, b, (((2,), (1,)), ((0,), (0,))), preferred_element_type=F32)


def _bmm_nt(a, b):
    return lax.dot_general(a, b, (((2,), (2,)), ((0,), (0,))), preferred_element_type=F32)


def _pair_specs(tm, d, npb):
    return [pl.BlockSpec((tm, d), lambda i: (jnp.minimum(i, npb - 1), 0)),
            pl.BlockSpec((tm, d), lambda i: (jnp.maximum(i - npb, 0), 0))]


def _rmsnorm_cast_kernel(xp_ref, xs_ref, g_ref, o_ref, *, npb):
    x = jnp.where(pl.program_id(0) < npb, xp_ref[...], xs_ref[...])
    ms = jnp.mean(x * x, axis=-1, keepdims=True)
    o_ref[...] = (x * lax.rsqrt(ms + NORM_EPS) * g_ref[...]).astype(o_ref.dtype)


def rmsnorm_cast(xp, xs, g):
    (mp, d), ms = xp.shape, xs.shape[0]
    tm = _pick_tile(math.gcd(mp, ms), 256, 16)
    npb = mp // tm
    return pl.pallas_call(
        functools.partial(_rmsnorm_cast_kernel, npb=npb),
        grid=((mp + ms) // tm,),
        in_specs=_pair_specs(tm, d, npb) + [pl.BlockSpec((1, d), lambda i: (0, 0))],
        out_specs=pl.BlockSpec((tm, d), lambda i: (i, 0)),
        out_shape=jax.ShapeDtypeStruct((mp + ms, d), BF16),
        compiler_params=_params(("arbitrary",)),
        name="rmsnorm_cast",
    )(xp, xs, g.reshape(1, d))


def _postnorm_residual_kernel(*refs, npb, pair_in, pair_out, fused):
    y_ref, refs = refs[0], refs[1:]
    first = pl.program_id(0) < npb
    if pair_in:
        x_old, refs = jnp.where(first, refs[0][...], refs[1][...]), refs[2:]
    else:
        x_old, refs = refs[0][...], refs[1:]
    g_ref, refs = refs[0], refs[1:]
    y = y_ref[...]
    x = x_old + y * lax.rsqrt(jnp.mean(y * y, axis=-1, keepdims=True) + NORM_EPS) * g_ref[...]
    if fused:
        gn_ref, refs = refs[0], refs[1:]
        refs[-1][...] = (x * lax.rsqrt(jnp.mean(x * x, axis=-1, keepdims=True) + NORM_EPS)
                         * gn_ref[...]).astype(refs[-1].dtype)
    if pair_out:
        @pl.when(first)
        def _():
            refs[0][...] = x

        @pl.when(jnp.logical_not(first))
        def _():
            refs[1][...] = x
    else:
        refs[0][...] = x


def postnorm_residual(y, x, g, mp, g_next=None, pair_out=False):
    m, d = y.shape
    pair_in = isinstance(x, tuple)
    tm = _pick_tile(math.gcd(mp, m - mp), 256, 16)
    npb = mp // tm
    row = pl.BlockSpec((tm, d), lambda i: (i, 0))
    vec = pl.BlockSpec((1, d), lambda i: (0, 0))
    fused = g_next is not None
    in_specs = [row] + (_pair_specs(tm, d, npb) if pair_in else [row]) + [vec] + ([vec] if fused else [])
    args = [y] + (list(x) if pair_in else [x]) + [g.reshape(1, d)] + ([g_next.reshape(1, d)] if fused else [])
    if pair_out:
        out_specs = _pair_specs(tm, d, npb)
        out_shape = [jax.ShapeDtypeStruct((mp, d), F32), jax.ShapeDtypeStruct((m - mp, d), F32)]
    else:
        out_specs, out_shape = [row], [jax.ShapeDtypeStruct((m, d), F32)]
    if fused:
        out_specs, out_shape = out_specs + [row], out_shape + [jax.ShapeDtypeStruct((m, d), BF16)]
    return pl.pallas_call(
        functools.partial(_postnorm_residual_kernel, npb=npb, pair_in=pair_in, pair_out=pair_out, fused=fused),
        grid=(m // tm,),
        in_specs=in_specs,
        out_specs=out_specs,
        out_shape=out_shape,
        compiler_params=_params(("arbitrary",)),
        name="postnorm_residual",
    )(*args)


def _out_proj_kernel(a_ref, w_ref, o_ref, wb_ref):
    @pl.when(pl.program_id(1) == 0)
    def _():
        wb_ref[...] = w_ref[...].astype(BF16)

    o_ref[...] = _dot(a_ref[...], wb_ref[...])


def _in_proj_kernel(a_ref, wt_ref, wg_ref, o_ref, og_ref):
    a = a_ref[...]

    @pl.when(pl.program_id(1) == 0)
    def _():
        og_ref[...] = _dot_nt(a, wg_ref[...].astype(BF16))

    o_ref[...] = _dot_nt(a, wt_ref[...].astype(BF16))


def in_proj(a, wt, layer, n_main, tn_target=512, tm_target=1408):
    m, k = a.shape
    n_gate = wt.shape[1] - n_main
    tm = _pick_tile(m, tm_target, 16)
    tn = _pick_tile(n_main, tn_target, LANES)
    return pl.pallas_call(
        _in_proj_kernel,
        grid=(m // tm, n_main // tn),
        in_specs=[pl.BlockSpec((tm, k), lambda i, j: (i, 0), pipeline_mode=pl.Buffered(1)),
                  pl.BlockSpec((None, tn, k), lambda i, j: (layer, j, 0)),
                  pl.BlockSpec((n_gate, k), lambda i, j: (0, 0))],
        out_specs=[pl.BlockSpec((tm, tn), lambda i, j: (i, j)), pl.BlockSpec((tm, n_gate), lambda i, j: (i, 0))],
        out_shape=[jax.ShapeDtypeStruct((m, n_main), F32), jax.ShapeDtypeStruct((m, n_gate), F32)],
        compiler_params=_params(("parallel", "arbitrary")),
        name="in_proj",
    )(a, wt, wt[layer, n_main:, :])


def out_proj(a, w, layer, tn_target=512, tm_target=1408):
    m, k = a.shape
    n = w.shape[2]
    tm = _pick_tile(m, tm_target, 16)
    tn = _pick_tile(n, tn_target, LANES)
    return pl.pallas_call(
        _out_proj_kernel,
        grid=(n // tn, m // tm),
        in_specs=[pl.BlockSpec((tm, k), lambda j, i: (i, 0)), pl.BlockSpec((None, k, tn), lambda j, i: (layer, 0, j))],
        out_specs=pl.BlockSpec((tm, tn), lambda j, i: (i, j)),
        out_shape=jax.ShapeDtypeStruct((m, n), F32),
        scratch_shapes=[pltpu.VMEM((k, tn), BF16)],
        compiler_params=_params(("parallel", "arbitrary")),
        name="out_proj",
    )(a, w)


def _logf_cumsum_kernel(f_ref, b_ref, init_ref, lf_ref, cum_ref, carry, *, apply_gate):
    @pl.when(pl.program_id(1) == 0)
    def _():
        carry[...] = init_ref[0]

    x = f_ref[...]
    if apply_gate:
        x = _log_sigmoid(x + b_ref[...])
    tb = x.shape[0]
    r = lax.broadcasted_iota(jnp.int32, (tb, tb), 0)
    c = lax.broadcasted_iota(jnp.int32, (tb, tb), 1)
    tri = (c <= r).astype(F32)
    cum = jnp.dot(tri, x, precision=HIGHEST, preferred_element_type=F32) + carry[...]
    lf_ref[...] = x
    cum_ref[...] = cum
    carry[...] = cum[tb - 1:tb, :]


def logf_cumsum(f, row_off, nseq, t, bias, init, apply_gate):
    h = f.shape[1]
    tb = _pick_tile(t, 512, 8)
    nt = t // tb
    off = row_off // tb
    assert row_off % tb == 0
    blk = pl.BlockSpec((tb, h), lambda s, i: (s * nt + i, 0))
    return pl.pallas_call(
        functools.partial(_logf_cumsum_kernel, apply_gate=apply_gate),
        grid=(nseq, nt),
        in_specs=[pl.BlockSpec((tb, h), lambda s, i: (off + s * nt + i, 0)),
                  pl.BlockSpec((1, h), lambda s, i: (0, 0)),
                  pl.BlockSpec((1, 1, h), lambda s, i: (s, 0, 0))],
        out_specs=[blk, blk],
        out_shape=[jax.ShapeDtypeStruct((nseq * t, h), F32)] * 2,
        scratch_shapes=[pltpu.VMEM((1, h), F32)],
        compiler_params=_params(("parallel", "arbitrary")),
        name="logf_cumsum",
    )(f, bias.reshape(1, h), init)


def _fox_select_matrices(h, k_side):
    sel = np.zeros((4, h, h * LANES), np.float32)
    part_lane0, ones_lane0 = (3, 0) if k_side else (0, 3)
    for hh in range(h):
        for p in range(3):
            sel[p, hh, hh * LANES + part_lane0 + p] = 1.0
            sel[3, hh, hh * LANES + ones_lane0 + p] = 1.0
    return jnp.asarray(sel.reshape(4 * h, h * LANES), BF16)


def _fox_build_kernel(x_ref, cum_ref, g_ref, sel_ref, *refs, normalize, mode, aliased, scale, negate, pitch,
                      work_step):
    v_ref = xn_ref = kout_ref = vout_ref = kscr = vscr = xscr = None
    if len(x_ref.shape) == 3:
        refs, xscr = refs[:-1], refs[-1]
    if mode == "kv":
        v_ref = refs[0]
        refs = refs[3:] if aliased else refs[1:]
        xp_ref, kout_ref, vout_ref, kscr, vscr = refs

        @pl.when(pl.program_id(1) != work_step)
        def _():
            kout_ref[...] = jnp.zeros(kout_ref.shape, kout_ref.dtype)
            vout_ref[...] = jnp.zeros(vout_ref.shape, vout_ref.dtype)
    elif mode == "norm":
        xp_ref, xn_ref = refs
    else:
        xp_ref, = refs
    pl.when(pl.program_id(1) == work_step)(functools.partial(
        _fox_build_body, x_ref, cum_ref, g_ref, sel_ref, v_ref, xp_ref, xn_ref, kout_ref, vout_ref, kscr, vscr, xscr,
        normalize=normalize, mode=mode, scale=scale, negate=negate, pitch=pitch))


def _fox_build_body(x_ref, cum_ref, g_ref, sel_ref, v_ref, xp_ref, xn_ref, kout_ref, vout_ref, kscr, vscr, xscr, *,
                    normalize, mode, scale, negate, pitch):
    tm = x_ref.shape[0]
    if xscr is not None:
        def unpack(t, _):
            for g8 in range(x_ref.shape[1] // 8):
                xscr[pl.ds(g8 * 8 * pitch + t, 8, stride=pitch), :] = x_ref[t, g8 * 8:(g8 + 1) * 8, :]
            return 0
        lax.fori_loop(0, tm, unpack, 0, unroll=8)
    c = cum_ref[...] * LOG2E
    if negate:
        c = -c
    c_hi = c.astype(BF16).astype(F32)
    r1 = c - c_hi
    c_mid = r1.astype(BF16).astype(F32)
    c_lo = (r1 - c_mid).astype(BF16).astype(F32)
    parts = jnp.concatenate([c_hi, c_mid, c_lo, jnp.ones_like(c_hi)], axis=1).astype(BF16)
    extra = _dot(parts, sel_ref[...])
    g = g_ref[...]
    for h in range(c.shape[1]):
        xs = x_ref[:, h * LANES:(h + 1) * LANES] if xscr is None else xscr[h * pitch:h * pitch + tm, :]
        if normalize:
            xs = xs * lax.rsqrt(jnp.mean(xs * xs, axis=-1, keepdims=True) + NORM_EPS) * g
        if mode == "norm":
            xn_ref[:, h * LANES:(h + 1) * LANES] = xs
        if mode == "kv":
            kscr[h * pitch:h * pitch + tm, :] = xs
            vscr[h * pitch:h * pitch + tm, :] = v_ref[:, h * LANES:(h + 1) * LANES]
        xp_ref[:, 2 * h * LANES:(2 * h + 1) * LANES] = (xs * scale).astype(BF16)
        xp_ref[:, (2 * h + 1) * LANES:(2 * h + 2) * LANES] = extra[:, h * LANES:(h + 1) * LANES].astype(BF16)
    if mode == "kv":
        def relay(t, _):
            for g8 in range(c.shape[1] // 8):
                rows8 = pl.ds(g8 * 8 * pitch + t, 8, stride=pitch)
                kout_ref[t, g8 * 8:(g8 + 1) * 8, :] = kscr[rows8, :]
                vout_ref[t, g8 * 8:(g8 + 1) * 8, :] = vscr[rows8, :]
            return 0
        lax.fori_loop(0, tm, relay, 0, unroll=8)


def fox_build(x, col_block, row_off, rows, cum, gain, k_side, normalize, scale, mode="plain", kv_bufs=None,
              n_layers=1, slot=0):
    h = cum.shape[1]
    w = h * LANES
    tm = _pick_tile(rows, 128, 16)
    assert row_off % tm == 0 and h % 8 == 0
    off = row_off // tm
    pitch = _relayout_pitch(tm)
    if x.ndim == 4:
        x_spec = pl.BlockSpec((None, tm, h, LANES), lambda i, l: (slot, off + i, 0, 0))
    else:
        x_spec = pl.BlockSpec((tm, w), lambda i, l: (off + i, col_block))
    in_specs = [x_spec,
                pl.BlockSpec((tm, h), lambda i, l: (i, 0)),
                pl.BlockSpec((1, LANES), lambda i, l: (0, 0)),
                pl.BlockSpec((4 * h, w), lambda i, l: (0, 0))]
    args = [x, cum, gain.reshape(1, LANES), _fox_select_matrices(h, k_side)]
    out_specs = [pl.BlockSpec((tm, 2 * w), lambda i, l: (i, 0))]
    out_shape = [jax.ShapeDtypeStruct((rows, 2 * w), BF16)]
    aliases = {}
    scratch = []
    slot_steps, work_step = 1, 0
    if mode == "kv":
        in_specs.append(pl.BlockSpec((tm, w), lambda i, l: (off + i, col_block + 1)))
        args.append(x)
        if kv_bufs is not None:
            in_specs += [ANY, ANY]
            args += list(kv_bufs)
            aliases = {5: 1, 6: 2}
            kv_spec = pl.BlockSpec((None, tm, h, LANES), lambda i, l: (slot, i, 0, 0))
        else:
            slot_steps, work_step = n_layers, slot
            kv_spec = pl.BlockSpec((None, tm, h, LANES), lambda i, l: (l, i, 0, 0))
        out_specs += [kv_spec, kv_spec]
        out_shape += [jax.ShapeDtypeStruct((n_layers, rows, h, LANES), F32)] * 2
        scratch = [pltpu.VMEM((h * pitch, LANES), F32)] * 2
    elif mode == "norm":
        out_specs.append(pl.BlockSpec((tm, w), lambda i, l: (i, 0)))
        out_shape.append(jax.ShapeDtypeStruct((rows, w), F32))
    if x.ndim == 4:
        scratch = scratch + [pltpu.VMEM((h * pitch, LANES), F32)]
    return pl.pallas_call(
        functools.partial(_fox_build_kernel, normalize=normalize, mode=mode, aliased=bool(aliases),
                          scale=scale, negate=k_side, pitch=pitch, work_step=work_step),
        grid=(rows // tm, slot_steps),
        in_specs=in_specs,
        out_specs=out_specs,
        out_shape=out_shape,
        scratch_shapes=scratch,
        input_output_aliases=aliases,
        compiler_params=_params(("parallel", "arbitrary")),
        name="fox_build",
    )(*args)


def _fox_attn_kernel(q_ref, k_ref, v_ref, z_ref, ybuf_ref, o_ref, vt_scr, s_a, s_b, *, tq, unroll):
    i = pl.program_id(1)

    @pl.when(i == 0)
    def _():
        vt_scr[...] = v_ref[...].astype(BF16).T

    q = q_ref[...]
    bufs = (s_a, s_b)

    def logits(j):
        return _dot_nt(k_ref[pl.ds(pl.multiple_of(j * tq, tq), tq), :], q)

    def step(u, j, carry, diagonal):
        m, l, acc = carry
        st = bufs[u % 2][...]
        if not diagonal:
            st_next = logits(j + 1)
        else:
            kidx = lax.broadcasted_iota(jnp.int32, (tq, tq), 0)
            qidx = lax.broadcasted_iota(jnp.int32, (tq, tq), 1)
            st = jnp.where(kidx <= qidx, st, NEG)
        m_new = jnp.maximum(m, jnp.max(st, axis=0, keepdims=True))
        alpha = jnp.exp2(m - m_new)
        p = jnp.exp2(st - m_new)
        l = alpha * l + jnp.sum(p, axis=0, keepdims=True)
        acc = alpha * acc + _dot(vt_scr[:, pl.ds(pl.multiple_of(j * tq, tq), tq)], p.astype(BF16))
        if not diagonal:
            bufs[(u + 1) % 2][...] = st_next
        return m_new, l, acc

    def trip(jj, carry):
        for u in range(unroll):
            carry = step(u, unroll * jj + u, carry, False)
        return carry

    def finish(rem, j0, carry):
        for u in range(rem):
            carry = step(u, j0 + u, carry, False)
        _, l, acc = step(rem, j0 + rem, carry, True)
        z = z_ref[...]
        o_ref[...] = ((acc / l).T * (z * _sigmoid(z))).astype(o_ref.dtype)

    s_a[...] = logits(0)
    carry = (jnp.full((1, tq), NEG, F32), jnp.zeros((1, tq), F32), jnp.zeros((LANES, tq), F32))
    trips = i // unroll
    carry = lax.fori_loop(0, trips, trip, carry)
    for rem in range(unroll):
        pl.when(i - trips * unroll == rem)(functools.partial(finish, rem, trips * unroll, carry))


def fox_attention_prompt(qp, kp, proj, nseq, t, nheads, ybuf):
    tq = _pick_tile(t, FOX_TQ, LANES)
    assert FOX_UNROLL % 2 == 0
    nq = t // tq
    blk = lambda g, i: (g // nheads * nq + i, g % nheads)
    return pl.pallas_call(
        functools.partial(_fox_attn_kernel, tq=tq, unroll=FOX_UNROLL),
        grid=(nseq * nheads, nq),
        in_specs=[pl.BlockSpec((tq, 2 * LANES), blk),
                  pl.BlockSpec((t, 2 * LANES), lambda g, i: (g // nheads, g % nheads)),
                  pl.BlockSpec((t, LANES), lambda g, i: (g // nheads, 2 * nheads + g % nheads)),
                  pl.BlockSpec((tq, LANES), lambda g, i: (g // nheads * nq + i, 3 * nheads + g % nheads)),
                  ANY],
        out_specs=pl.BlockSpec((tq, LANES), blk),
        out_shape=jax.ShapeDtypeStruct(ybuf.shape, ybuf.dtype),
        scratch_shapes=[pltpu.VMEM((LANES, t), BF16), pltpu.VMEM((tq, tq), F32), pltpu.VMEM((tq, tq), F32)],
        input_output_aliases={4: 0},
        compiler_params=_params(("parallel", "arbitrary")),
        name="fox_attention",
    )(qp, kp, proj, proj, ybuf)


def _fox_attn_sample_kernel(q_ref, kc_ref, kn_ref, vc_ref, vn_ref, z_ref, ybuf_ref, o_ref, vscr, *, nh, pitch):
    ts = q_ref.shape[0]
    p = vc_ref.shape[0]

    def unpack(t, _):
        vscr[pl.ds(t, nh, stride=pitch), :] = vc_ref[t]
        return 0
    lax.fori_loop(0, p, unpack, 0, unroll=8)
    r = lax.broadcasted_iota(jnp.int32, (ts, ts), 0)
    c = lax.broadcasted_iota(jnp.int32, (ts, ts), 1)
    for u in range(nh):
        q = q_ref[:, 2 * u * LANES:2 * (u + 1) * LANES]
        s_c = _dot_nt(q, kc_ref[:, 2 * u * LANES:2 * (u + 1) * LANES])
        s_n = jnp.where(c <= r, _dot_nt(q, kn_ref[:, 2 * u * LANES:2 * (u + 1) * LANES]), NEG)
        m = jnp.maximum(jnp.max(s_c, axis=-1, keepdims=True), jnp.max(s_n, axis=-1, keepdims=True))
        p_c = jnp.exp2(s_c - m)
        p_n = jnp.exp2(s_n - m)
        l = jnp.sum(p_c, axis=-1, keepdims=True) + jnp.sum(p_n, axis=-1, keepdims=True)
        acc = (_dot(p_c.astype(BF16), vscr[u * pitch:u * pitch + p, :].astype(BF16))
               + _dot(p_n.astype(BF16), vn_ref[:, u * LANES:(u + 1) * LANES].astype(BF16)))
        z = z_ref[:, u * LANES:(u + 1) * LANES]
        o_ref[:, u * LANES:(u + 1) * LANES] = (acc / l * (z * _sigmoid(z))).astype(o_ref.dtype)


def fox_attention_sample(qp, kp_c, kp_n, cache_v, slot, proj, row_off, nseq, ts, p, nheads, ybuf):
    nh = FOX_HEADS_PER_STEP
    ng = nheads // nh
    assert row_off % ts == 0 and nheads % nh == 0
    off = row_off // ts
    pitch = _relayout_pitch(p)
    return pl.pallas_call(
        functools.partial(_fox_attn_sample_kernel, nh=nh, pitch=pitch),
        grid=(nseq, ng),
        in_specs=[pl.BlockSpec((ts, 2 * nh * LANES), lambda s, g: (s, g)),
                  pl.BlockSpec((p, 2 * nh * LANES), lambda s, g: (s, g)),
                  pl.BlockSpec((ts, 2 * nh * LANES), lambda s, g: (s, g)),
                  pl.BlockSpec((None, p, nh, LANES), lambda s, g: (slot, s, g, 0)),
                  pl.BlockSpec((ts, nh * LANES), lambda s, g: (off + s, 2 * ng + g)),
                  pl.BlockSpec((ts, nh * LANES), lambda s, g: (off + s, 3 * ng + g)),
                  ANY],
        out_specs=pl.BlockSpec((ts, nh * LANES), lambda s, g: (off + s, g)),
        out_shape=jax.ShapeDtypeStruct(ybuf.shape, ybuf.dtype),
        scratch_shapes=[pltpu.VMEM((nh * pitch, LANES), F32)],
        input_output_aliases={6: 0},
        compiler_params=_params(("parallel", "arbitrary")),
        name="fox_attention_sample",
    )(qp, kp_c, kp_n, cache_v, proj, proj, ybuf)


def fox_layer(h, mp_rows, b, t, bs, ts, cache_k, cache_v, cache_lf, w_in, slot, b_f, q_gain, k_gain, kv_bufs):
    nheads = b_f.shape[0]
    w = nheads * LANES
    n_layers, _, p = cache_k.shape[:3]
    qscale = float(LANES) ** -0.5 * LOG2E
    wt = jnp.swapaxes(w_in, 1, 2)
    proj, fgate = in_proj(h, wt, slot, 4 * w)
    ybuf = jnp.zeros((h.shape[0], w), BF16)
    lf_p, cum_p = logf_cumsum(fgate, 0, b, t, b_f, jnp.zeros((b, 1, nheads), F32), True)
    qp_p, = fox_build(proj, 0, 0, mp_rows, cum_p, q_gain, False, True, qscale)
    kp_p, kbuf, vbuf = fox_build(proj, 1, 0, mp_rows, cum_p, k_gain, True, True, 1.0, "kv", kv_bufs, n_layers, slot)
    ybuf = fox_attention_prompt(qp_p, kp_p, proj, b, t, nheads, ybuf)
    ms_rows = bs * ts
    _, cum_c = logf_cumsum(cache_lf.reshape(bs * p, nheads), 0, bs, p, b_f, jnp.zeros((bs, 1, nheads), F32), False)
    init_s = cum_c.reshape(bs, p, nheads)[:, p - 1:, :]
    lf_s, cum_s = logf_cumsum(fgate, mp_rows, bs, ts, b_f, init_s, True)
    qp_s, = fox_build(proj, 0, mp_rows, ms_rows, cum_s, q_gain, False, True, qscale)
    kp_s, kn_s = fox_build(proj, 1, mp_rows, ms_rows, cum_s, k_gain, True, True, 1.0, "norm")
    by_head = (n_layers, bs * p, nheads, LANES)
    kp_c, = fox_build(cache_k.reshape(by_head), 0, 0, bs * p, cum_c, k_gain, True, False, 1.0, slot=slot)
    ybuf = fox_attention_sample(qp_s, kp_c, kp_s, cache_v.reshape(by_head), slot, proj, mp_rows, bs, ts, p,
                                nheads, ybuf)
    v_s = proj[mp_rows:, 2 * w:3 * w]
    hd = LANES
    outs = (lf_p.reshape(b, t, nheads), kn_s.reshape(bs, ts, nheads, hd), v_s.reshape(bs, ts, nheads, hd),
            lf_s.reshape(bs, ts, nheads))
    return ybuf, (kbuf, vbuf), outs


def _mlstm_kernel(q_ref, k_ref, v_ref, o_ref, z_ref, gate_ref, bi_ref, bf_ref, gain_ref, expand_ref,
                  ct0_ref, n0_ref, m0_ref, ybuf_ref, y_ref, ct_ref, n_ref, m_ref, *, nheads, dk, dv):
    @pl.when(pl.program_id(1) == 0)
    def _():
        ct_ref[...] = ct0_ref[...]
        n_ref[...] = n0_ref[...]
        m_ref[...] = m0_ref[...]

    ln = q_ref.shape[0]
    pack = MXU_DIM // ln
    ngrp = nheads // pack
    rr = pack * ln
    lsh = ln.bit_length() - 1
    gw = pack * dv
    gates = gate_ref[...]
    log_i = gates[:, :nheads] + bi_ref[...]
    log_f = _log_sigmoid(gates[:, nheads:] + bf_ref[...])
    rl = lax.broadcasted_iota(jnp.int32, (ln, ln), 0)
    cl = lax.broadcasted_iota(jnp.int32, (ln, ln), 1)
    b_all = jnp.dot((cl <= rl).astype(F32), log_f, precision=HIGHEST, preferred_element_type=F32)
    m_row = m_ref[0, :, 0:nheads]
    b_last = b_all[ln - 1:ln, :]
    g_all = b_last - b_all + log_i
    m_new = jnp.maximum(b_last + m_row, jnp.max(g_all, axis=0, keepdims=True))
    wk_all = jnp.exp(g_all - m_new)
    carry_w = jnp.exp(b_last + m_row - m_new)
    carry_wide = jnp.dot(carry_w, expand_ref[...], precision=HIGHEST, preferred_element_type=F32)
    rh = lax.broadcasted_iota(jnp.int32, (nheads, nheads), 0)
    ch = lax.broadcasted_iota(jnp.int32, (nheads, nheads), 1)
    carry_col = jnp.sum(jnp.where(rh == ch, carry_w, 0.0), axis=1, keepdims=True)

    def stack_cols(a):
        return jnp.stack([jnp.concatenate([a[:, g * pack + p:g * pack + p + 1] for p in range(pack)], axis=0)
                          for g in range(ngrp)], axis=0)

    def stack_heads(ref, width):
        return jnp.stack([jnp.concatenate([ref[:, (g * pack + p) * width:(g * pack + p + 1) * width]
                                           for p in range(pack)], axis=0) for g in range(ngrp)], axis=0)

    bcol = stack_cols(b_all)
    licol = stack_cols(log_i)
    mprev = stack_cols(jnp.broadcast_to(m_row, (ln, nheads)))
    wkcol = stack_cols(wk_all)
    r = lax.broadcasted_iota(jnp.int32, (rr, rr), 0)
    c = lax.broadcasted_iota(jnp.int32, (rr, rr), 1)
    incl = ((r >> lsh) == (c >> lsh)) & (c <= r)
    brow = jnp.sum(jnp.where(r == c, bcol, 0.0), axis=1, keepdims=True)
    lirow = jnp.sum(jnp.where(r == c, licol, 0.0), axis=1, keepdims=True)
    d = jnp.where(incl, bcol - brow + lirow, -jnp.inf)
    inter = bcol + mprev
    m_t = jnp.maximum(inter, jnp.max(d, axis=-1, keepdims=True))
    qf = stack_heads(q_ref, dk)
    kf = stack_heads(k_ref, dk) * (float(dk) ** -0.5)
    vf = stack_heads(v_ref, dv)
    qb, kb, vb = qf.astype(BF16), kf.astype(BF16), vf.astype(BF16)
    wmat = jnp.exp(d - m_t) * _bmm_nt(qb, kb)
    inter_w = jnp.exp(inter - m_t)
    ctb = ct_ref[0].astype(BF16)
    rowblk = lax.broadcasted_iota(jnp.int32, (rr, 1), 0) >> lsh

    def own_block(wide):
        out = jnp.where(rowblk == 0, wide[:, :dv], 0.0)
        for p in range(1, pack):
            out = out + jnp.where(rowblk == p, wide[:, p * dv:(p + 1) * dv], 0.0)
        return out

    qc = jnp.stack([own_block(_dot(qb[g], ctb[:, g * gw:(g + 1) * gw])) for g in range(ngrp)], axis=0)
    num = _bmm(wmat.astype(BF16), vb) + inter_w * qc
    nst = jnp.stack([jnp.concatenate([jnp.broadcast_to(n_ref[0, g * pack + p:g * pack + p + 1, :], (ln, dk))
                                      for p in range(pack)], axis=0) for g in range(ngrp)], axis=0)
    den = jnp.sum(wmat, axis=-1, keepdims=True) + inter_w * jnp.sum(qf * nst, axis=-1, keepdims=True)
    hid = num / jnp.maximum(jnp.abs(den), jnp.exp(-m_t))
    own = ((lax.broadcasted_iota(jnp.int32, (rr, gw), 0) >> lsh)
           == (lax.broadcasted_iota(jnp.int32, (rr, gw), 1) >> (dv.bit_length() - 1)))
    blocksum = ((lax.broadcasted_iota(jnp.int32, (pack, rr), 1) >> lsh)
                == lax.broadcasted_iota(jnp.int32, (pack, rr), 0)).astype(F32)
    for g in range(ngrp):
        wv = (wkcol[g] * vf[g]).astype(BF16)
        vwbd = jnp.where(own, jnp.concatenate([wv] * pack, axis=1), jnp.zeros((), BF16))
        cols = slice(g * gw, (g + 1) * gw)
        ct_ref[0, :, cols] = carry_wide[:, cols] * ct_ref[0, :, cols] + _dot_tn(kb[g], vwbd)
        heads = slice(g * pack, (g + 1) * pack)
        ksum = jnp.dot(blocksum, wkcol[g] * kf[g], precision=HIGHEST, preferred_element_type=F32)
        n_ref[0, heads, :] = carry_col[heads, :] * n_ref[0, heads, :] + ksum
        for p in range(pack):
            h = g * pack + p
            hh = hid[g, p * ln:(p + 1) * ln, :]
            hn = hh * lax.rsqrt(jnp.mean(hh * hh, axis=-1, keepdims=True) + NORM_EPS)
            hn = hn * gain_ref[:, h * dv:(h + 1) * dv]
            og = o_ref[:, h * dv:(h + 1) * dv]
            zg = z_ref[:, h * dv:(h + 1) * dv]
            y_ref[:, h * dv:(h + 1) * dv] = (hn * _sigmoid(og) * (zg * _sigmoid(zg))).astype(y_ref.dtype)
    m_ref[0, :, 0:nheads] = m_new


def mlstm_scan(proj, gates, row_off, nseq, t, ln, c0, n0, m0, b_i, b_f, gain, ybuf):
    nheads, dv, dk = c0.shape[1:]
    nc = t // ln
    assert row_off % ln == 0
    off = row_off // ln
    qk_w, v_w = nheads * dk, nheads * dv
    assert v_w == 2 * qk_w

    def rows(width, cb):
        return pl.BlockSpec((ln, width), lambda s, i: (off + s * nc + i, cb))

    assert MXU_DIM % ln == 0 and nheads % (MXU_DIM // ln) == 0 and nheads <= LANES
    ct0 = jnp.transpose(c0, (0, 3, 1, 2)).reshape(nseq, dk, v_w)
    m0p = jnp.pad(m0[:, None, :], ((0, 0), (0, 0), (0, LANES - nheads)))
    expand = jnp.asarray(np.kron(np.eye(nheads, dtype=np.float32), np.ones((1, dv), np.float32)))
    state = [pl.BlockSpec((1, dk, v_w), lambda s, i: (s, 0, 0)),
             pl.BlockSpec((1, nheads, dk), lambda s, i: (s, 0, 0)),
             pl.BlockSpec((1, 1, LANES), lambda s, i: (s, 0, 0))]
    y, ct, n, m = pl.pallas_call(
        functools.partial(_mlstm_kernel, nheads=nheads, dk=dk, dv=dv),
        grid=(nseq, nc),
        in_specs=[rows(qk_w, 0), rows(qk_w, 1), rows(v_w, 1), rows(v_w, 2), rows(v_w, 3),
                  rows(2 * nheads, 0),
                  pl.BlockSpec((1, nheads), lambda s, i: (0, 0)),
                  pl.BlockSpec((1, nheads), lambda s, i: (0, 0)),
                  pl.BlockSpec((1, v_w), lambda s, i: (0, 0)),
                  pl.BlockSpec((nheads, v_w), lambda s, i: (0, 0))] + state + [ANY],
        out_specs=[rows(v_w, 0)] + state,
        out_shape=[jax.ShapeDtypeStruct(ybuf.shape, ybuf.dtype),
                   jax.ShapeDtypeStruct((nseq, dk, v_w), F32),
                   jax.ShapeDtypeStruct((nseq, nheads, dk), F32),
                   jax.ShapeDtypeStruct((nseq, 1, LANES), F32)],
        input_output_aliases={13: 0},
        compiler_params=_params(("parallel", "arbitrary")),
        name="mlstm_scan",
    )(proj, proj, proj, proj, proj, gates, b_i.reshape(1, nheads), b_f.reshape(1, nheads),
      gain.reshape(1, v_w), expand, ct0, n0, m0p, ybuf)
    c = jnp.transpose(ct.reshape(nseq, dk, nheads, dv), (0, 2, 3, 1))
    return y, c, n, m[:, 0, :nheads]


def mlstm_layer(h, mp_rows, b, t, bs, ts, c0, n0, m0, w_in, slot, b_i, b_f, gain):
    nheads, dv, dk = c0.shape[1:]
    main = 2 * nheads * dk + 3 * nheads * dv
    wt = jnp.swapaxes(w_in, 1, 2)
    proj, gates = in_proj(h, wt, slot, main)
    ybuf = jnp.zeros((h.shape[0], nheads * dv), BF16)
    zc = jnp.zeros((b, nheads, dv, dk), F32)
    ybuf, c_p, n_p, m_p = mlstm_scan(proj, gates, 0, b, t, min(CHUNK, t), zc, zc[:, :, 0, :], zc[:, :, 0, 0],
                                     b_i, b_f, gain, ybuf)
    ybuf, c_s, n_s, m_s = mlstm_scan(proj, gates, mp_rows, bs, ts, ts, c0, n0, m0, b_i, b_f, gain, ybuf)
    return ybuf, (c_p, n_p, m_p, c_s, n_s, m_s)


def _gdn_conv_kernel(x_ref, halo_ref, init_ref, w_ref, sc_ref, o_ref, *, l2norm):
    tm = x_ref.shape[0]
    prev = jnp.where(pl.program_id(1) == 0, init_ref[0], halo_ref[...])
    xc = jnp.concatenate([prev, x_ref[...]], axis=0)
    w = w_ref[...]
    acc = w[3:4] * xc[8:8 + tm] + w[2:3] * xc[7:7 + tm] + w[1:2] * xc[6:6 + tm] + w[0:1] * xc[5:5 + tm]
    act = acc * _sigmoid(acc)
    if l2norm:
        for g in range(act.shape[1] // LANES):
            a = act[:, g * LANES:(g + 1) * LANES]
            a = a * lax.rsqrt(jnp.sum(a * a, axis=-1, keepdims=True) + NORM_EPS)
            o_ref[:, g * LANES:(g + 1) * LANES] = (a * sc_ref[:, g * LANES:(g + 1) * LANES]).astype(o_ref.dtype)
    else:
        o_ref[...] = act.astype(o_ref.dtype)


def gdn_conv(proj, row_off, nseq, t, ch_off, ch, init, conv_w, slot, scale, l2norm, out_dtype):
    tm = _pick_tile(t, 512, 8)
    tc = _pick_tile(ch, 1024, LANES)
    nt = t // tm
    assert row_off % tm == 0 and ch_off % tc == 0 and tm % 8 == 0
    off, coff, sub = row_off // tm, ch_off // tc, tm // 8
    return pl.pallas_call(
        functools.partial(_gdn_conv_kernel, l2norm=l2norm),
        grid=(nseq, nt, ch // tc),
        in_specs=[pl.BlockSpec((tm, tc), lambda s, i, j: (off + s * nt + i, coff + j)),
                  pl.BlockSpec((8, tc), lambda s, i, j: (jnp.maximum((off + s * nt + i) * sub - 1, 0), coff + j)),
                  pl.BlockSpec((1, 8, tc), lambda s, i, j: (s, 0, coff + j)),
                  pl.BlockSpec((None, conv_w.shape[1], tc), lambda s, i, j: (slot, 0, coff + j)),
                  pl.BlockSpec((1, tc), lambda s, i, j: (0, j))],
        out_specs=pl.BlockSpec((tm, tc), lambda s, i, j: (s * nt + i, j)),
        out_shape=jax.ShapeDtypeStruct((nseq * t, ch), out_dtype),
        compiler_params=_params(("parallel", "arbitrary", "arbitrary")),
        name="gdn_conv",
    )(proj, proj, init, conv_w, scale)


def _unit_lower_inverse(a, r, c, levels):
    x = jnp.where(r == c, 1.0, 0.0) - jnp.where(((r >> 1) == (c >> 1)) & (c < r), a, 0.0)
    for sh in range(1, levels):
        off = jnp.where(((r >> (sh + 1)) == (c >> (sh + 1))) & ((r >> sh) != (c >> sh)) & (c < r), a, 0.0)
        xb = x.astype(BF16)
        x = x - _bmm(_bmm(xb, off.astype(BF16)).astype(BF16), xb)
    return x


def _gdn_kernel(qk_ref, v_ref, z_ref, gate_ref, alog_ref, dt_ref, gain_ref, expand_ref, ws0_ref, ybuf_ref,
                y_ref, ws_ref, *, kheads, vheads):
    @pl.when(pl.program_id(1) == 0)
    def _():
        ws_ref[...] = ws0_ref[...]

    ln = v_ref.shape[0]
    hd = LANES
    pack = MXU_DIM // ln
    ngrp = vheads // pack
    rr = pack * ln
    lsh = ln.bit_length() - 1
    rep = vheads // kheads
    gates = gate_ref[...]
    g_all = -jnp.exp(alog_ref[...]) * _softplus(gates[:, :vheads] + dt_ref[...])
    beta_all = _sigmoid(gates[:, vheads:])
    rl = lax.broadcasted_iota(jnp.int32, (ln, ln), 0)
    cl = lax.broadcasted_iota(jnp.int32, (ln, ln), 1)
    gsum_all = jnp.dot((cl <= rl).astype(F32), g_all, precision=HIGHEST, preferred_element_type=F32)
    glast_row = gsum_all[ln - 1:ln, :]
    decay_row = jnp.exp(jnp.dot(glast_row, expand_ref[...], precision=HIGHEST, preferred_element_type=F32))

    def stack_cols(a):
        return jnp.stack([jnp.concatenate([a[:, g * pack + p:g * pack + p + 1] for p in range(pack)], axis=0)
                          for g in range(ngrp)], axis=0)

    def stack_heads(ref, head_of):
        return jnp.stack([jnp.concatenate([ref[:, head_of(g * pack + p) * hd:(head_of(g * pack + p) + 1) * hd]
                                           for p in range(pack)], axis=0) for g in range(ngrp)], axis=0)

    gcol = stack_cols(gsum_all)
    bt = stack_cols(beta_all)
    glast = stack_cols(jnp.broadcast_to(glast_row, (ln, vheads)))
    r = lax.broadcasted_iota(jnp.int32, (rr, rr), 0)
    c = lax.broadcasted_iota(jnp.int32, (rr, rr), 1)
    blk = (r >> lsh) == (c >> lsh)
    incl = blk & (c <= r)
    strict = blk & (c < r)
    grow = jnp.sum(jnp.where(r == c, gcol, 0.0), axis=1, keepdims=True)
    dincl = jnp.where(incl, jnp.exp(jnp.where(incl, gcol - grow, 0.0)), 0.0)
    qst = stack_heads(qk_ref, lambda h: h // rep)
    kst = stack_heads(qk_ref, lambda h: kheads + h // rep)
    vst = stack_heads(v_ref, lambda h: h)
    kk = _bmm_nt(kst, kst)
    qk = _bmm_nt(qst, kst)
    x = _unit_lower_inverse(jnp.where(strict, bt * dincl * kk, 0.0), r, c, lsh)
    eg = jnp.exp(gcol)
    wsb = ws_ref[0].astype(BF16)
    gw = pack * hd
    rowblk = lax.broadcasted_iota(jnp.int32, (rr, 1), 0) >> lsh

    def own_block(wide):
        out = jnp.where(rowblk == 0, wide[:, :hd], 0.0)
        for p in range(1, pack):
            out = out + jnp.where(rowblk == p, wide[:, p * hd:(p + 1) * hd], 0.0)
        return out

    ksd = jnp.stack([own_block(_dot(kst[g], wsb[:, g * gw:(g + 1) * gw])) for g in range(ngrp)], axis=0)
    qsd = jnp.stack([own_block(_dot(qst[g], wsb[:, g * gw:(g + 1) * gw])) for g in range(ngrp)], axis=0)
    rhs = bt * (vst - eg * ksd)
    u = _bmm(x.astype(BF16), rhs.astype(BF16))
    o = eg * qsd + _bmm((dincl * qk).astype(BF16), u.astype(BF16))
    uw = (jnp.exp(glast - gcol) * u).astype(BF16)
    own = ((lax.broadcasted_iota(jnp.int32, (rr, gw), 0) >> lsh)
           == (lax.broadcasted_iota(jnp.int32, (rr, gw), 1) >> (hd.bit_length() - 1)))
    gain = gain_ref[...]
    for g in range(ngrp):
        uwbd = jnp.where(own, jnp.concatenate([uw[g]] * pack, axis=1), jnp.zeros((), BF16))
        cols = slice(g * gw, (g + 1) * gw)
        ws_ref[0, :, cols] = decay_row[:, cols] * ws_ref[0, :, cols] + _dot_tn(kst[g], uwbd)
        for p in range(pack):
            h = g * pack + p
            oh = o[g, p * ln:(p + 1) * ln, :]
            on = oh * lax.rsqrt(jnp.mean(oh * oh, axis=-1, keepdims=True) + NORM_EPS) * gain
            zg = z_ref[:, h * hd:(h + 1) * hd]
            y_ref[:, h * hd:(h + 1) * hd] = (on * (zg * _sigmoid(zg))).astype(y_ref.dtype)


def gdn_scan(qk, v, proj, z_col_block, gates, row_off, nseq, t, ln, s0, a_log, dt_bias, gain, ybuf):
    vheads, dv, dk = s0.shape[1:]
    v_w = vheads * LANES
    kheads = qk.shape[1] // (2 * LANES)
    nc = t // ln
    assert row_off % ln == 0 and MXU_DIM % ln == 0 and vheads % (MXU_DIM // ln) == 0 and dv == LANES and dk == LANES
    off = row_off // ln
    ws0 = jnp.transpose(s0, (0, 3, 1, 2)).reshape(nseq, dk, v_w)
    expand = jnp.asarray(np.kron(np.eye(vheads, dtype=np.float32), np.ones((1, LANES), np.float32)))
    local = lambda width: pl.BlockSpec((ln, width), lambda s, i: (s * nc + i, 0))
    state = pl.BlockSpec((1, dk, v_w), lambda s, i: (s, 0, 0))
    vec = lambda width: pl.BlockSpec((1, width), lambda s, i: (0, 0))
    ybuf, ws = pl.pallas_call(
        functools.partial(_gdn_kernel, kheads=kheads, vheads=vheads),
        grid=(nseq, nc),
        in_specs=[local(qk.shape[1]), local(v_w),
                  pl.BlockSpec((ln, v_w), lambda s, i: (off + s * nc + i, z_col_block)),
                  pl.BlockSpec((ln, 2 * vheads), lambda s, i: (off + s * nc + i, 0)),
                  vec(vheads), vec(vheads), vec(LANES),
                  pl.BlockSpec((vheads, v_w), lambda s, i: (0, 0)), state, ANY],
        out_specs=[pl.BlockSpec((ln, v_w), lambda s, i: (off + s * nc + i, 0)), state],
        out_shape=[jax.ShapeDtypeStruct(ybuf.shape, ybuf.dtype),
                   jax.ShapeDtypeStruct((nseq, dk, v_w), F32)],
        input_output_aliases={9: 0},
        compiler_params=_params(("parallel", "arbitrary")),
        name="gdn_scan",
    )(qk, v, proj, gates, a_log.reshape(1, vheads), dt_bias.reshape(1, vheads), gain.reshape(1, LANES),
      expand, ws0, ybuf)
    return ybuf, jnp.transpose(ws.reshape(nseq, dk, vheads, dv), (0, 2, 3, 1))


def gdn_layer(h, mp_rows, b, t, bs, ts, s0, conv0, w_in, slot, conv_w, a_log, dt_bias, gain):
    vheads = s0.shape[1]
    v_w = vheads * LANES
    taps, ch = conv_w.shape[1:]
    qk_w = ch - v_w
    wt = jnp.swapaxes(w_in, 1, 2)
    proj, gates = in_proj(h, wt, slot, ch + v_w)
    scale = jnp.concatenate([jnp.full((1, qk_w // 2), float(LANES) ** -0.5, F32), jnp.ones((1, qk_w // 2), F32)], axis=1)
    ones = jnp.ones((1, v_w), F32)
    ybuf = jnp.zeros((h.shape[0], v_w), BF16)
    outs = []
    for row_off, nseq, tt, ln, init_rows, st0 in (
            (0, b, t, min(CHUNK, t), jnp.zeros((b, taps - 1, ch), F32), jnp.zeros((b,) + s0.shape[1:], F32)),
            (mp_rows, bs, ts, ts, conv0, s0)):
        init = jnp.pad(init_rows, ((0, 0), (8 - (taps - 1), 0), (0, 0)))
        qk = gdn_conv(proj, row_off, nseq, tt, 0, qk_w, init, conv_w, slot, scale, True, BF16)
        v = gdn_conv(proj, row_off, nseq, tt, qk_w, v_w, init, conv_w, slot, ones, False, F32)
        ybuf, s_new = gdn_scan(qk, v, proj, ch // v_w, gates, row_off, nseq, tt, ln, st0, a_log[slot], dt_bias[slot],
                               gain[slot], ybuf)
        conv_state = jnp.stack([proj[row_off + (s + 1) * tt - (taps - 1):row_off + (s + 1) * tt, :ch]
                                for s in range(nseq)])
        outs += [s_new, conv_state]
    return ybuf, tuple(outs)


def kernel(x_prompt, x_sample, cache_fox_k, cache_fox_v, cache_fox_logf, state_mlstm_c, state_mlstm_n, state_mlstm_m, state_gdn_s, state_gdn_conv, pre_norm, post_norm, fox_w_in, fox_b_f, fox_q_norm, fox_k_norm, fox_w_out, mlstm_w_in, mlstm_b_i, mlstm_b_f, mlstm_h_norm, mlstm_w_out, gdn_w_in, gdn_conv_w, gdn_a_log, gdn_dt_bias, gdn_o_norm, gdn_w_out):
    b, t, d = x_prompt.shape
    bs, ts, _ = x_sample.shape
    mp_rows = b * t
    n_fox = fox_w_in.shape[0]
    x = (x_prompt.reshape(mp_rows, d), x_sample.reshape(bs * ts, d))
    kv_bufs = None
    fox, ml, gd = [], [], []
    depth = pre_norm.shape[0]
    h = rmsnorm_cast(x[0], x[1], pre_norm[0])
    for layer in range(depth):
        kind, j = layer % 3, layer // 3
        if kind == 0:
            y, kv_bufs, outs = fox_layer(h, mp_rows, b, t, bs, ts, cache_fox_k, cache_fox_v, cache_fox_logf[j],
                                         fox_w_in, j, fox_b_f[j], fox_q_norm[j], fox_k_norm[j], kv_bufs)
            fox.append(outs)
            w_out = fox_w_out
        elif kind == 1:
            y, outs = mlstm_layer(h, mp_rows, b, t, bs, ts, state_mlstm_c[j], state_mlstm_n[j], state_mlstm_m[j],
                                  mlstm_w_in, j, mlstm_b_i[j], mlstm_b_f[j], mlstm_h_norm[j])
            ml.append(outs)
            w_out = mlstm_w_out
        else:
            y, outs = gdn_layer(h, mp_rows, b, t, bs, ts, state_gdn_s[j], state_gdn_conv[j], gdn_w_in, j,
                                gdn_conv_w, gdn_a_log, gdn_dt_bias, gdn_o_norm)
            gd.append(outs)
            w_out = gdn_w_out
        mixed = out_proj(y, w_out, j)
        if layer + 1 < depth:
            x, h = postnorm_residual(mixed, x, post_norm[layer], mp_rows, g_next=pre_norm[layer + 1])
        else:
            x = postnorm_residual(mixed, x, post_norm[layer], mp_rows, pair_out=True)
    stack = lambda group, idx: jnp.stack([o[idx] for o in group])
    nh = fox_b_f.shape[1]
    kv_shape = (n_fox, b, t, nh, LANES)
    return ((x[0].reshape(b, t, d), x[1].reshape(bs, ts, d),
             kv_bufs[0].reshape(kv_shape), kv_bufs[1].reshape(kv_shape))
            + tuple(stack(fox, i) for i in range(4))
            + tuple(stack(ml, i) for i in range(6))
            + tuple(stack(gd, i) for i in range(4)))
```

```python
import functools
import math

import numpy as np
import jax
import jax.numpy as jnp
from jax import lax
from jax.experimental import pallas as pl
from jax.experimental.pallas import tpu as pltpu

F32 = jnp.float32
BF16 = jnp.bfloat16
NORM_EPS = 1e-6
CHUNK = 64
LANES = 128
MXU_DIM = 256
NEG = -1e30
VMEM_LIMIT = 56 * 1024 * 1024
FOX_TQ = 512
FOX_UNROLL = 4
FOX_HEADS_PER_STEP = 8
LOG2E = math.log2(math.e)
HIGHEST = lax.Precision.HIGHEST
ANY = pl.BlockSpec(memory_space=pl.ANY)


def _params(sem, vmem=VMEM_LIMIT):
    return pltpu.CompilerParams(dimension_semantics=sem, vmem_limit_bytes=vmem)


def _pick_tile(n, target, mult):
    best = None
    for t in range(mult, min(n, target) + 1, mult):
        if n % t == 0:
            best = t
    assert best is not None, (n, target, mult)
    return best


def _relayout_pitch(rows):
    assert rows % 8 == 0
    pitch = rows + 8
    return pitch if (pitch // 8) % 2 else pitch + 8


def _sigmoid(x):
    return 0.5 * jnp.tanh(0.5 * x) + 0.5


def _log_sigmoid(x):
    return jnp.minimum(x, 0.0) - jnp.log(1.0 + jnp.exp(-jnp.abs(x)))


def _softplus(x):
    return jnp.maximum(x, 0.0) + jnp.log(1.0 + jnp.exp(-jnp.abs(x)))


def _dot(a, b):
    return jnp.dot(a, b, preferred_element_type=F32)


def _dot_nt(a, b):
    return lax.dot_general(a, b, (((1,), (1,)), ((), ())), preferred_element_type=F32)


def _dot_tn(a, b):
    return lax.dot_general(a, b, (((0,), (0,)), ((), ())), preferred_element_type=F32)


def _bmm(a, b):
    return lax.dot_general(a, b, (((2,), (1,)), ((0,), (0,))), preferred_element_type=F32)


def _bmm_nt(a, b):
    return lax.dot_general(a, b, (((2,), (2,)), ((0,), (0,))), preferred_element_type=F32)


def _pair_specs(tm, d, npb):
    return [pl.BlockSpec((tm, d), lambda i: (jnp.minimum(i, npb - 1), 0)),
            pl.BlockSpec((tm, d), lambda i: (jnp.maximum(i - npb, 0), 0))]


def _rmsnorm_cast_kernel(xp_ref, xs_ref, g_ref, o_ref, *, npb):
    x = jnp.where(pl.program_id(0) < npb, xp_ref[...], xs_ref[...])
    ms = jnp.mean(x * x, axis=-1, keepdims=True)
    o_ref[...] = (x * lax.rsqrt(ms + NORM_EPS) * g_ref[...]).astype(o_ref.dtype)


def rmsnorm_cast(xp, xs, g):
    (mp, d), ms = xp.shape, xs.shape[0]
    tm = _pick_tile(math.gcd(mp, ms), 256, 16)
    npb = mp // tm
    return pl.pallas_call(
        functools.partial(_rmsnorm_cast_kernel, npb=npb),
        grid=((mp + ms) // tm,),
        in_specs=_pair_specs(tm, d, npb) + [pl.BlockSpec((1, d), lambda i: (0, 0))],
        out_specs=pl.BlockSpec((tm, d), lambda i: (i, 0)),
        out_shape=jax.ShapeDtypeStruct((mp + ms, d), BF16),
        compiler_params=_params(("arbitrary",)),
        name="rmsnorm_cast",
    )(xp, xs, g.reshape(1, d))


def _postnorm_residual_kernel(*refs, npb, pair_in, pair_out, fused):
    y_ref, refs = refs[0], refs[1:]
    first = pl.program_id(0) < npb
    if pair_in:
        x_old, refs = jnp.where(first, refs[0][...], refs[1][...]), refs[2:]
    else:
        x_old, refs = refs[0][...], refs[1:]
    g_ref, refs = refs[0], refs[1:]
    y = y_ref[...]
    x = x_old + y * lax.rsqrt(jnp.mean(y * y, axis=-1, keepdims=True) + NORM_EPS) * g_ref[...]
    if fused:
        gn_ref, refs = refs[0], refs[1:]
        refs[-1][...] = (x * lax.rsqrt(jnp.mean(x * x, axis=-1, keepdims=True) + NORM_EPS)
                         * gn_ref[...]).astype(refs[-1].dtype)
    if pair_out:
        @pl.when(first)
        def _():
            refs[0][...] = x

        @pl.when(jnp.logical_not(first))
        def _():
            refs[1][...] = x
    else:
        refs[0][...] = x


def postnorm_residual(y, x, g, mp, g_next=None, pair_out=False):
    m, d = y.shape
    pair_in = isinstance(x, tuple)
    tm = _pick_tile(math.gcd(mp, m - mp), 256, 16)
    npb = mp // tm
    row = pl.BlockSpec((tm, d), lambda i: (i, 0))
    vec = pl.BlockSpec((1, d), lambda i: (0, 0))
    fused = g_next is not None
    in_specs = [row] + (_pair_specs(tm, d, npb) if pair_in else [row]) + [vec] + ([vec] if fused else [])
    args = [y] + (list(x) if pair_in else [x]) + [g.reshape(1, d)] + ([g_next.reshape(1, d)] if fused else [])
    if pair_out:
        out_specs = _pair_specs(tm, d, npb)
        out_shape = [jax.ShapeDtypeStruct((mp, d), F32), jax.ShapeDtypeStruct((m - mp, d), F32)]
    else:
        out_specs, out_shape = [row], [jax.ShapeDtypeStruct((m, d), F32)]
    if fused:
        out_specs, out_shape = out_specs + [row], out_shape + [jax.ShapeDtypeStruct((m, d), BF16)]
    return pl.pallas_call(
        functools.partial(_postnorm_residual_kernel, npb=npb, pair_in=pair_in, pair_out=pair_out, fused=fused),
        grid=(m // tm,),
        in_specs=in_specs,
        out_specs=out_specs,
        out_shape=out_shape,
        compiler_params=_params(("arbitrary",)),
        name="postnorm_residual",
    )(*args)


def _out_proj_kernel(a_ref, w_ref, o_ref, wb_ref):
    @pl.when(pl.program_id(1) == 0)
    def _():
        wb_ref[...] = w_ref[...].astype(BF16)

    o_ref[...] = _dot(a_ref[...], wb_ref[...])


def _in_proj_kernel(a_ref, wt_ref, o_ref):
    o_ref[...] = _dot_nt(a_ref[...], wt_ref[...].astype(BF16))


def in_proj(a, wt, layer, n_main, tn_target=512, tm_target=1408):
    m, k = a.shape
    n_gate = wt.shape[1] - n_main
    tm = _pick_tile(m, tm_target, 16)
    tn = _pick_tile(n_main, tn_target, LANES)
    main = pl.pallas_call(
        _in_proj_kernel,
        grid=(m // tm, n_main // tn),
        in_specs=[pl.BlockSpec((tm, k), lambda i, j: (i, 0), pipeline_mode=pl.Buffered(1)),
                  pl.BlockSpec((None, tn, k), lambda i, j: (layer, j, 0))],
        out_specs=pl.BlockSpec((tm, tn), lambda i, j: (i, j)),
        out_shape=jax.ShapeDtypeStruct((m, n_main), F32),
        compiler_params=_params(("parallel", "arbitrary")),
        name="in_proj",
    )(a, wt)
    gates = pl.pallas_call(
        _in_proj_kernel,
        grid=(m // tm,),
        in_specs=[pl.BlockSpec((tm, k), lambda i: (i, 0)), pl.BlockSpec((n_gate, k), lambda i: (0, 0))],
        out_specs=pl.BlockSpec((tm, n_gate), lambda i: (i, 0)),
        out_shape=jax.ShapeDtypeStruct((m, n_gate), F32),
        compiler_params=_params(("parallel",)),
        name="gate_proj",
    )(a, wt[layer, n_main:, :])
    return main, gates


def out_proj(a, w, layer, tn_target=512, tm_target=1408):
    m, k = a.shape
    n = w.shape[2]
    tm = _pick_tile(m, tm_target, 16)
    tn = _pick_tile(n, tn_target, LANES)
    return pl.pallas_call(
        _out_proj_kernel,
        grid=(n // tn, m // tm),
        in_specs=[pl.BlockSpec((tm, k), lambda j, i: (i, 0)), pl.BlockSpec((None, k, tn), lambda j, i: (layer, 0, j))],
        out_specs=pl.BlockSpec((tm, tn), lambda j, i: (i, j)),
        out_shape=jax.ShapeDtypeStruct((m, n), F32),
        scratch_shapes=[pltpu.VMEM((k, tn), BF16)],
        compiler_params=_params(("parallel", "arbitrary")),
        name="out_proj",
    )(a, w)


def _logf_cumsum_kernel(f_ref, b_ref, init_ref, lf_ref, cum_ref, carry, *, apply_gate):
    @pl.when(pl.program_id(1) == 0)
    def _():
        carry[...] = init_ref[0]

    x = f_ref[...]
    if apply_gate:
        x = _log_sigmoid(x + b_ref[...])
    tb = x.shape[0]
    r = lax.broadcasted_iota(jnp.int32, (tb, tb), 0)
    c = lax.broadcasted_iota(jnp.int32, (tb, tb), 1)
    tri = (c <= r).astype(F32)
    cum = jnp.dot(tri, x, precision=HIGHEST, preferred_element_type=F32) + carry[...]
    lf_ref[...] = x
    cum_ref[...] = cum
    carry[...] = cum[tb - 1:tb, :]


def logf_cumsum(f, row_off, nseq, t, bias, init, apply_gate):
    h = f.shape[1]
    tb = _pick_tile(t, 512, 8)
    nt = t // tb
    off = row_off // tb
    assert row_off % tb == 0
    blk = pl.BlockSpec((tb, h), lambda s, i: (s * nt + i, 0))
    return pl.pallas_call(
        functools.partial(_logf_cumsum_kernel, apply_gate=apply_gate),
        grid=(nseq, nt),
        in_specs=[pl.BlockSpec((tb, h), lambda s, i: (off + s * nt + i, 0)),
                  pl.BlockSpec((1, h), lambda s, i: (0, 0)),
                  pl.BlockSpec((1, 1, h), lambda s, i: (s, 0, 0))],
        out_specs=[blk, blk],
        out_shape=[jax.ShapeDtypeStruct((nseq * t, h), F32)] * 2,
        scratch_shapes=[pltpu.VMEM((1, h), F32)],
        compiler_params=_params(("parallel", "arbitrary")),
        name="logf_cumsum",
    )(f, bias.reshape(1, h), init)


def _fox_select_matrices(h, k_side):
    sel = np.zeros((4, h, h * LANES), np.float32)
    part_lane0, ones_lane0 = (3, 0) if k_side else (0, 3)
    for hh in range(h):
        for p in range(3):
            sel[p, hh, hh * LANES + part_lane0 + p] = 1.0
            sel[3, hh, hh * LANES + ones_lane0 + p] = 1.0
    return jnp.asarray(sel.reshape(4 * h, h * LANES), BF16)


def _fox_build_kernel(x_ref, cum_ref, g_ref, sel_ref, *refs, normalize, mode, aliased, scale, negate, pitch,
                      work_step):
    v_ref = xn_ref = kout_ref = vout_ref = kscr = vscr = xscr = None
    if len(x_ref.shape) == 3:
        refs, xscr = refs[:-1], refs[-1]
    if mode == "kv":
        v_ref = refs[0]
        refs = refs[3:] if aliased else refs[1:]
        xp_ref, kout_ref, vout_ref, kscr, vscr = refs

        @pl.when(pl.program_id(1) != work_step)
        def _():
            kout_ref[...] = jnp.zeros(kout_ref.shape, kout_ref.dtype)
            vout_ref[...] = jnp.zeros(vout_ref.shape, vout_ref.dtype)
    elif mode == "norm":
        xp_ref, xn_ref = refs
    else:
        xp_ref, = refs
    pl.when(pl.program_id(1) == work_step)(functools.partial(
        _fox_build_body, x_ref, cum_ref, g_ref, sel_ref, v_ref, xp_ref, xn_ref, kout_ref, vout_ref, kscr, vscr, xscr,
        normalize=normalize, mode=mode, scale=scale, negate=negate, pitch=pitch))


def _fox_build_body(x_ref, cum_ref, g_ref, sel_ref, v_ref, xp_ref, xn_ref, kout_ref, vout_ref, kscr, vscr, xscr, *,
                    normalize, mode, scale, negate, pitch):
    tm = x_ref.shape[0]
    if xscr is not None:
        def unpack(t, _):
            for g8 in range(x_ref.shape[1] // 8):
                xscr[pl.ds(g8 * 8 * pitch + t, 8, stride=pitch), :] = x_ref[t, g8 * 8:(g8 + 1) * 8, :]
            return 0
        lax.fori_loop(0, tm, unpack, 0, unroll=8)
    c = cum_ref[...] * LOG2E
    if negate:
        c = -c
    c_hi = c.astype(BF16).astype(F32)
    r1 = c - c_hi
    c_mid = r1.astype(BF16).astype(F32)
    c_lo = (r1 - c_mid).astype(BF16).astype(F32)
    parts = jnp.concatenate([c_hi, c_mid, c_lo, jnp.ones_like(c_hi)], axis=1).astype(BF16)
    extra = _dot(parts, sel_ref[...])
    g = g_ref[...]
    for h in range(c.shape[1]):
        xs = x_ref[:, h * LANES:(h + 1) * LANES] if xscr is None else xscr[h * pitch:h * pitch + tm, :]
        if normalize:
            xs = xs * lax.rsqrt(jnp.mean(xs * xs, axis=-1, keepdims=True) + NORM_EPS) * g
        if mode == "norm":
            xn_ref[:, h * LANES:(h + 1) * LANES] = xs
        if mode == "kv":
            kscr[h * pitch:h * pitch + tm, :] = xs
            vscr[h * pitch:h * pitch + tm, :] = v_ref[:, h * LANES:(h + 1) * LANES]
        xp_ref[:, 2 * h * LANES:(2 * h + 1) * LANES] = (xs * scale).astype(BF16)
        xp_ref[:, (2 * h + 1) * LANES:(2 * h + 2) * LANES] = extra[:, h * LANES:(h + 1) * LANES].astype(BF16)
    if mode == "kv":
        def relay(t, _):
            for g8 in range(c.shape[1] // 8):
                rows8 = pl.ds(g8 * 8 * pitch + t, 8, stride=pitch)
                kout_ref[t, g8 * 8:(g8 + 1) * 8, :] = kscr[rows8, :]
                vout_ref[t, g8 * 8:(g8 + 1) * 8, :] = vscr[rows8, :]
            return 0
        lax.fori_loop(0, tm, relay, 0, unroll=8)


def fox_build(x, col_block, row_off, rows, cum, gain, k_side, normalize, scale, mode="plain", kv_bufs=None,
              n_layers=1, slot=0):
    h = cum.shape[1]
    w = h * LANES
    tm = _pick_tile(rows, 128, 16)
    assert row_off % tm == 0 and h % 8 == 0
    off = row_off // tm
    pitch = _relayout_pitch(tm)
    if x.ndim == 4:
        x_spec = pl.BlockSpec((None, tm, h, LANES), lambda i, l: (slot, off + i, 0, 0))
    else:
        x_spec = pl.BlockSpec((tm, w), lambda i, l: (off + i, col_block))
    in_specs = [x_spec,
                pl.BlockSpec((tm, h), lambda i, l: (i, 0)),
                pl.BlockSpec((1, LANES), lambda i, l: (0, 0)),
                pl.BlockSpec((4 * h, w), lambda i, l: (0, 0))]
    args = [x, cum, gain.reshape(1, LANES), _fox_select_matrices(h, k_side)]
    out_specs = [pl.BlockSpec((tm, 2 * w), lambda i, l: (i, 0))]
    out_shape = [jax.ShapeDtypeStruct((rows, 2 * w), BF16)]
    aliases = {}
    scratch = []
    slot_steps, work_step = 1, 0
    if mode == "kv":
        in_specs.append(pl.BlockSpec((tm, w), lambda i, l: (off + i, col_block + 1)))
        args.append(x)
        if kv_bufs is not None:
            in_specs += [ANY, ANY]
            args += list(kv_bufs)
            aliases = {5: 1, 6: 2}
            kv_spec = pl.BlockSpec((None, tm, h, LANES), lambda i, l: (slot, i, 0, 0))
        else:
            slot_steps, work_step = n_layers, slot
            kv_spec = pl.BlockSpec((None, tm, h, LANES), lambda i, l: (l, i, 0, 0))
        out_specs += [kv_spec, kv_spec]
        out_shape += [jax.ShapeDtypeStruct((n_layers, rows, h, LANES), F32)] * 2
        scratch = [pltpu.VMEM((h * pitch, LANES), F32)] * 2
    elif mode == "norm":
        out_specs.append(pl.BlockSpec((tm, w), lambda i, l: (i, 0)))
        out_shape.append(jax.ShapeDtypeStruct((rows, w), F32))
    if x.ndim == 4:
        scratch = scratch + [pltpu.VMEM((h * pitch, LANES), F32)]
    return pl.pallas_call(
        functools.partial(_fox_build_kernel, normalize=normalize, mode=mode, aliased=bool(aliases),
                          scale=scale, negate=k_side, pitch=pitch, work_step=work_step),
        grid=(rows // tm, slot_steps),
        in_specs=in_specs,
        out_specs=out_specs,
        out_shape=out_shape,
        scratch_shapes=scratch,
        input_output_aliases=aliases,
        compiler_params=_params(("parallel", "arbitrary")),
        name="fox_build",
    )(*args)


def _fox_attn_kernel(q_ref, k_ref, v_ref, z_ref, ybuf_ref, o_ref, vt_scr, s_a, s_b, *, tq, unroll):
    i = pl.program_id(1)

    @pl.when(i == 0)
    def _():
        vt_scr[...] = v_ref[...].astype(BF16).T

    q = q_ref[...]
    bufs = (s_a, s_b)

    def logits(j):
        return _dot_nt(k_ref[pl.ds(pl.multiple_of(j * tq, tq), tq), :], q)

    def step(u, j, carry, diagonal):
        m, l, acc = carry
        st = bufs[u % 2][...]
        if not diagonal:
            st_next = logits(j + 1)
        else:
            kidx = lax.broadcasted_iota(jnp.int32, (tq, tq), 0)
            qidx = lax.broadcasted_iota(jnp.int32, (tq, tq), 1)
            st = jnp.where(kidx <= qidx, st, NEG)
        m_new = jnp.maximum(m, jnp.max(st, axis=0, keepdims=True))
        alpha = jnp.exp2(m - m_new)
        p = jnp.exp2(st - m_new)
        l = alpha * l + jnp.sum(p, axis=0, keepdims=True)
        acc = alpha * acc + _dot(vt_scr[:, pl.ds(pl.multiple_of(j * tq, tq), tq)], p.astype(BF16))
        if not diagonal:
            bufs[(u + 1) % 2][...] = st_next
        return m_new, l, acc

    def trip(jj, carry):
        for u in range(unroll):
            carry = step(u, unroll * jj + u, carry, False)
        return carry

    def finish(rem, j0, carry):
        for u in range(rem):
            carry = step(u, j0 + u, carry, False)
        _, l, acc = step(rem, j0 + rem, carry, True)
        z = z_ref[...]
        o_ref[...] = ((acc / l).T * (z * _sigmoid(z))).astype(o_ref.dtype)

    s_a[...] = logits(0)
    carry = (jnp.full((1, tq), NEG, F32), jnp.zeros((1, tq), F32), jnp.zeros((LANES, tq), F32))
    trips = i // unroll
    carry = lax.fori_loop(0, trips, trip, carry)
    for rem in range(unroll):
        pl.when(i - trips * unroll == rem)(functools.partial(finish, rem, trips * unroll, carry))


def fox_attention_prompt(qp, kp, proj, nseq, t, nheads, ybuf):
    tq = _pick_tile(t, FOX_TQ, LANES)
    assert FOX_UNROLL % 2 == 0
    nq = t // tq
    blk = lambda g, i: (g // nheads * nq + i, g % nheads)
    return pl.pallas_call(
        functools.partial(_fox_attn_kernel, tq=tq, unroll=FOX_UNROLL),
        grid=(nseq * nheads, nq),
        in_specs=[pl.BlockSpec((tq, 2 * LANES), blk),
                  pl.BlockSpec((t, 2 * LANES), lambda g, i: (g // nheads, g % nheads)),
                  pl.BlockSpec((t, LANES), lambda g, i: (g // nheads, 2 * nheads + g % nheads)),
                  pl.BlockSpec((tq, LANES), lambda g, i: (g // nheads * nq + i, 3 * nheads + g % nheads)),
                  ANY],
        out_specs=pl.BlockSpec((tq, LANES), blk),
        out_shape=jax.ShapeDtypeStruct(ybuf.shape, ybuf.dtype),
        scratch_shapes=[pltpu.VMEM((LANES, t), BF16), pltpu.VMEM((tq, tq), F32), pltpu.VMEM((tq, tq), F32)],
        input_output_aliases={4: 0},
        compiler_params=_params(("parallel", "arbitrary")),
        name="fox_attention",
    )(qp, kp, proj, proj, ybuf)


def _fox_attn_sample_kernel(q_ref, kc_ref, kn_ref, vc_ref, vn_ref, z_ref, ybuf_ref, o_ref, vscr, *, nh, pitch):
    ts = q_ref.shape[0]
    p = vc_ref.shape[0]

    def unpack(t, _):
        vscr[pl.ds(t, nh, stride=pitch), :] = vc_ref[t]
        return 0
    lax.fori_loop(0, p, unpack, 0, unroll=8)
    r = lax.broadcasted_iota(jnp.int32, (ts, ts), 0)
    c = lax.broadcasted_iota(jnp.int32, (ts, ts), 1)
    for u in range(nh):
        q = q_ref[:, 2 * u * LANES:2 * (u + 1) * LANES]
        s_c = _dot_nt(q, kc_ref[:, 2 * u * LANES:2 * (u + 1) * LANES])
        s_n = jnp.where(c <= r, _dot_nt(q, kn_ref[:, 2 * u * LANES:2 * (u + 1) * LANES]), NEG)
        m = jnp.maximum(jnp.max(s_c, axis=-1, keepdims=True), jnp.max(s_n, axis=-1, keepdims=True))
        p_c = jnp.exp2(s_c - m)
        p_n = jnp.exp2(s_n - m)
        l = jnp.sum(p_c, axis=-1, keepdims=True) + jnp.sum(p_n, axis=-1, keepdims=True)
        acc = (_dot(p_c.astype(BF16), vscr[u * pitch:u * pitch + p, :].astype(BF16))
               + _dot(p_n.astype(BF16), vn_ref[:, u * LANES:(u + 1) * LANES].astype(BF16)))
        z = z_ref[:, u * LANES:(u + 1) * LANES]
        o_ref[:, u * LANES:(u + 1) * LANES] = (acc / l * (z * _sigmoid(z))).astype(o_ref.dtype)


def fox_attention_sample(qp, kp_c, kp_n, cache_v, slot, proj, row_off, nseq, ts, p, nheads, ybuf):
    nh = FOX_HEADS_PER_STEP
    ng = nheads // nh
    assert row_off % ts == 0 and nheads % nh == 0
    off = row_off // ts
    pitch = _relayout_pitch(p)
    return pl.pallas_call(
        functools.partial(_fox_attn_sample_kernel, nh=nh, pitch=pitch),
        grid=(nseq, ng),
        in_specs=[pl.BlockSpec((ts, 2 * nh * LANES), lambda s, g: (s, g)),
                  pl.BlockSpec((p, 2 * nh * LANES), lambda s, g: (s, g)),
                  pl.BlockSpec((ts, 2 * nh * LANES), lambda s, g: (s, g)),
                  pl.BlockSpec((None, p, nh, LANES), lambda s, g: (slot, s, g, 0)),
                  pl.BlockSpec((ts, nh * LANES), lambda s, g: (off + s, 2 * ng + g)),
                  pl.BlockSpec((ts, nh * LANES), lambda s, g: (off + s, 3 * ng + g)),
                  ANY],
        out_specs=pl.BlockSpec((ts, nh * LANES), lambda s, g: (off + s, g)),
        out_shape=jax.ShapeDtypeStruct(ybuf.shape, ybuf.dtype),
        scratch_shapes=[pltpu.VMEM((nh * pitch, LANES), F32)],
        input_output_aliases={6: 0},
        compiler_params=_params(("parallel", "arbitrary")),
        name="fox_attention_sample",
    )(qp, kp_c, kp_n, cache_v, proj, proj, ybuf)


def fox_layer(h, mp_rows, b, t, bs, ts, cache_k, cache_v, cache_lf, w_in, slot, b_f, q_gain, k_gain, kv_bufs):
    nheads = b_f.shape[0]
    w = nheads * LANES
    n_layers, _, p = cache_k.shape[:3]
    qscale = float(LANES) ** -0.5 * LOG2E
    wt = jnp.swapaxes(w_in, 1, 2)
    proj, fgate = in_proj(h, wt, slot, 4 * w)
    ybuf = jnp.zeros((h.shape[0], w), BF16)
    lf_p, cum_p = logf_cumsum(fgate, 0, b, t, b_f, jnp.zeros((b, 1, nheads), F32), True)
    qp_p, = fox_build(proj, 0, 0, mp_rows, cum_p, q_gain, False, True, qscale)
    kp_p, kbuf, vbuf = fox_build(proj, 1, 0, mp_rows, cum_p, k_gain, True, True, 1.0, "kv", kv_bufs, n_layers, slot)
    ybuf = fox_attention_prompt(qp_p, kp_p, proj, b, t, nheads, ybuf)
    ms_rows = bs * ts
    _, cum_c = logf_cumsum(cache_lf.reshape(bs * p, nheads), 0, bs, p, b_f, jnp.zeros((bs, 1, nheads), F32), False)
    init_s = cum_c.reshape(bs, p, nheads)[:, p - 1:, :]
    lf_s, cum_s = logf_cumsum(fgate, mp_rows, bs, ts, b_f, init_s, True)
    qp_s, = fox_build(proj, 0, mp_rows, ms_rows, cum_s, q_gain, False, True, qscale)
    kp_s, kn_s = fox_build(proj, 1, mp_rows, ms_rows, cum_s, k_gain, True, True, 1.0, "norm")
    by_head = (n_layers, bs * p, nheads, LANES)
    kp_c, = fox_build(cache_k.reshape(by_head), 0, 0, bs * p, cum_c, k_gain, True, False, 1.0, slot=slot)
    ybuf = fox_attention_sample(qp_s, kp_c, kp_s, cache_v.reshape(by_head), slot, proj, mp_rows, bs, ts, p,
                                nheads, ybuf)
    v_s = proj[mp_rows:, 2 * w:3 * w]
    hd = LANES
    outs = (lf_p.reshape(b, t, nheads), kn_s.reshape(bs, ts, nheads, hd), v_s.reshape(bs, ts, nheads, hd),
            lf_s.reshape(bs, ts, nheads))
    return ybuf, (kbuf, vbuf), outs


def _mlstm_kernel(q_ref, k_ref, v_ref, o_ref, z_ref, gate_ref, bi_ref, bf_ref, gain_ref, expand_ref,
                  ct0_ref, n0_ref, m0_ref, ybuf_ref, y_ref, ct_ref, n_ref, m_ref, *, nheads, dk, dv):
    @pl.when(pl.program_id(1) == 0)
    def _():
        ct_ref[...] = ct0_ref[...]
        n_ref[...] = n0_ref[...]
        m_ref[...] = m0_ref[...]

    ln = q_ref.shape[0]
    pack = MXU_DIM // ln
    ngrp = nheads // pack
    rr = pack * ln
    lsh = ln.bit_length() - 1
    gw = pack * dv
    gates = gate_ref[...]
    log_i = gates[:, :nheads] + bi_ref[...]
    log_f = _log_sigmoid(gates[:, nheads:] + bf_ref[...])
    rl = lax.broadcasted_iota(jnp.int32, (ln, ln), 0)
    cl = lax.broadcasted_iota(jnp.int32, (ln, ln), 1)
    b_all = jnp.dot((cl <= rl).astype(F32), log_f, precision=HIGHEST, preferred_element_type=F32)
    m_row = m_ref[0, :, 0:nheads]
    b_last = b_all[ln - 1:ln, :]
    g_all = b_last - b_all + log_i
    m_new = jnp.maximum(b_last + m_row, jnp.max(g_all, axis=0, keepdims=True))
    wk_all = jnp.exp(g_all - m_new)
    carry_w = jnp.exp(b_last + m_row - m_new)
    carry_wide = jnp.dot(carry_w, expand_ref[...], precision=HIGHEST, preferred_element_type=F32)
    rh = lax.broadcasted_iota(jnp.int32, (nheads, nheads), 0)
    ch = lax.broadcasted_iota(jnp.int32, (nheads, nheads), 1)
    carry_col = jnp.sum(jnp.where(rh == ch, carry_w, 0.0), axis=1, keepdims=True)

    def stack_cols(a):
        return jnp.stack([jnp.concatenate([a[:, g * pack + p:g * pack + p + 1] for p in range(pack)], axis=0)
                          for g in range(ngrp)], axis=0)

    def stack_heads(ref, width):
        return jnp.stack([jnp.concatenate([ref[:, (g * pack + p) * width:(g * pack + p + 1) * width]
                                           for p in range(pack)], axis=0) for g in range(ngrp)], axis=0)

    bcol = stack_cols(b_all)
    licol = stack_cols(log_i)
    mprev = stack_cols(jnp.broadcast_to(m_row, (ln, nheads)))
    wkcol = stack_cols(wk_all)
    r = lax.broadcasted_iota(jnp.int32, (rr, rr), 0)
    c = lax.broadcasted_iota(jnp.int32, (rr, rr), 1)
    incl = ((r >> lsh) == (c >> lsh)) & (c <= r)
    brow = jnp.sum(jnp.where(r == c, bcol, 0.0), axis=1, keepdims=True)
    lirow = jnp.sum(jnp.where(r == c, licol, 0.0), axis=1, keepdims=True)
    d = jnp.where(incl, bcol - brow + lirow, -jnp.inf)
    inter = bcol + mprev
    m_t = jnp.maximum(inter, jnp.max(d, axis=-1, keepdims=True))
    qf = stack_heads(q_ref, dk)
    kf = stack_heads(k_ref, dk) * (float(dk) ** -0.5)
    vf = stack_heads(v_ref, dv)
    qb, kb, vb = qf.astype(BF16), kf.astype(BF16), vf.astype(BF16)
    wmat = jnp.exp(d - m_t) * _bmm_nt(qb, kb)
    inter_w = jnp.exp(inter - m_t)
    ctb = ct_ref[0].astype(BF16)
    rowblk = lax.broadcasted_iota(jnp.int32, (rr, 1), 0) >> lsh

    def own_block(wide):
        out = jnp.where(rowblk == 0, wide[:, :dv], 0.0)
        for p in range(1, pack):
            out = out + jnp.where(rowblk == p, wide[:, p * dv:(p + 1) * dv], 0.0)
        return out

    qc = jnp.stack([own_block(_dot(qb[g], ctb[:, g * gw:(g + 1) * gw])) for g in range(ngrp)], axis=0)
    num = _bmm(wmat.astype(BF16), vb) + inter_w * qc
    nst = jnp.stack([jnp.concatenate([jnp.broadcast_to(n_ref[0, g * pack + p:g * pack + p + 1, :], (ln, dk))
                                      for p in range(pack)], axis=0) for g in range(ngrp)], axis=0)
    den = jnp.sum(wmat, axis=-1, keepdims=True) + inter_w * jnp.sum(qf * nst, axis=-1, keepdims=True)
    hid = num / jnp.maximum(jnp.abs(den), jnp.exp(-m_t))
    own = ((lax.broadcasted_iota(jnp.int32, (rr, gw), 0) >> lsh)
           == (lax.broadcasted_iota(jnp.int32, (rr, gw), 1) >> (dv.bit_length() - 1)))
    blocksum = ((lax.broadcasted_iota(jnp.int32, (pack, rr), 1) >> lsh)
                == lax.broadcasted_iota(jnp.int32, (pack, rr), 0)).astype(F32)
    for g in range(ngrp):
        wv = (wkcol[g] * vf[g]).astype(BF16)
        vwbd = jnp.where(own, jnp.concatenate([wv] * pack, axis=1), jnp.zeros((), BF16))
        cols = slice(g * gw, (g + 1) * gw)
        ct_ref[0, :, cols] = carry_wide[:, cols] * ct_ref[0, :, cols] + _dot_tn(kb[g], vwbd)
        heads = slice(g * pack, (g + 1) * pack)
        ksum = jnp.dot(blocksum, wkcol[g] * kf[g], precision=HIGHEST, preferred_element_type=F32)
        n_ref[0, heads, :] = carry_col[heads, :] * n_ref[0, heads, :] + ksum
        for p in range(pack):
            h = g * pack + p
            hh = hid[g, p * ln:(p + 1) * ln, :]
            hn = hh * lax.rsqrt(jnp.mean(hh * hh, axis=-1, keepdims=True) + NORM_EPS)
            hn = hn * gain_ref[:, h * dv:(h + 1) * dv]
            og = o_ref[:, h * dv:(h + 1) * dv]
            zg = z_ref[:, h * dv:(h + 1) * dv]
            y_ref[:, h * dv:(h + 1) * dv] = (hn * _sigmoid(og) * (zg * _sigmoid(zg))).astype(y_ref.dtype)
    m_ref[0, :, 0:nheads] = m_new


def mlstm_scan(proj, gates, row_off, nseq, t, ln, c0, n0, m0, b_i, b_f, gain, ybuf):
    nheads, dv, dk = c0.shape[1:]
    nc = t // ln
    assert row_off % ln == 0
    off = row_off // ln
    qk_w, v_w = nheads * dk, nheads * dv
    assert v_w == 2 * qk_w

    def rows(width, cb):
        return pl.BlockSpec((ln, width), lambda s, i: (off + s * nc + i, cb))

    assert MXU_DIM % ln == 0 and nheads % (MXU_DIM // ln) == 0 and nheads <= LANES
    ct0 = jnp.transpose(c0, (0, 3, 1, 2)).reshape(nseq, dk, v_w)
    m0p = jnp.pad(m0[:, None, :], ((0, 0), (0, 0), (0, LANES - nheads)))
    expand = jnp.asarray(np.kron(np.eye(nheads, dtype=np.float32), np.ones((1, dv), np.float32)))
    state = [pl.BlockSpec((1, dk, v_w), lambda s, i: (s, 0, 0)),
             pl.BlockSpec((1, nheads, dk), lambda s, i: (s, 0, 0)),
             pl.BlockSpec((1, 1, LANES), lambda s, i: (s, 0, 0))]
    y, ct, n, m = pl.pallas_call(
        functools.partial(_mlstm_kernel, nheads=nheads, dk=dk, dv=dv),
        grid=(nseq, nc),
        in_specs=[rows(qk_w, 0), rows(qk_w, 1), rows(v_w, 1), rows(v_w, 2), rows(v_w, 3),
                  rows(2 * nheads, 0),
                  pl.BlockSpec((1, nheads), lambda s, i: (0, 0)),
                  pl.BlockSpec((1, nheads), lambda s, i: (0, 0)),
                  pl.BlockSpec((1, v_w), lambda s, i: (0, 0)),
                  pl.BlockSpec((nheads, v_w), lambda s, i: (0, 0))] + state + [ANY],
        out_specs=[rows(v_w, 0)] + state,
        out_shape=[jax.ShapeDtypeStruct(ybuf.shape, ybuf.dtype),
                   jax.ShapeDtypeStruct((nseq, dk, v_w), F32),
                   jax.ShapeDtypeStruct((nseq, nheads, dk), F32),
                   jax.ShapeDtypeStruct((nseq, 1, LANES), F32)],
        input_output_aliases={13: 0},
        compiler_params=_params(("parallel", "arbitrary")),
        name="mlstm_scan",
    )(proj, proj, proj, proj, proj, gates, b_i.reshape(1, nheads), b_f.reshape(1, nheads),
      gain.reshape(1, v_w), expand, ct0, n0, m0p, ybuf)
    c = jnp.transpose(ct.reshape(nseq, dk, nheads, dv), (0, 2, 3, 1))
    return y, c, n, m[:, 0, :nheads]


def mlstm_layer(h, mp_rows, b, t, bs, ts, c0, n0, m0, w_in, slot, b_i, b_f, gain):
    nheads, dv, dk = c0.shape[1:]
    main = 2 * nheads * dk + 3 * nheads * dv
    wt = jnp.swapaxes(w_in, 1, 2)
    proj, gates = in_proj(h, wt, slot, main)
    ybuf = jnp.zeros((h.shape[0], nheads * dv), BF16)
    zc = jnp.zeros((b, nheads, dv, dk), F32)
    ybuf, c_p, n_p, m_p = mlstm_scan(proj, gates, 0, b, t, min(CHUNK, t), zc, zc[:, :, 0, :], zc[:, :, 0, 0],
                                     b_i, b_f, gain, ybuf)
    ybuf, c_s, n_s, m_s = mlstm_scan(proj, gates, mp_rows, bs, ts, ts, c0, n0, m0, b_i, b_f, gain, ybuf)
    return ybuf, (c_p, n_p, m_p, c_s, n_s, m_s)


def _gdn_conv_kernel(x_ref, halo_ref, init_ref, w_ref, sc_ref, o_ref, *, l2norm):
    tm = x_ref.shape[0]
    prev = jnp.where(pl.program_id(1) == 0, init_ref[0], halo_ref[...])
    xc = jnp.concatenate([prev, x_ref[...]], axis=0)
    w = w_ref[...]
    acc = w[3:4] * xc[8:8 + tm] + w[2:3] * xc[7:7 + tm] + w[1:2] * xc[6:6 + tm] + w[0:1] * xc[5:5 + tm]
    act = acc * _sigmoid(acc)
    if l2norm:
        for g in range(act.shape[1] // LANES):
            a = act[:, g * LANES:(g + 1) * LANES]
            a = a * lax.rsqrt(jnp.sum(a * a, axis=-1, keepdims=True) + NORM_EPS)
            o_ref[:, g * LANES:(g + 1) * LANES] = (a * sc_ref[:, g * LANES:(g + 1) * LANES]).astype(o_ref.dtype)
    else:
        o_ref[...] = act.astype(o_ref.dtype)


def gdn_conv(proj, row_off, nseq, t, ch_off, ch, init, conv_w, slot, scale, l2norm, out_dtype):
    tm = _pick_tile(t, 512, 8)
    tc = _pick_tile(ch, 1024, LANES)
    nt = t // tm
    assert row_off % tm == 0 and ch_off % tc == 0 and tm % 8 == 0
    off, coff, sub = row_off // tm, ch_off // tc, tm // 8
    return pl.pallas_call(
        functools.partial(_gdn_conv_kernel, l2norm=l2norm),
        grid=(nseq, nt, ch // tc),
        in_specs=[pl.BlockSpec((tm, tc), lambda s, i, j: (off + s * nt + i, coff + j)),
                  pl.BlockSpec((8, tc), lambda s, i, j: (jnp.maximum((off + s * nt + i) * sub - 1, 0), coff + j)),
                  pl.BlockSpec((1, 8, tc), lambda s, i, j: (s, 0, coff + j)),
                  pl.BlockSpec((None, conv_w.shape[1], tc), lambda s, i, j: (slot, 0, coff + j)),
                  pl.BlockSpec((1, tc), lambda s, i, j: (0, j))],
        out_specs=pl.BlockSpec((tm, tc), lambda s, i, j: (s * nt + i, j)),
        out_shape=jax.ShapeDtypeStruct((nseq * t, ch), out_dtype),
        compiler_params=_params(("parallel", "arbitrary", "arbitrary")),
        name="gdn_conv",
    )(proj, proj, init, conv_w, scale)


def _unit_lower_inverse(a, r, c, levels):
    x = jnp.where(r == c, 1.0, 0.0) - jnp.where(((r >> 1) == (c >> 1)) & (c < r), a, 0.0)
    for sh in range(1, levels):
        off = jnp.where(((r >> (sh + 1)) == (c >> (sh + 1))) & ((r >> sh) != (c >> sh)) & (c < r), a, 0.0)
        xb = x.astype(BF16)
        x = x - _bmm(_bmm(xb, off.astype(BF16)).astype(BF16), xb)
    return x


def _gdn_kernel(qk_ref, v_ref, z_ref, gate_ref, alog_ref, dt_ref, gain_ref, expand_ref, ws0_ref, ybuf_ref,
                y_ref, ws_ref, *, kheads, vheads):
    @pl.when(pl.program_id(1) == 0)
    def _():
        ws_ref[...] = ws0_ref[...]

    ln = v_ref.shape[0]
    hd = LANES
    pack = MXU_DIM // ln
    ngrp = vheads // pack
    rr = pack * ln
    lsh = ln.bit_length() - 1
    rep = vheads // kheads
    gates = gate_ref[...]
    g_all = -jnp.exp(alog_ref[...]) * _softplus(gates[:, :vheads] + dt_ref[...])
    beta_all = _sigmoid(gates[:, vheads:])
    rl = lax.broadcasted_iota(jnp.int32, (ln, ln), 0)
    cl = lax.broadcasted_iota(jnp.int32, (ln, ln), 1)
    gsum_all = jnp.dot((cl <= rl).astype(F32), g_all, precision=HIGHEST, preferred_element_type=F32)
    glast_row = gsum_all[ln - 1:ln, :]
    decay_row = jnp.exp(jnp.dot(glast_row, expand_ref[...], precision=HIGHEST, preferred_element_type=F32))

    def stack_cols(a):
        return jnp.stack([jnp.concatenate([a[:, g * pack + p:g * pack + p + 1] for p in range(pack)], axis=0)
                          for g in range(ngrp)], axis=0)

    def stack_heads(ref, head_of):
        return jnp.stack([jnp.concatenate([ref[:, head_of(g * pack + p) * hd:(head_of(g * pack + p) + 1) * hd]
                                           for p in range(pack)], axis=0) for g in range(ngrp)], axis=0)

    gcol = stack_cols(gsum_all)
    bt = stack_cols(beta_all)
    glast = stack_cols(jnp.broadcast_to(glast_row, (ln, vheads)))
    r = lax.broadcasted_iota(jnp.int32, (rr, rr), 0)
    c = lax.broadcasted_iota(jnp.int32, (rr, rr), 1)
    blk = (r >> lsh) == (c >> lsh)
    incl = blk & (c <= r)
    strict = blk & (c < r)
    grow = jnp.sum(jnp.where(r == c, gcol, 0.0), axis=1, keepdims=True)
    dincl = jnp.where(incl, jnp.exp(jnp.where(incl, gcol - grow, 0.0)), 0.0)
    qst = stack_heads(qk_ref, lambda h: h // rep)
    kst = stack_heads(qk_ref, lambda h: kheads + h // rep)
    vst = stack_heads(v_ref, lambda h: h)
    kk = _bmm_nt(kst, kst)
    qk = _bmm_nt(qst, kst)
    x = _unit_lower_inverse(jnp.where(strict, bt * dincl * kk, 0.0), r, c, lsh)
    eg = jnp.exp(gcol)
    wsb = ws_ref[0].astype(BF16)
    gw = pack * hd
    rowblk = lax.broadcasted_iota(jnp.int32, (rr, 1), 0) >> lsh

    def own_block(wide):
        out = jnp.where(rowblk == 0, wide[:, :hd], 0.0)
        for p in range(1, pack):
            out = out + jnp.where(rowblk == p, wide[:, p * hd:(p + 1) * hd], 0.0)
        return out

    ksd = jnp.stack([own_block(_dot(kst[g], wsb[:, g * gw:(g + 1) * gw])) for g in range(ngrp)], axis=0)
    qsd = jnp.stack([own_block(_dot(qst[g], wsb[:, g * gw:(g + 1) * gw])) for g in range(ngrp)], axis=0)
    rhs = bt * (vst - eg * ksd)
    u = _bmm(x.astype(BF16), rhs.astype(BF16))
    o = eg * qsd + _bmm((dincl * qk).astype(BF16), u.astype(BF16))
    uw = (jnp.exp(glast - gcol) * u).astype(BF16)
    own = ((lax.broadcasted_iota(jnp.int32, (rr, gw), 0) >> lsh)
           == (lax.broadcasted_iota(jnp.int32, (rr, gw), 1) >> (hd.bit_length() - 1)))
    gain = gain_ref[...]
    for g in range(ngrp):
        uwbd = jnp.where(own, jnp.concatenate([uw[g]] * pack, axis=1), jnp.zeros((), BF16))
        cols = slice(g * gw, (g + 1) * gw)
        ws_ref[0, :, cols] = decay_row[:, cols] * ws_ref[0, :, cols] + _dot_tn(kst[g], uwbd)
        for p in range(pack):
            h = g * pack + p
            oh = o[g, p * ln:(p + 1) * ln, :]
            on = oh * lax.rsqrt(jnp.mean(oh * oh, axis=-1, keepdims=True) + NORM_EPS) * gain
            zg = z_ref[:, h * hd:(h + 1) * hd]
            y_ref[:, h * hd:(h + 1) * hd] = (on * (zg * _sigmoid(zg))).astype(y_ref.dtype)


def gdn_scan(qk, v, proj, z_col_block, gates, row_off, nseq, t, ln, s0, a_log, dt_bias, gain, ybuf):
    vheads, dv, dk = s0.shape[1:]
    v_w = vheads * LANES
    kheads = qk.shape[1] // (2 * LANES)
    nc = t // ln
    assert row_off % ln == 0 and MXU_DIM % ln == 0 and vheads % (MXU_DIM // ln) == 0 and dv == LANES and dk == LANES
    off = row_off // ln
    ws0 = jnp.transpose(s0, (0, 3, 1, 2)).reshape(nseq, dk, v_w)
    expand = jnp.asarray(np.kron(np.eye(vheads, dtype=np.float32), np.ones((1, LANES), np.float32)))
    local = lambda width: pl.BlockSpec((ln, width), lambda s, i: (s * nc + i, 0))
    state = pl.BlockSpec((1, dk, v_w), lambda s, i: (s, 0, 0))
    vec = lambda width: pl.BlockSpec((1, width), lambda s, i: (0, 0))
    ybuf, ws = pl.pallas_call(
        functools.partial(_gdn_kernel, kheads=kheads, vheads=vheads),
        grid=(nseq, nc),
        in_specs=[local(qk.shape[1]), local(v_w),
                  pl.BlockSpec((ln, v_w), lambda s, i: (off + s * nc + i, z_col_block)),
                  pl.BlockSpec((ln, 2 * vheads), lambda s, i: (off + s * nc + i, 0)),
                  vec(vheads), vec(vheads), vec(LANES),
                  pl.BlockSpec((vheads, v_w), lambda s, i: (0, 0)), state, ANY],
        out_specs=[pl.BlockSpec((ln, v_w), lambda s, i: (off + s * nc + i, 0)), state],
        out_shape=[jax.ShapeDtypeStruct(ybuf.shape, ybuf.dtype),
                   jax.ShapeDtypeStruct((nseq, dk, v_w), F32)],
        input_output_aliases={9: 0},
        compiler_params=_params(("parallel", "arbitrary")),
        name="gdn_scan",
    )(qk, v, proj, gates, a_log.reshape(1, vheads), dt_bias.reshape(1, vheads), gain.reshape(1, LANES),
      expand, ws0, ybuf)
    return ybuf, jnp.transpose(ws.reshape(nseq, dk, vheads, dv), (0, 2, 3, 1))


def gdn_layer(h, mp_rows, b, t, bs, ts, s0, conv0, w_in, slot, conv_w, a_log, dt_bias, gain):
    vheads = s0.shape[1]
    v_w = vheads * LANES
    taps, ch = conv_w.shape[1:]
    qk_w = ch - v_w
    wt = jnp.swapaxes(w_in, 1, 2)
    proj, gates = in_proj(h, wt, slot, ch + v_w)
    scale = jnp.concatenate([jnp.full((1, qk_w // 2), float(LANES) ** -0.5, F32), jnp.ones((1, qk_w // 2), F32)], axis=1)
    ones = jnp.ones((1, v_w), F32)
    ybuf = jnp.zeros((h.shape[0], v_w), BF16)
    outs = []
    for row_off, nseq, tt, ln, init_rows, st0 in (
            (0, b, t, min(CHUNK, t), jnp.zeros((b, taps - 1, ch), F32), jnp.zeros((b,) + s0.shape[1:], F32)),
            (mp_rows, bs, ts, ts, conv0, s0)):
        init = jnp.pad(init_rows, ((0, 0), (8 - (taps - 1), 0), (0, 0)))
        qk = gdn_conv(proj, row_off, nseq, tt, 0, qk_w, init, conv_w, slot, scale, True, BF16)
        v = gdn_conv(proj, row_off, nseq, tt, qk_w, v_w, init, conv_w, slot, ones, False, F32)
        ybuf, s_new = gdn_scan(qk, v, proj, ch // v_w, gates, row_off, nseq, tt, ln, st0, a_log[slot], dt_bias[slot],
                               gain[slot], ybuf)
        conv_state = jnp.stack([proj[row_off + (s + 1) * tt - (taps - 1):row_off + (s + 1) * tt, :ch]
                                for s in range(nseq)])
        outs += [s_new, conv_state]
    return ybuf, tuple(outs)


def kernel(x_prompt, x_sample, cache_fox_k, cache_fox_v, cache_fox_logf, state_mlstm_c, state_mlstm_n, state_mlstm_m, state_gdn_s, state_gdn_conv, pre_norm, post_norm, fox_w_in, fox_b_f, fox_q_norm, fox_k_norm, fox_w_out, mlstm_w_in, mlstm_b_i, mlstm_b_f, mlstm_h_norm, mlstm_w_out, gdn_w_in, gdn_conv_w, gdn_a_log, gdn_dt_bias, gdn_o_norm, gdn_w_out):
    b, t, d = x_prompt.shape
    bs, ts, _ = x_sample.shape
    mp_rows = b * t
    n_fox = fox_w_in.shape[0]
    x = (x_prompt.reshape(mp_rows, d), x_sample.reshape(bs * ts, d))
    kv_bufs = None
    fox, ml, gd = [], [], []
    depth = pre_norm.shape[0]
    h = rmsnorm_cast(x[0], x[1], pre_norm[0])
    for layer in range(depth):
        kind, j = layer % 3, layer // 3
        if kind == 0:
            y, kv_bufs, outs = fox_layer(h, mp_rows, b, t, bs, ts, cache_fox_k, cache_fox_v, cache_fox_logf[j],
                                         fox_w_in, j, fox_b_f[j], fox_q_norm[j], fox_k_norm[j], kv_bufs)
            fox.append(outs)
            w_out = fox_w_out
        elif kind == 1:
            y, outs = mlstm_layer(h, mp_rows, b, t, bs, ts, state_mlstm_c[j], state_mlstm_n[j], state_mlstm_m[j],
                                  mlstm_w_in, j, mlstm_b_i[j], mlstm_b_f[j], mlstm_h_norm[j])
            ml.append(outs)
            w_out = mlstm_w_out
        else:
            y, outs = gdn_layer(h, mp_rows, b, t, bs, ts, state_gdn_s[j], state_gdn_conv[j], gdn_w_in, j,
                                gdn_conv_w, gdn_a_log, gdn_dt_bias, gdn_o_norm)
            gd.append(outs)
            w_out = gdn_w_out
        mixed = out_proj(y, w_out, j)
        if layer + 1 < depth:
            x, h = postnorm_residual(mixed, x, post_norm[layer], mp_rows, g_next=pre_norm[layer + 1])
        else:
            x = postnorm_residual(mixed, x, post_norm[layer], mp_rows, pair_out=True)
    stack = lambda group, idx: jnp.stack([o[idx] for o in group])
    nh = fox_b_f.shape[1]
    kv_shape = (n_fox, b, t, nh, LANES)
    return ((x[0].reshape(b, t, d), x[1].reshape(bs, ts, d),
             kv_bufs[0].reshape(kv_shape), kv_bufs[1].reshape(kv_shape))
            + tuple(stack(fox, i) for i in range(4))
            + tuple(stack(ml, i) for i in range(6))
            + tuple(stack(gd, i) for i in range(4)))
```
